```python
import jax, jax.numpy as jnp
from jax import lax
import numpy as np

D_MODEL = 4096
BATCH = 1
SEQ = 16384
DEPTH = 1

GRID_W = 64
CTX_LEN = 256
GDN_HEADS = 16
GDN_HEAD_DIM = 128
GDN_WIDTH = GDN_HEADS * GDN_HEAD_DIM
GDN_CHUNK = 64
QKV_CONV = 3
CONV_WIDTH = D_MODEL // 2
SHORT_CONV = 3
N_GROUPS = 8
EXPERTS_PER_GROUP = 8
N_EXPERTS = N_GROUPS * EXPERTS_PER_GROUP
TOP_K_IN_GROUP = 2
EXPERT_HIDDEN = 768
ROUTE_BLOCK = 128
N_MOD = 6
IN_COLS = 4 * GDN_WIDTH + 4 * GDN_HEADS + 3 * CONV_WIDTH + 2 * D_MODEL
DEEPNORM_ALPHA = (2 * DEPTH) ** 0.25
DEEPNORM_BETA = (8 * DEPTH) ** -0.25
LN_EPS = 1e-6
RMS_EPS = 1e-6

kernel_name = "hybrid_gdn_shortconv_hmoe_diffusion_block"


def layer_norm(x, gain=None, bias=None):
    xf = x.astype(jnp.float32)
    mu = jnp.mean(xf, axis=-1, keepdims=True)
    var = jnp.mean(jnp.square(xf - mu), axis=-1, keepdims=True)
    y = (xf - mu) * lax.rsqrt(var + LN_EPS)
    if gain is not None:
        y = y * gain.astype(jnp.float32) + bias.astype(jnp.float32)
    return y.astype(x.dtype)


def modulate(h, shift, scale):
    return h * (1 + scale) + shift


def l2_normalize(t):
    return t * lax.rsqrt(jnp.sum(t * t, axis=-1, keepdims=True) + 1e-6)


def dwconv_centred(x, w):
    k = w.shape[0]
    pad = k // 2
    n = x.shape[-2]
    xp = jnp.pad(x, [(0, 0)] * (x.ndim - 2) + [(pad, pad), (0, 0)])
    return sum(w[i] * xp[..., i:i + n, :] for i in range(k))


def conv_latent(x, w):
    b, n, ch = x.shape
    rows = n // GRID_W
    return dwconv_centred(x.reshape(b, rows, GRID_W, ch), w).reshape(b, n, ch)


def split_projection(p):
    sizes = (3 * GDN_WIDTH, GDN_WIDTH, 2 * GDN_HEADS, 2 * GDN_HEADS, 3 * CONV_WIDTH, 2 * D_MODEL)
    return jnp.split(p, np.cumsum(sizes)[:-1].tolist(), axis=-1)


def gdn_inputs(qkv, beta_raw, a_raw, conv_fn, conv_qkv, a_log, dt_bias):
    b, n, _ = qkv.shape
    qkv = jax.nn.silu(conv_fn(qkv, conv_qkv)).astype(jnp.float32)
    q, k, v = (t.reshape(b, n, GDN_HEADS, GDN_HEAD_DIM) for t in jnp.split(qkv, 3, axis=-1))
    q = l2_normalize(q) * GDN_HEAD_DIM ** -0.5
    k = l2_normalize(k)
    beta = jax.nn.sigmoid(beta_raw.astype(jnp.float32)).reshape(b, n, 2, GDN_HEADS)
    a = a_raw.astype(jnp.float32).reshape(b, n, 2, GDN_HEADS)
    g = -jnp.exp(a_log.astype(jnp.float32)) * jax.nn.softplus(a + dt_bias.astype(jnp.float32))
    return q, k, v, g, beta


def gdn_chunked(q, k, v, g, beta, s0):
    b, n, h, _ = q.shape
    nc = n // GDN_CHUNK

    def chunks(t):
        return jnp.moveaxis(t.reshape((b, nc, GDN_CHUNK, h) + t.shape[3:]), 3, 1)

    q, k, v, g, beta = (chunks(t) for t in (q, k, v, g, beta))
    g = jnp.cumsum(g, axis=-1)
    incl = jnp.tril(jnp.ones((GDN_CHUNK, GDN_CHUNK), bool))
    strict = jnp.tril(jnp.ones((GDN_CHUNK, GDN_CHUNK), bool), -1)
    decay = jnp.exp(jnp.where(incl, g[..., :, None] - g[..., None, :], -jnp.inf))
    kb = k * beta[..., None]
    lmat = jnp.where(strict, jnp.einsum('bhnid,bhnjd->bhnij', kb, k) * decay, 0.0)
    amat = lmat + jnp.eye(GDN_CHUNK, dtype=jnp.float32)
    u = lax.linalg.triangular_solve(amat, v * beta[..., None], left_side=True, lower=True, unit_diagonal=True)
    w = lax.linalg.triangular_solve(amat, kb * jnp.exp(g)[..., None], left_side=True, lower=True, unit_diagonal=True)
    attn = jnp.einsum('bhnid,bhnjd->bhnij', q, k) * decay
    qg = q * jnp.exp(g)[..., None]
    kd = k * jnp.exp(g[..., -1:] - g)[..., None]
    g_last = jnp.exp(g[..., -1])
    xs = tuple(jnp.moveaxis(t, 2, 0) for t in (u, w, attn, qg, kd, g_last))

    def step(s, xc):
        u_c, w_c, attn_c, qg_c, kd_c, gl_c = xc
        v_new = u_c - jnp.einsum('bhck,bhkv->bhcv', w_c, s)
        o_c = jnp.einsum('bhck,bhkv->bhcv', qg_c, s) + jnp.einsum('bhij,bhjv->bhiv', attn_c, v_new)
        s = s * gl_c[..., None, None] + jnp.einsum('bhck,bhcv->bhkv', kd_c, v_new)
        return s, o_c

    s_final, o = lax.scan(step, s0, xs)
    o = jnp.transpose(o, (1, 0, 3, 2, 4)).reshape(b, n, h, v.shape[-1])
    return o, s_final


def gdn_bidirectional(q, k, v, g, beta, s0):
    b = q.shape[0]
    both = lambda t: jnp.concatenate([t, jnp.flip(t, axis=1)], axis=0)
    per_dir = lambda t: jnp.concatenate([t[:, :, 0], jnp.flip(t[:, :, 1], axis=1)], axis=0)
    o, s = gdn_chunked(both(q), both(k), both(v), per_dir(g), per_dir(beta), s0)
    return o[:b] + jnp.flip(o[b:], axis=1), s


def gated_rms_norm(o, z, w):
    zf = z.astype(jnp.float32).reshape(o.shape)
    y = o * lax.rsqrt(jnp.mean(o * o, axis=-1, keepdims=True) + RMS_EPS) * w.astype(jnp.float32) * jax.nn.silu(zf)
    return y.reshape(o.shape[0], o.shape[1], GDN_WIDTH).astype(z.dtype)


def short_conv_mixer(xbc, conv_fn, conv_b):
    xb, bg, cg = jnp.split(xbc, 3, axis=-1)
    return bg * conv_fn(cg * xb, conv_b)


def merge_branches(y_gdn, y_conv, gates, w_branch_a, w_branch_b, w_out):
    gate_a, gate_b = jnp.split(gates, 2, axis=-1)
    m = jax.nn.sigmoid(gate_a) * (y_gdn @ w_branch_a) + jax.nn.sigmoid(gate_b) * (y_conv @ w_branch_b)
    return m @ w_out


def hierarchical_moe(h, w_router_group, b_router_group, w_router_expert, b_router_expert,
                     w_exp_gate, w_exp_up, w_exp_down):
    n, d = h.shape
    hf = h.astype(jnp.float32)
    p_group = jax.nn.softmax(hf @ w_router_group.astype(jnp.float32) + b_router_group.astype(jnp.float32), axis=-1)
    group = jnp.argmax(p_group, axis=-1)
    gate_group = jnp.take_along_axis(p_group, group[:, None], axis=-1)
    logits_e = (hf @ w_router_expert.astype(jnp.float32) + b_router_expert.astype(jnp.float32)).reshape(n, N_GROUPS, EXPERTS_PER_GROUP)
    logits_e = jnp.take_along_axis(logits_e, group[:, None, None], axis=1)[:, 0]
    top_p, top_i = lax.top_k(jax.nn.softmax(logits_e, axis=-1), TOP_K_IN_GROUP)
    weights = (gate_group * top_p / jnp.sum(top_p, axis=-1, keepdims=True)).reshape(-1)
    expert_id = (group[:, None] * EXPERTS_PER_GROUP + top_i).reshape(-1)
    n_assign = n * TOP_K_IN_GROUP
    token_id = jnp.repeat(jnp.arange(n), TOP_K_IN_GROUP)
    order = jnp.argsort(expert_id)
    e_s, tok_s, w_s = expert_id[order], token_id[order], weights[order]
    counts = jnp.bincount(expert_id, length=N_EXPERTS)
    padded = (counts + ROUTE_BLOCK - 1) // ROUTE_BLOCK * ROUTE_BLOCK
    pad_end = jnp.cumsum(padded)
    pad_start = pad_end - padded
    start = jnp.cumsum(counts) - counts
    dest = pad_start[e_s] + jnp.arange(n_assign) - start[e_s]
    n_blocks = -(-n_assign // ROUTE_BLOCK) + N_EXPERTS
    buf = jnp.zeros((n_blocks * ROUTE_BLOCK, d), h.dtype).at[dest].set(h[tok_s])
    block_e = jnp.minimum(jnp.searchsorted(pad_end, jnp.arange(n_blocks) * ROUTE_BLOCK, side='right'), N_EXPERTS - 1)

    def expert_block(args):
        xb, e = args
        return (jax.nn.silu(xb @ w_exp_gate[e]) * (xb @ w_exp_up[e])) @ w_exp_down[e]

    y = lax.map(expert_block, (buf.reshape(n_blocks, ROUTE_BLOCK, d), block_e)).reshape(-1, d)
    y = y[dest] * w_s[:, None].astype(h.dtype)
    return jax.ops.segment_sum(y, tok_s, num_segments=n)


def trunk_layer(x, ctx, c, c_ctx, w_ada, b_ada, w_in, conv_qkv, a_log, dt_bias, gdn_norm_w, conv_b,
                w_branch_a, w_branch_b, w_out, ln1_g, ln1_b, w_router_group, b_router_group,
                w_router_expert, b_router_expert, w_exp_gate, w_exp_up, w_exp_down, ln2_g, ln2_b, need_ctx):
    b, n, d = x.shape
    mod_lat = jnp.split((jax.nn.silu(c) @ w_ada + b_ada)[:, None, :], N_MOD, axis=-1)
    mod_ctx = jnp.split(jax.nn.silu(c_ctx) @ w_ada + b_ada, N_MOD, axis=-1)

    h_lat = modulate(layer_norm(x), mod_lat[0], mod_lat[1])
    h_ctx = modulate(layer_norm(ctx), mod_ctx[0], mod_ctx[1])
    qkv_l, z_l, beta_l, a_l, xbc_l, gates_l = split_projection(h_lat @ w_in)
    qkv_c, z_c, beta_c, a_c, xbc_c, gates_c = split_projection(h_ctx @ w_in)
    q_c, k_c, v_c, g_c, bt_c = gdn_inputs(qkv_c, beta_c, a_c, dwconv_centred, conv_qkv, a_log, dt_bias)
    q_l, k_l, v_l, g_l, bt_l = gdn_inputs(qkv_l, beta_l, a_l, conv_latent, conv_qkv, a_log, dt_bias)
    s0 = jnp.zeros((2 * b, GDN_HEADS, GDN_HEAD_DIM, GDN_HEAD_DIM), jnp.float32)
    o_c, s_c = gdn_bidirectional(q_c, k_c, v_c, g_c, bt_c, s0)
    o_l, _ = gdn_bidirectional(q_l, k_l, v_l, g_l, bt_l, s_c)
    mix_l = merge_branches(gated_rms_norm(o_l, z_l, gdn_norm_w), short_conv_mixer(xbc_l, conv_latent, conv_b),
                           gates_l, w_branch_a, w_branch_b, w_out)
    x = layer_norm(DEEPNORM_ALPHA * x + mod_lat[2] * mix_l, ln1_g, ln1_b)
    if need_ctx:
        mix_c = merge_branches(gated_rms_norm(o_c, z_c, gdn_norm_w), short_conv_mixer(xbc_c, dwconv_centred, conv_b),
                               gates_c, w_branch_a, w_branch_b, w_out)
        ctx = layer_norm(DEEPNORM_ALPHA * ctx + mod_ctx[2] * mix_c, ln1_g, ln1_b)

    tokens = modulate(layer_norm(x), mod_lat[3], mod_lat[4]).reshape(-1, d)
    if need_ctx:
        tokens = jnp.concatenate([tokens, modulate(layer_norm(ctx), mod_ctx[3], mod_ctx[4]).reshape(-1, d)], axis=0)
    f = hierarchical_moe(tokens, w_router_group, b_router_group, w_router_expert, b_router_expert,
                         w_exp_gate, w_exp_up, w_exp_down)
    x = layer_norm(DEEPNORM_ALPHA * x + mod_lat[5] * f[:b * n].reshape(b, n, d), ln2_g, ln2_b)
    if need_ctx:
        ctx = layer_norm(DEEPNORM_ALPHA * ctx + mod_ctx[5] * f[b * n:].reshape(ctx.shape), ln2_g, ln2_b)
    return x, ctx


def setup_inputs(seed: int = 0) -> dict:
    key = jax.random.key(seed)
    ks = jax.random.split(key, 26)
    nrm = lambda k, shape, s: jax.random.normal(k, shape, jnp.float32) * s
    dt = jnp.exp(jax.random.uniform(ks[9], (DEPTH, 2, GDN_HEADS), jnp.float32, np.log(1e-3), np.log(1e-1)))
    return {
        "x": nrm(ks[0], (BATCH, SEQ, D_MODEL), 1.0),
        "c": nrm(ks[1], (BATCH, D_MODEL), 1.0),
        "ctx": nrm(ks[2], (BATCH, CTX_LEN, D_MODEL), 1.0),
        "c_ctx": nrm(ks[3], (D_MODEL,), 1.0),
        "w_ada": nrm(ks[4], (DEPTH, D_MODEL, N_MOD * D_MODEL), 0.5 * D_MODEL ** -0.5),
        "b_ada": nrm(ks[5], (DEPTH, N_MOD * D_MODEL), 0.02),
        "w_in": nrm(ks[6], (DEPTH, D_MODEL, IN_COLS), D_MODEL ** -0.5),
        "conv_qkv": nrm(ks[7], (DEPTH, QKV_CONV, 3 * GDN_WIDTH), QKV_CONV ** -0.5),
        "a_log": jnp.log(jax.random.uniform(ks[8], (DEPTH, 2, GDN_HEADS), jnp.float32, 1.0, 16.0)),
        "dt_bias": dt + jnp.log(-jnp.expm1(-dt)),
        "gdn_norm_w": 1.0 + nrm(ks[10], (DEPTH, GDN_HEAD_DIM), 0.02),
        "conv_b": nrm(ks[11], (DEPTH, SHORT_CONV, CONV_WIDTH), SHORT_CONV ** -0.5),
        "w_branch_a": nrm(ks[12], (DEPTH, GDN_WIDTH, D_MODEL), GDN_WIDTH ** -0.5),
        "w_branch_b": nrm(ks[13], (DEPTH, CONV_WIDTH, D_MODEL), CONV_WIDTH ** -0.5),
        "w_out": nrm(ks[14], (DEPTH, D_MODEL, D_MODEL), DEEPNORM_BETA * D_MODEL ** -0.5),
        "ln1_g": 1.0 + nrm(ks[15], (DEPTH, D_MODEL), 0.02),
        "ln1_b": nrm(ks[16], (DEPTH, D_MODEL), 0.02),
        "w_router_group": nrm(ks[17], (DEPTH, D_MODEL, N_GROUPS), D_MODEL ** -0.5),
        "b_router_group": nrm(ks[18], (DEPTH, N_GROUPS), 0.01),
        "w_router_expert": nrm(ks[19], (DEPTH, D_MODEL, N_EXPERTS), D_MODEL ** -0.5),
        "b_router_expert": nrm(ks[20], (DEPTH, N_EXPERTS), 0.01),
        "w_exp_gate": nrm(ks[21], (DEPTH, N_EXPERTS, D_MODEL, EXPERT_HIDDEN), D_MODEL ** -0.5),
        "w_exp_up": nrm(ks[22], (DEPTH, N_EXPERTS, D_MODEL, EXPERT_HIDDEN), D_MODEL ** -0.5),
        "w_exp_down": nrm(ks[23], (DEPTH, N_EXPERTS, EXPERT_HIDDEN, D_MODEL), DEEPNORM_BETA * EXPERT_HIDDEN ** -0.5),
        "ln2_g": 1.0 + nrm(ks[24], (DEPTH, D_MODEL), 0.02),
        "ln2_b": nrm(ks[25], (DEPTH, D_MODEL), 0.02),
    }


def reference(x, c, ctx, c_ctx, w_ada, b_ada, w_in, conv_qkv, a_log, dt_bias, gdn_norm_w, conv_b,
              w_branch_a, w_branch_b, w_out, ln1_g, ln1_b, w_router_group, b_router_group,
              w_router_expert, b_router_expert, w_exp_gate, w_exp_up, w_exp_down, ln2_g, ln2_b):
    for layer in range(DEPTH):
        x, ctx = trunk_layer(
            x, ctx, c, c_ctx, w_ada[layer], b_ada[layer], w_in[layer], conv_qkv[layer], a_log[layer],
            dt_bias[layer], gdn_norm_w[layer], conv_b[layer], w_branch_a[layer], w_branch_b[layer],
            w_out[layer], ln1_g[layer], ln1_b[layer], w_router_group[layer], b_router_group[layer],
            w_router_expert[layer], b_router_expert[layer], w_exp_gate[layer], w_exp_up[layer],
            w_exp_down[layer], ln2_g[layer], ln2_b[layer], need_ctx=layer < DEPTH - 1)
    return x
```

```python
import functools

import numpy as np
import jax
import jax.numpy as jnp
from jax import lax
from jax.experimental import pallas as pl
from jax.experimental.pallas import tpu as pltpu

F32 = jnp.float32
BF16 = jnp.bfloat16
HIGHEST = lax.Precision.HIGHEST

D_MODEL = 4096
GRID_W = 64
CHUNK = 64
HEADS = 16
HEAD_DIM = 128
GDN_WIDTH = HEADS * HEAD_DIM
CONV_WIDTH = D_MODEL // 2
N_GROUPS = 8
EXPERTS_PER_GROUP = 8
N_EXPERTS = 64
EXPERT_HIDDEN = 768
N_MOD = 6
DEEPNORM_ALPHA = 2.0 ** 0.25
LN_EPS = 1e-6
RMS_EPS = 1e-6

LANES = 128
SUBLANES = 8
UNITS = 128
VMEM_LIMIT = 52 * 1024 * 1024


def _cparams(sem, vmem=VMEM_LIMIT):
    return pltpu.CompilerParams(dimension_semantics=sem, vmem_limit_bytes=vmem)


def _silu(x):
    return x * jax.nn.sigmoid(x)


def _layer_norm(x):
    mu = jnp.mean(x, axis=-1, keepdims=True)
    xc = x - mu
    var = jnp.mean(xc * xc, axis=-1, keepdims=True)
    return xc * lax.rsqrt(var + LN_EPS)


def _mod_kernel(c_ref, w_ref, b_ref, o_ref):
    s = _silu(c_ref[...])
    o_ref[...] = jnp.dot(s, w_ref[...], preferred_element_type=F32, precision=HIGHEST) + b_ref[...]


def _mod_call(cs, w_ada, b_ada):
    n = w_ada.shape[1]
    tn = 512
    return pl.pallas_call(
        _mod_kernel,
        out_shape=jax.ShapeDtypeStruct((SUBLANES, n), F32),
        grid=(n // tn,),
        in_specs=[pl.BlockSpec((SUBLANES, D_MODEL), lambda j: (0, 0)),
                  pl.BlockSpec((D_MODEL, tn), lambda j: (0, j)),
                  pl.BlockSpec((1, tn), lambda j: (0, j))],
        out_specs=pl.BlockSpec((SUBLANES, tn), lambda j: (0, j)),
        compiler_params=_cparams(("arbitrary",)),
        name="mod",
    )(cs, w_ada, b_ada)


def _ln_mod_kernel(x_ref, shift_ref, scale_ref, o_ref):
    y = _layer_norm(x_ref[...])
    o_ref[...] = (y * (1.0 + scale_ref[...]) + shift_ref[...]).astype(o_ref.dtype)


def _ln_mod_call(x, shift, scale):
    t = x.shape[0]
    tr = min(256, t)
    return pl.pallas_call(
        _ln_mod_kernel,
        out_shape=jax.ShapeDtypeStruct((t, D_MODEL), BF16),
        grid=(t // tr,),
        in_specs=[pl.BlockSpec((tr, D_MODEL), lambda i: (i, 0)),
                  pl.BlockSpec((1, D_MODEL), lambda i: (0, 0)),
                  pl.BlockSpec((1, D_MODEL), lambda i: (0, 0))],
        out_specs=pl.BlockSpec((tr, D_MODEL), lambda i: (i, 0)),
        compiler_params=_cparams(("arbitrary",)),
        name="ln_mod",
    )(x, shift, scale)


COL_QKV = 0
COL_Z = 3 * GDN_WIDTH
COL_XB = COL_Z + GDN_WIDTH
COL_BG = COL_XB + CONV_WIDTH
COL_CG = COL_BG + CONV_WIDTH
COL_GATES = COL_CG + CONV_WIDTH
COL_BA = COL_GATES + 2 * D_MODEL
W_CAT_COLS = COL_BA + LANES


def _conv3_rows(x, taps_ref, group):
    rows = x.shape[0]
    pos = lax.broadcasted_iota(jnp.int32, x.shape, 0) % group
    prev = jnp.where(pos == 0, 0.0, pltpu.roll(x, 1, 0))
    nxt = jnp.where(pos == group - 1, 0.0, pltpu.roll(x, rows - 1, 0))
    return taps_ref[0:1, :] * prev + taps_ref[1:2, :] * x + taps_ref[2:3, :] * nxt


def _ip_qkv_kernel(group, x_ref, w_ref, taps_ref, o_ref):
    acc = jnp.dot(x_ref[...], w_ref[...], preferred_element_type=F32)
    y = _silu(_conv3_rows(acc, taps_ref, group))
    j = pl.program_id(0)
    tiles_per_part = GDN_WIDTH // acc.shape[1]
    for hh in range(acc.shape[1] // HEAD_DIM):
        ys = y[:, hh * HEAD_DIM:(hh + 1) * HEAD_DIM]
        r = lax.rsqrt(jnp.sum(ys * ys, axis=-1, keepdims=True) + 1e-6)
        scale = jnp.where(j < tiles_per_part, r * HEAD_DIM ** -0.5,
                          jnp.where(j < 2 * tiles_per_part, r, 1.0))
        o_ref[hh] = (ys * scale).astype(o_ref.dtype)


def _ip_z_kernel(x_ref, w_ref, nw_ref, o_ref):
    acc = jnp.dot(x_ref[...], w_ref[...], preferred_element_type=F32)
    for hh in range(acc.shape[1] // HEAD_DIM):
        zs = acc[:, hh * HEAD_DIM:(hh + 1) * HEAD_DIM]
        o_ref[hh] = _silu(zs) * nw_ref[...]


def _ip_xbc_kernel(group, x_ref, wxb_ref, wbg_ref, wcg_ref, taps_ref, o_ref):
    x = x_ref[...]
    xb = jnp.dot(x, wxb_ref[...], preferred_element_type=F32)
    bg = jnp.dot(x, wbg_ref[...], preferred_element_type=F32)
    cg = jnp.dot(x, wcg_ref[...], preferred_element_type=F32)
    o_ref[...] = (bg * _conv3_rows(cg * xb, taps_ref, group)).astype(o_ref.dtype)


def _ip_gates_kernel(x_ref, w_ref, o_ref):
    acc = jnp.dot(x_ref[...], w_ref[...], preferred_element_type=F32)
    o_ref[...] = jax.nn.sigmoid(acc).astype(o_ref.dtype)


def _ip_ba_kernel(x_ref, w_ref, alog_ref, dtb_ref, o_ref):
    acc = jnp.dot(x_ref[...], w_ref[...], preferred_element_type=F32)
    lane = lax.broadcasted_iota(jnp.int32, acc.shape, 1)
    a = acc + dtb_ref[...]
    softplus = jnp.maximum(a, 0.0) + jnp.log(1.0 + jnp.exp(-jnp.abs(a)))
    o_ref[...] = jnp.where(lane < 2 * HEADS, jax.nn.sigmoid(acc), -jnp.exp(alog_ref[...]) * softplus)


def _inproj(kernel, h, w_cat, col_starts, tn, n_tiles, extra, extra_specs, out_shape, out_spec, tm):
    t = h.shape[0]
    w_specs = [pl.BlockSpec((D_MODEL, tn), functools.partial(lambda j, i, o: (0, o + j), o=c // tn))
               for c in col_starts]
    return pl.pallas_call(
        kernel,
        out_shape=out_shape,
        grid=(n_tiles, t // tm),
        in_specs=[pl.BlockSpec((tm, D_MODEL), lambda j, i: (i, 0))] + w_specs + extra_specs,
        out_specs=out_spec,
        compiler_params=_cparams(("arbitrary", "arbitrary")),
        name="inproj",
    )(h, *([w_cat] * len(col_starts)), *extra)


def _inproj_qkv(h, w_cat, taps, group, tm):
    t = h.shape[0]
    tn = 1024
    hpt = tn // HEAD_DIM
    return _inproj(
        functools.partial(_ip_qkv_kernel, group), h, w_cat, [COL_QKV], tn, 3 * GDN_WIDTH // tn,
        [taps], [pl.BlockSpec((3, tn), lambda j, i: (0, j))],
        jax.ShapeDtypeStruct((3 * HEADS, t, HEAD_DIM), BF16),
        pl.BlockSpec((hpt, tm, HEAD_DIM), lambda j, i: (j, i, 0)), tm)


def _inproj_z(h, w_cat, norm_w, tm):
    t = h.shape[0]
    tn = 1024
    hpt = tn // HEAD_DIM
    return _inproj(
        _ip_z_kernel, h, w_cat, [COL_Z], tn, GDN_WIDTH // tn,
        [norm_w], [pl.BlockSpec((1, HEAD_DIM), lambda j, i: (0, 0))],
        jax.ShapeDtypeStruct((HEADS, t, HEAD_DIM), F32),
        pl.BlockSpec((hpt, tm, HEAD_DIM), lambda j, i: (j, i, 0)), tm)


def _inproj_xbc(h, w_cat, taps, group, tm):
    t = h.shape[0]
    tn = 512
    return _inproj(
        functools.partial(_ip_xbc_kernel, group), h, w_cat, [COL_XB, COL_BG, COL_CG], tn,
        CONV_WIDTH // tn, [taps], [pl.BlockSpec((3, tn), lambda j, i: (0, j))],
        jax.ShapeDtypeStruct((t, CONV_WIDTH), BF16),
        pl.BlockSpec((tm, tn), lambda j, i: (i, j)), tm)


def _inproj_gates(h, w_cat, tm):
    t = h.shape[0]
    tn = 1024
    return _inproj(
        _ip_gates_kernel, h, w_cat, [COL_GATES], tn, 2 * D_MODEL // tn, [], [],
        jax.ShapeDtypeStruct((t, 2 * D_MODEL), BF16),
        pl.BlockSpec((tm, tn), lambda j, i: (i, j)), tm)


def _inproj_ba(h, w_cat, alog_row, dtb_row, tm):
    t = h.shape[0]
    return _inproj(
        _ip_ba_kernel, h, w_cat, [COL_BA], LANES, 1, [alog_row, dtb_row],
        [pl.BlockSpec((1, LANES), lambda j, i: (0, 0))] * 2,
        jax.ShapeDtypeStruct((t, LANES), F32),
        pl.BlockSpec((tm, LANES), lambda j, i: (i, 0)), tm)


def _gdn_scal_kernel(g_ref, gc_ref, ep_ref, egl_ref):
    d = pl.program_id(0)
    r = lax.broadcasted_iota(jnp.int32, (CHUNK, CHUNK), 0)
    c = lax.broadcasted_iota(jnp.int32, (CHUNK, CHUNK), 1)
    tri = jnp.where(d == 0, (c <= r).astype(F32), (c >= r).astype(F32))
    gc = jnp.dot(tri, g_ref[0], preferred_element_type=F32, precision=HIGHEST)
    gl = jnp.where(d == 0, gc[CHUNK - 1:CHUNK, :], gc[0:1, :])
    gc_ref[0] = gc
    ep_ref[0] = jnp.exp(gl - gc)
    egl_ref[0] = jnp.exp(gl)


def _gdn_scal_call(g_s):
    u = g_s.shape[2]
    tu = 512 if u % 512 == 0 else UNITS
    spec = pl.BlockSpec((1, CHUNK, tu), lambda d, i: (d, 0, i))
    return pl.pallas_call(
        _gdn_scal_kernel,
        out_shape=(jax.ShapeDtypeStruct(g_s.shape, F32), jax.ShapeDtypeStruct(g_s.shape, F32),
                   jax.ShapeDtypeStruct((2, 1, u), F32)),
        grid=(2, u // tu),
        in_specs=[spec],
        out_specs=(spec, spec, pl.BlockSpec((1, 1, tu), lambda d, i: (d, 0, i))),
        compiler_params=_cparams(("arbitrary", "arbitrary")),
        name="gdn_scal",
    )(g_s)


def _nt_dot(a, b):
    return lax.dot_general(a, b, (((1,), (1,)), ((), ())), preferred_element_type=F32)


def _gdn_gram_kernel(cb, q_ref, k_ref, g_ref, qk_ref, kt_ref):
    r = lax.broadcasted_iota(jnp.int32, (HEAD_DIM, HEAD_DIM), 0)
    c = lax.broadcasted_iota(jnp.int32, (HEAD_DIM, HEAD_DIM), 1)
    eye = (r == c).astype(BF16)
    for ci in range(cb):
        k = k_ref[0, ci * CHUNK:(ci + 1) * CHUNK, :]
        q = q_ref[0, ci * CHUNK:(ci + 1) * CHUNK, :]
        g_ref[ci] = _nt_dot(k, k)
        qk_ref[ci] = _nt_dot(q, k)
        kt_ref[ci] = _nt_dot(eye, k).astype(BF16)


def _gdn_gram_call(qkv, nc):
    cb = min(8, nc)
    ncb = nc // cb
    u = HEADS * nc
    return pl.pallas_call(
        functools.partial(_gdn_gram_kernel, cb),
        out_shape=(jax.ShapeDtypeStruct((u, CHUNK, CHUNK), F32),
                   jax.ShapeDtypeStruct((u, CHUNK, CHUNK), F32),
                   jax.ShapeDtypeStruct((u, HEAD_DIM, CHUNK), BF16)),
        grid=(HEADS, ncb),
        in_specs=[pl.BlockSpec((1, cb * CHUNK, HEAD_DIM), lambda h, c: (h, c, 0)),
                  pl.BlockSpec((1, cb * CHUNK, HEAD_DIM), lambda h, c: (HEADS + h, c, 0))],
        out_specs=(pl.BlockSpec((cb, CHUNK, CHUNK), lambda h, c: (h * ncb + c, 0, 0)),
                   pl.BlockSpec((cb, CHUNK, CHUNK), lambda h, c: (h * ncb + c, 0, 0)),
                   pl.BlockSpec((cb, HEAD_DIM, CHUNK), lambda h, c: (h * ncb + c, 0, 0))),
        compiler_params=_cparams(("arbitrary", "arbitrary")),
        name="gdn_gram",
    )(qkv, qkv)


NB = CHUNK // SUBLANES


def _row_bcast(ref, row):
    return jnp.broadcast_to(ref[pl.ds(row, 1), :], (SUBLANES, UNITS))


def _gdn_inv_kernel(bwd, g_ref, qk_ref, beta_ref, gc_ref, twtu_ref, atde_ref,
                    g_s, qk_s, l_s, t_s, tw_s, tu_s, at_s, de_s, e_s):
    pos = (lambda a: CHUNK - 1 - a) if bwd else (lambda a: a)
    blk = (lambda b: NB - 1 - b) if bwd else (lambda b: b)

    for pp in range(CHUNK // 2):
        sl = slice(pp * LANES, (pp + 1) * LANES)
        g_s[sl, :] = g_ref[:, sl].T
        qk_s[sl, :] = qk_ref[:, sl].T
    e_s[...] = jnp.exp(gc_ref[0])

    @pl.when(pl.program_id(0) == 0)
    def _():
        for ref in (tw_s, tu_s, at_s, de_s):
            ref[...] = jnp.zeros(ref.shape, F32)

    sub = lax.broadcasted_iota(jnp.int32, (SUBLANES, UNITS), 0)
    zero = jnp.zeros((SUBLANES, UNITS), F32)
    rs = range(SUBLANES)
    own = [(SUBLANES - 1 - r) if bwd else r for r in rs]
    earlier = [(sub > own[r]) if bwd else (sub < own[r]) for r in rs]

    def tile_ds(p, b):
        return pl.ds(pl.multiple_of(p * CHUNK + b * SUBLANES, SUBLANES), SUBLANES)

    def cols_ds(b):
        return pl.ds(pl.multiple_of(b * SUBLANES, SUBLANES), SUBLANES)

    def row_block(ib, carry):
        b_own = blk(ib)
        ps = [pos(ib * SUBLANES + r) for r in rs]
        gc_p = [_row_bcast(gc_ref.at[0], p) for p in ps]
        beta_p = [_row_bcast(beta_ref.at[0], p) for p in ps]

        def weights_offdiag(bc, c):
            b = blk(bc)
            gc_c = gc_ref[0, cols_ds(b), :]
            for r in rs:
                dec = jnp.exp(gc_p[r] - gc_c)
                l_s[tile_ds(ps[r], b), :] = beta_p[r] * g_s[tile_ds(ps[r], b), :] * dec
                at_s[tile_ds(ps[r], b), :] = qk_s[tile_ds(ps[r], b), :] * dec
            return c

        lax.fori_loop(0, ib, weights_offdiag, 0)
        gc_c = gc_ref[0, cols_ds(b_own), :]
        for r in rs:
            t = tile_ds(ps[r], b_own)
            dec = jnp.exp(gc_p[r] - gc_c)
            l_s[t, :] = jnp.where(earlier[r], beta_p[r] * g_s[t, :] * dec, 0.0)
            at_s[t, :] = jnp.where(sub == own[r], qk_s[t, :], jnp.where(earlier[r], qk_s[t, :] * dec, 0.0))
            de_s[t, :] = jnp.where(sub == own[r], _row_bcast(e_s, ps[r]), 0.0)

        def finish(b, acc):
            done = []
            for r in rs:
                a_r = acc[r]
                for kk in range(r):
                    a_r = a_r - _row_bcast(l_s, ps[r] * CHUNK + ps[kk]) * done[kk]
                done.append(a_r)
            beta_c = beta_ref[0, cols_ds(b), :]
            be_c = beta_c * e_s[cols_ds(b), :]
            for r in rs:
                t_s[tile_ds(ps[r], b), :] = done[r]
                tu_s[tile_ds(ps[r], b), :] = done[r] * beta_c
                tw_s[tile_ds(ps[r], b), :] = done[r] * be_c

        def subst_offdiag(bc, c):
            b = blk(bc)

            def k_block(kb, acc):
                acc = list(acc)
                for kk in rs:
                    pk = pos(kb * SUBLANES + kk)
                    t_k = t_s[tile_ds(pk, b), :]
                    for r in rs:
                        acc[r] = acc[r] - _row_bcast(l_s, ps[r] * CHUNK + pk) * t_k
                return tuple(acc)

            finish(b, lax.fori_loop(bc, ib, k_block, (zero,) * SUBLANES))
            return c

        lax.fori_loop(0, ib, subst_offdiag, 0)
        finish(b_own, [jnp.where(sub == own[r], 1.0, 0.0) for r in rs])
        return carry

    lax.fori_loop(0, NB, row_block, 0)

    for p in range(CHUNK):
        rows = slice(p * CHUNK, (p + 1) * CHUNK)
        sl = slice(p * LANES, (p + 1) * LANES)
        twtu_ref[:, sl] = jnp.concatenate([tw_s[rows, :], tu_s[rows, :]], axis=0).T.astype(BF16)
        atde_ref[:, sl] = jnp.concatenate([at_s[rows, :], de_s[rows, :]], axis=0).T.astype(BF16)


def _gdn_inv_call(bwd, g_flat, qk_flat, beta_s, gc_s):
    u = g_flat.shape[0]
    d = 1 if bwd else 0
    mat = pl.BlockSpec((UNITS, CHUNK * CHUNK), lambda i: (i, 0))
    sc = pl.BlockSpec((1, CHUNK, UNITS), lambda i: (d, 0, i))
    out = pl.BlockSpec((UNITS, CHUNK * LANES), lambda i: (i, 0))
    soa = pltpu.VMEM((CHUNK * CHUNK, UNITS), F32)
    return pl.pallas_call(
        functools.partial(_gdn_inv_kernel, bwd),
        out_shape=(jax.ShapeDtypeStruct((u, CHUNK * LANES), BF16),
                   jax.ShapeDtypeStruct((u, CHUNK * LANES), BF16)),
        grid=(u // UNITS,),
        in_specs=[mat, mat, sc, sc],
        out_specs=(out, out),
        scratch_shapes=[soa] * 8 + [pltpu.VMEM((CHUNK, UNITS), F32)],
        compiler_params=_cparams(("arbitrary",)),
        name="gdn_inv_bwd" if bwd else "gdn_inv_fwd",
    )(g_flat, qk_flat, beta_s, gc_s)


def _gdn_scan_kernel(hb, cb, egl_ref, q_ref, k_ref, v_ref, kt_ref, twtu_ref, atde_ref, ep_ref, s0_ref,
                     o_ref, sfin_ref, s_scr):
    d = pl.program_id(0)
    hg = pl.program_id(1)
    ci = pl.program_id(2)
    ncb = pl.num_programs(2)

    @pl.when(ci == 0)
    def _():
        s_scr[...] = s0_ref[0]

    cblk = ci + d * (ncb - 1 - 2 * ci)
    zeros = jnp.zeros((CHUNK, HEAD_DIM), BF16)

    def chunk_step(cc, carry):
        c = cc + d * (cb - 1 - 2 * cc)
        r0 = pl.multiple_of(c * CHUNK, CHUNK)
        for hh in range(hb):
            k = k_ref[hh, pl.ds(r0, CHUNK), :]
            q = q_ref[hh, pl.ds(r0, CHUNK), :]
            v = v_ref[hh, pl.ds(r0, CHUNK), :]
            rhs = jnp.concatenate([jnp.concatenate([k, zeros], axis=1),
                                   jnp.concatenate([zeros, v], axis=1)], axis=0)
            wu = jnp.dot(twtu_ref[0, hh, c], rhs, preferred_element_type=F32)
            w = wu[:, :HEAD_DIM]
            u = wu[:, HEAD_DIM:]
            s = s_scr[hh]
            x = jnp.dot(jnp.concatenate([w.astype(BF16), q], axis=0), s.astype(BF16),
                        preferred_element_type=F32)
            v_new = u - x[:CHUNK]
            o = jnp.dot(atde_ref[0, hh, c], jnp.concatenate([v_new, x[CHUNK:]], axis=0).astype(BF16),
                        preferred_element_type=F32)
            o_ref[0, hh, pl.ds(r0, CHUNK), :] = o
            kdt = (kt_ref[hh, c].astype(F32) * ep_ref[0, hh, pl.ds(c, 1), :]).astype(BF16)
            egl = egl_ref[d, hg * hb + hh, cblk * cb + c]
            s_scr[hh] = egl * s + jnp.dot(kdt, v_new.astype(BF16), preferred_element_type=F32)
        return carry

    lax.fori_loop(0, cb, chunk_step, 0)

    @pl.when(ci == ncb - 1)
    def _():
        sfin_ref[0] = s_scr[...]


def _gdn_scan_call(qkv, kt, twtu, atde, ep, egl, s0, nc):
    t = qkv.shape[1]
    hb = 4
    cb = min(8, nc)
    ncb = nc // cb
    hgs = HEADS // hb
    nat = lambda d, c: c + d * (ncb - 1 - 2 * c)
    tok = lambda part: pl.BlockSpec((hb, cb * CHUNK, HEAD_DIM),
                                    lambda d, hg, c, egl, part=part: (part * hgs + hg, nat(d, c), 0))
    per_dir = lambda last: pl.BlockSpec((1, hb, cb, CHUNK, last),
                                        lambda d, hg, c, egl: (d, hg, nat(d, c), 0, 0))
    grid_spec = pltpu.PrefetchScalarGridSpec(
        num_scalar_prefetch=1,
        grid=(2, hgs, ncb),
        in_specs=[tok(0), tok(1), tok(2),
                  pl.BlockSpec((hb, cb, HEAD_DIM, CHUNK), lambda d, hg, c, egl: (hg, nat(d, c), 0, 0)),
                  per_dir(LANES), per_dir(LANES),
                  pl.BlockSpec((1, hb, cb, CHUNK), lambda d, hg, c, egl: (d, hg, nat(d, c), 0)),
                  pl.BlockSpec((1, hb, HEAD_DIM, HEAD_DIM), lambda d, hg, c, egl: (d, hg, 0, 0))],
        out_specs=(pl.BlockSpec((1, hb, cb * CHUNK, HEAD_DIM), lambda d, hg, c, egl: (d, hg, nat(d, c), 0)),
                   pl.BlockSpec((1, hb, HEAD_DIM, HEAD_DIM), lambda d, hg, c, egl: (d, hg, 0, 0))),
        scratch_shapes=[pltpu.VMEM((hb, HEAD_DIM, HEAD_DIM), F32)],
    )
    return pl.pallas_call(
        functools.partial(_gdn_scan_kernel, hb, cb),
        out_shape=(jax.ShapeDtypeStruct((2, HEADS, t, HEAD_DIM), F32),
                   jax.ShapeDtypeStruct((2, HEADS, HEAD_DIM, HEAD_DIM), F32)),
        grid_spec=grid_spec,
        compiler_params=_cparams(("arbitrary", "arbitrary", "arbitrary")),
        name="gdn_scan",
    )(egl, qkv, qkv, qkv, kt.reshape(HEADS, nc, HEAD_DIM, CHUNK),
      twtu.reshape(2, HEADS, nc, CHUNK, LANES), atde.reshape(2, HEADS, nc, CHUNK, LANES), ep, s0)


def _gdn(qkv, bg, s0):
    t = qkv.shape[1]
    nc = t // CHUNK
    u = HEADS * nc
    up = -(-u // UNITS) * UNITS
    to_soa = lambda a: jnp.pad(jnp.transpose(a.reshape(nc, CHUNK, 2, HEADS), (2, 1, 3, 0)).reshape(2, CHUNK, u),
                               ((0, 0), (0, 0), (0, up - u)))
    beta_s = to_soa(bg[:, :2 * HEADS])
    g_s = to_soa(bg[:, 2 * HEADS:4 * HEADS])
    gc_s, ep_s, egl_s = _gdn_scal_call(g_s)
    gram, qk, kt = _gdn_gram_call(qkv, nc)
    pad_u = lambda a: jnp.pad(a.reshape(u, -1), ((0, up - u), (0, 0)))
    g_flat, qk_flat = pad_u(gram), pad_u(qk)
    outs = [_gdn_inv_call(bwd, g_flat, qk_flat, beta_s, gc_s) for bwd in (False, True)]
    twtu = jnp.stack([o[0][:u] for o in outs])
    atde = jnp.stack([o[1][:u] for o in outs])
    ep = jnp.transpose(ep_s[:, :, :u].reshape(2, CHUNK, HEADS, nc), (0, 2, 3, 1))
    egl = egl_s[:, 0, :u].reshape(2, HEADS, nc)
    return _gdn_scan_call(qkv, kt, twtu, atde, ep, egl, s0, nc)


def _merge_kernel(o_ref, zs_ref, yc_ref, ga_ref, gb_ref, wa_ref, wb_ref, m_ref, yg_s):
    @pl.when(pl.program_id(1) == 0)
    def _():
        for hh in range(HEADS):
            o = o_ref[0, hh] + o_ref[1, hh]
            y = o * lax.rsqrt(jnp.mean(o * o, axis=-1, keepdims=True) + RMS_EPS) * zs_ref[hh]
            yg_s[:, hh * HEAD_DIM:(hh + 1) * HEAD_DIM] = y.astype(BF16)

    pa = jnp.dot(yg_s[...], wa_ref[...], preferred_element_type=F32)
    pb = jnp.dot(yc_ref[...], wb_ref[...], preferred_element_type=F32)
    m_ref[...] = (ga_ref[...].astype(F32) * pa + gb_ref[...].astype(F32) * pb).astype(m_ref.dtype)


def _merge_call(o, zs, yconv, gates, wa, wb):
    t = yconv.shape[0]
    tm, tn = 512, 512
    nj = D_MODEL // tn
    return pl.pallas_call(
        _merge_kernel,
        out_shape=jax.ShapeDtypeStruct((t, D_MODEL), BF16),
        grid=(t // tm, nj),
        in_specs=[pl.BlockSpec((2, HEADS, tm, HEAD_DIM), lambda i, j: (0, 0, i, 0)),
                  pl.BlockSpec((HEADS, tm, HEAD_DIM), lambda i, j: (0, i, 0)),
                  pl.BlockSpec((tm, CONV_WIDTH), lambda i, j: (i, 0)),
                  pl.BlockSpec((tm, tn), lambda i, j: (i, j)),
                  pl.BlockSpec((tm, tn), lambda i, j: (i, nj + j)),
                  pl.BlockSpec((GDN_WIDTH, tn), lambda i, j: (0, j)),
                  pl.BlockSpec((CONV_WIDTH, tn), lambda i, j: (0, j))],
        out_specs=pl.BlockSpec((tm, tn), lambda i, j: (i, j)),
        scratch_shapes=[pltpu.VMEM((tm, GDN_WIDTH), BF16)],
        compiler_params=_cparams(("arbitrary", "arbitrary")),
        name="merge",
    )(o, zs, yconv, gates, gates, wa, wb)


def _out_kernel(m_ref, w_ref, x_ref, mod_ref, ln_ref, wr_ref, x1_ref, tok_ref, logit_ref, acc_s):
    kk = pl.program_id(1)

    @pl.when(kk == 0)
    def _():
        acc_s[...] = jnp.zeros(acc_s.shape, F32)

    acc_s[...] += jnp.dot(m_ref[...], w_ref[...], preferred_element_type=F32)

    @pl.when(kk == pl.num_programs(1) - 1)
    def _():
        x1 = _layer_norm(DEEPNORM_ALPHA * x_ref[...] + mod_ref[0:1, :] * acc_s[...])
        x1 = x1 * ln_ref[0:1, :] + ln_ref[1:2, :]
        x1_ref[...] = x1
        tok = _layer_norm(x1) * (1.0 + mod_ref[2:3, :]) + mod_ref[1:2, :]
        tok_ref[...] = tok
        logit_ref[...] = jnp.dot(tok, wr_ref[...], preferred_element_type=F32, precision=HIGHEST)


def _out_call(m, w_out, x, mod3, ln1, w_router):
    t = m.shape[0]
    tm, tk = 256, 512
    return pl.pallas_call(
        _out_kernel,
        out_shape=(jax.ShapeDtypeStruct((t, D_MODEL), F32),
                   jax.ShapeDtypeStruct((t, D_MODEL), F32),
                   jax.ShapeDtypeStruct((t, LANES), F32)),
        grid=(t // tm, D_MODEL // tk),
        in_specs=[pl.BlockSpec((tm, tk), lambda i, k: (i, k)),
                  pl.BlockSpec((tk, D_MODEL), lambda i, k: (k, 0)),
                  pl.BlockSpec((tm, D_MODEL), lambda i, k: (i, 0)),
                  pl.BlockSpec((SUBLANES, D_MODEL), lambda i, k: (0, 0)),
                  pl.BlockSpec((SUBLANES, D_MODEL), lambda i, k: (0, 0)),
                  pl.BlockSpec((D_MODEL, LANES), lambda i, k: (0, 0))],
        out_specs=(pl.BlockSpec((tm, D_MODEL), lambda i, k: (i, 0)),
                   pl.BlockSpec((tm, D_MODEL), lambda i, k: (i, 0)),
                   pl.BlockSpec((tm, LANES), lambda i, k: (i, 0))),
        scratch_shapes=[pltpu.VMEM((tm, D_MODEL), F32)],
        compiler_params=_cparams(("arbitrary", "arbitrary")),
        name="out_proj",
    )(m, w_out, x, mod3, ln1, w_router)


MOE_BM = 256
MOE_HC = 256


def _gather_kernel(rows, idx_ref, src_ref, dst_ref, sem):
    base = pl.program_id(0) * rows

    def copy(r):
        return pltpu.make_async_copy(src_ref.at[pl.ds(idx_ref[0, 0, r], 1), :],
                                     dst_ref.at[pl.ds(base + r, 1), :], sem)

    def start(r, c):
        copy(r).start()
        return c

    def wait(r, c):
        copy(r).wait()
        return c

    lax.fori_loop(0, rows, start, 0)
    lax.fori_loop(0, rows, wait, 0)


def _gather_call(src, idx):
    n = idx.shape[0]
    rows = 256
    return pl.pallas_call(
        functools.partial(_gather_kernel, rows),
        out_shape=jax.ShapeDtypeStruct((n, src.shape[1]), src.dtype),
        grid=(n // rows,),
        in_specs=[pl.BlockSpec((1, 1, rows), lambda i: (i, 0, 0), memory_space=pltpu.SMEM),
                  pl.BlockSpec(memory_space=pl.ANY)],
        out_specs=pl.BlockSpec(memory_space=pl.ANY),
        scratch_shapes=[pltpu.SemaphoreType.DMA(())],
        compiler_params=_cparams(("arbitrary",)),
        name="moe_gather",
    )(idx.reshape(n // rows, 1, rows), src)


def _expert_kernel(be_ref, nu_ref, x_ref, wg_ref, wu_ref, wd_ref, rw_ref, y_ref):
    b = pl.program_id(0)
    hc = pl.program_id(1)

    @pl.when((b >= nu_ref[0]) & (hc == 0))
    def _():
        y_ref[...] = jnp.zeros(y_ref.shape, F32)

    @pl.when(b < nu_ref[0])
    def _():
        x = x_ref[...].astype(BF16)
        hg = jnp.dot(x, wg_ref[0].astype(BF16), preferred_element_type=F32)
        hu = jnp.dot(x, wu_ref[0].astype(BF16), preferred_element_type=F32)
        hid = (_silu(hg) * hu).astype(BF16)
        part = jnp.dot(hid, wd_ref[0].astype(BF16), preferred_element_type=F32)

        @pl.when(hc == 0)
        def _():
            y_ref[...] = part

        @pl.when(hc > 0)
        def _():
            y_ref[...] += part

        @pl.when(hc == pl.num_programs(1) - 1)
        def _():
            y_ref[...] = y_ref[...] * rw_ref[...]


def _expert_call(xs, wg, wu, wd, row_w, block_e, n_used):
    n = xs.shape[0]
    nb = n // MOE_BM
    nh = EXPERT_HIDDEN // MOE_HC
    blk = lambda b, nu: jnp.minimum(b, nu[0] - 1)
    grid_spec = pltpu.PrefetchScalarGridSpec(
        num_scalar_prefetch=2,
        grid=(nb, nh),
        in_specs=[pl.BlockSpec((MOE_BM, D_MODEL), lambda b, h, be, nu: (blk(b, nu), 0)),
                  pl.BlockSpec((1, D_MODEL, MOE_HC), lambda b, h, be, nu: (be[blk(b, nu)], 0, h)),
                  pl.BlockSpec((1, D_MODEL, MOE_HC), lambda b, h, be, nu: (be[blk(b, nu)], 0, h)),
                  pl.BlockSpec((1, MOE_HC, D_MODEL), lambda b, h, be, nu: (be[blk(b, nu)], h, 0)),
                  pl.BlockSpec((MOE_BM, 1), lambda b, h, be, nu: (blk(b, nu), 0))],
        out_specs=pl.BlockSpec((MOE_BM, D_MODEL), lambda b, h, be, nu: (b, 0)),
    )
    return pl.pallas_call(
        _expert_kernel,
        out_shape=jax.ShapeDtypeStruct((n, D_MODEL), F32),
        grid_spec=grid_spec,
        compiler_params=_cparams(("arbitrary", "arbitrary")),
        name="moe_experts",
    )(block_e, n_used, xs, wg, wu, wd, row_w)


def _combine_kernel(rows, d_ref, y_ref, x1_ref, mod_ref, ln_ref, o_ref, buf, sem):
    def copy(slot, r):
        return pltpu.make_async_copy(y_ref.at[pl.ds(d_ref[0, slot, r], 1), :],
                                     buf.at[slot, pl.ds(r, 1), :], sem)

    def start(r, c):
        copy(0, r).start()
        copy(1, r).start()
        return c

    def wait(r, c):
        copy(0, r).wait()
        copy(1, r).wait()
        return c

    lax.fori_loop(0, rows, start, 0)
    lax.fori_loop(0, rows, wait, 0)
    f = buf[0] + buf[1]
    y = _layer_norm(DEEPNORM_ALPHA * x1_ref[...] + mod_ref[0:1, :] * f)
    o_ref[...] = y * ln_ref[0:1, :] + ln_ref[1:2, :]


def _combine_call(ys, dest, x1, mod_row, ln2):
    t = x1.shape[0]
    rows = 128
    dest3 = jnp.transpose(dest.reshape(t // rows, rows, 2), (0, 2, 1))
    return pl.pallas_call(
        functools.partial(_combine_kernel, rows),
        out_shape=jax.ShapeDtypeStruct((t, D_MODEL), F32),
        grid=(t // rows,),
        in_specs=[pl.BlockSpec((1, 2, rows), lambda i: (i, 0, 0), memory_space=pltpu.SMEM),
                  pl.BlockSpec(memory_space=pl.ANY),
                  pl.BlockSpec((rows, D_MODEL), lambda i: (i, 0)),
                  pl.BlockSpec((SUBLANES, D_MODEL), lambda i: (0, 0)),
                  pl.BlockSpec((SUBLANES, D_MODEL), lambda i: (0, 0))],
        out_specs=pl.BlockSpec((rows, D_MODEL), lambda i: (i, 0)),
        scratch_shapes=[pltpu.VMEM((2, rows, D_MODEL), F32), pltpu.SemaphoreType.DMA(())],
        compiler_params=_cparams(("arbitrary",)),
        name="moe_combine",
    )(dest3, ys, x1, mod_row, ln2)


def _route(logits, b_group, b_expert):
    t = logits.shape[0]
    p_group = jax.nn.softmax(logits[:, :N_GROUPS] + b_group, axis=-1)
    group = jnp.argmax(p_group, axis=-1)
    gate_group = jnp.take_along_axis(p_group, group[:, None], axis=-1)
    le = (logits[:, N_GROUPS:N_GROUPS + N_EXPERTS] + b_expert).reshape(t, N_GROUPS, EXPERTS_PER_GROUP)
    le = jnp.take_along_axis(le, group[:, None, None], axis=1)[:, 0]
    top_p, top_i = lax.top_k(jax.nn.softmax(le, axis=-1), 2)
    weights = gate_group * top_p / jnp.sum(top_p, axis=-1, keepdims=True)
    expert_id = group[:, None] * EXPERTS_PER_GROUP + top_i
    return expert_id.astype(jnp.int32), weights


def _moe(tok, logits, b_group, b_expert, wg, wu, wd, x1, mod_row, ln2):
    t = tok.shape[0]
    expert_id, weights = _route(logits, b_group, b_expert)
    e_flat = expert_id.reshape(-1)
    n_assign = e_flat.shape[0]
    onehot = (e_flat[:, None] == jnp.arange(N_EXPERTS)[None, :]).astype(jnp.int32)
    rank = jnp.take_along_axis(jnp.cumsum(onehot, axis=0), e_flat[:, None], axis=1)[:, 0] - 1
    counts = jnp.sum(onehot, axis=0)
    padded = (counts + MOE_BM - 1) // MOE_BM * MOE_BM
    pad_end = jnp.cumsum(padded)
    dest = (pad_end - padded)[e_flat] + rank
    n_blocks = -(-n_assign // MOE_BM) + N_EXPERTS
    n_rows = n_blocks * MOE_BM
    src_tok = jnp.zeros((n_rows,), jnp.int32).at[dest].set(jnp.arange(n_assign, dtype=jnp.int32) // 2)
    row_w = jnp.zeros((n_rows, 1), F32).at[dest, 0].set(weights.reshape(-1))
    block_e = jnp.minimum(jnp.searchsorted(pad_end, jnp.arange(n_blocks) * MOE_BM, side="right"),
                          N_EXPERTS - 1).astype(jnp.int32)
    n_used = (pad_end[-1] // MOE_BM).astype(jnp.int32).reshape(1)
    xs = _gather_call(tok, src_tok)
    ys = _expert_call(xs, wg, wu, wd, row_w, block_e, n_used)
    return _combine_call(ys, dest.reshape(t, 2).astype(jnp.int32), x1, mod_row, ln2)


def _pad_rows(v, rows=SUBLANES):
    return jnp.pad(v, ((0, rows - v.shape[0]), (0, 0)))


def _layer(x, ctx, c, c_ctx, w_ada, b_ada, w_in, conv_qkv, a_log, dt_bias, gdn_norm_w, conv_b,
           w_branch_a, w_branch_b, w_out, ln1_g, ln1_b, w_router_group, b_router_group,
           w_router_expert, b_router_expert, w_exp_gate, w_exp_up, w_exp_down, ln2_g, ln2_b):
    t = x.shape[0]
    mod = _mod_call(_pad_rows(jnp.stack([c, c_ctx])), w_ada, b_ada.reshape(1, -1))
    mod_lat = mod[0].reshape(N_MOD, D_MODEL)
    mod_ctx = mod[1].reshape(N_MOD, D_MODEL)

    sizes = np.cumsum([3 * GDN_WIDTH, GDN_WIDTH, 2 * HEADS, 2 * HEADS, 3 * CONV_WIDTH])
    w_qkv, w_z, w_beta, w_a, w_xbc, w_gates = jnp.split(w_in, sizes.tolist(), axis=1)
    w_cat = jnp.concatenate(
        [w_qkv, w_z, w_xbc, w_gates, w_beta, w_a, jnp.zeros((D_MODEL, LANES - 4 * HEADS), w_in.dtype)],
        axis=1).astype(BF16)
    lane_row = lambda v: jnp.pad(v.reshape(1, -1), ((0, 0), (2 * HEADS, LANES - 4 * HEADS)))
    alog_row, dtb_row = lane_row(a_log), lane_row(dt_bias)

    h_ctx = _ln_mod_call(ctx, mod_ctx[0:1], mod_ctx[1:2])
    tc = ctx.shape[0]
    qkv_c = _inproj_qkv(h_ctx, w_cat, conv_qkv, tc, tc)
    bg_c = _inproj_ba(h_ctx, w_cat, alog_row, dtb_row, tc)
    s0 = jnp.zeros((2, HEADS, HEAD_DIM, HEAD_DIM), F32)
    _, s_ctx = _gdn(qkv_c, bg_c, s0)

    tm = 512
    h = _ln_mod_call(x, mod_lat[0:1], mod_lat[1:2])
    qkv = _inproj_qkv(h, w_cat, conv_qkv, GRID_W, tm)
    zs = _inproj_z(h, w_cat, gdn_norm_w.reshape(1, HEAD_DIM), tm)
    yconv = _inproj_xbc(h, w_cat, conv_b, GRID_W, tm)
    gates = _inproj_gates(h, w_cat, tm)
    bg = _inproj_ba(h, w_cat, alog_row, dtb_row, tm)
    o, _ = _gdn(qkv, bg, s_ctx)
    m = _merge_call(o, zs, yconv, gates, w_branch_a.astype(BF16), w_branch_b.astype(BF16))

    w_router = jnp.pad(jnp.concatenate([w_router_group, w_router_expert], axis=1),
                       ((0, 0), (0, LANES - N_GROUPS - N_EXPERTS)))
    x1, tok, logits = _out_call(m, w_out.astype(BF16), x, _pad_rows(mod_lat[2:5]),
                                _pad_rows(jnp.stack([ln1_g, ln1_b])), w_router)

    return _moe(tok, logits, b_router_group, b_router_expert, w_exp_gate, w_exp_up, w_exp_down,
                x1, _pad_rows(mod_lat[5:6]), _pad_rows(jnp.stack([ln2_g, ln2_b])))


def kernel(x, c, ctx, c_ctx, w_ada, b_ada, w_in, conv_qkv, a_log, dt_bias, gdn_norm_w, conv_b,
           w_branch_a, w_branch_b, w_out, ln1_g, ln1_b, w_router_group, b_router_group,
           w_router_expert, b_router_expert, w_exp_gate, w_exp_up, w_exp_down, ln2_g, ln2_b):
    assert x.shape[0] == 1 and w_ada.shape[0] == 1, "single batch element, single layer"
    out = _layer(x[0], ctx[0], c[0], c_ctx, w_ada[0], b_ada[0], w_in[0], conv_qkv[0],
                 a_log[0].reshape(-1), dt_bias[0].reshape(-1), gdn_norm_w[0], conv_b[0],
                 w_branch_a[0], w_branch_b[0], w_out[0], ln1_g[0], ln1_b[0],
                 w_router_group[0], b_router_group[0], w_router_expert[0], b_router_expert[0],
                 w_exp_gate[0], w_exp_up[0], w_exp_down[0], ln2_g[0], ln2_b[0])
    return out[None]
```

```python
import functools

import numpy as np
import jax
import jax.numpy as jnp
from jax import lax
from jax.experimental import pallas as pl
from jax.experimental.pallas import tpu as pltpu

F32 = jnp.float32
BF16 = jnp.bfloat16
HIGHEST = lax.Precision.HIGHEST

D_MODEL = 4096
GRID_W = 64
CHUNK = 64
HEADS = 16
HEAD_DIM = 128
GDN_WIDTH = HEADS * HEAD_DIM
CONV_WIDTH = D_MODEL // 2
N_GROUPS = 8
EXPERTS_PER_GROUP = 8
N_EXPERTS = 64
EXPERT_HIDDEN = 768
N_MOD = 6
DEEPNORM_ALPHA = 2.0 ** 0.25
LN_EPS = 1e-6
RMS_EPS = 1e-6

LANES = 128
SUBLANES = 8
UNITS = 128
VMEM_LIMIT = 52 * 1024 * 1024


def _cparams(sem, vmem=VMEM_LIMIT):
    return pltpu.CompilerParams(dimension_semantics=sem, vmem_limit_bytes=vmem)


def _silu(x):
    return x * jax.nn.sigmoid(x)


def _layer_norm(x):
    mu = jnp.mean(x, axis=-1, keepdims=True)
    xc = x - mu
    var = jnp.mean(xc * xc, axis=-1, keepdims=True)
    return xc * lax.rsqrt(var + LN_EPS)


def _mod_kernel(c_ref, w_ref, b_ref, o_ref):
    s = _silu(c_ref[...])
    o_ref[...] = jnp.dot(s, w_ref[...], preferred_element_type=F32, precision=HIGHEST) + b_ref[...]


def _mod_call(cs, w_ada, b_ada):
    n = w_ada.shape[1]
    tn = 512
    return pl.pallas_call(
        _mod_kernel,
        out_shape=jax.ShapeDtypeStruct((SUBLANES, n), F32),
        grid=(n // tn,),
        in_specs=[pl.BlockSpec((SUBLANES, D_MODEL), lambda j: (0, 0)),
                  pl.BlockSpec((D_MODEL, tn), lambda j: (0, j)),
                  pl.BlockSpec((1, tn), lambda j: (0, j))],
        out_specs=pl.BlockSpec((SUBLANES, tn), lambda j: (0, j)),
        compiler_params=_cparams(("arbitrary",)),
        name="mod",
    )(cs, w_ada, b_ada)


def _ln_mod_kernel(x_ref, shift_ref, scale_ref, o_ref):
    y = _layer_norm(x_ref[...])
    o_ref[...] = (y * (1.0 + scale_ref[...]) + shift_ref[...]).astype(o_ref.dtype)


def _ln_mod_call(x, shift, scale):
    t = x.shape[0]
    tr = min(256, t)
    return pl.pallas_call(
        _ln_mod_kernel,
        out_shape=jax.ShapeDtypeStruct((t, D_MODEL), BF16),
        grid=(t // tr,),
        in_specs=[pl.BlockSpec((tr, D_MODEL), lambda i: (i, 0)),
                  pl.BlockSpec((1, D_MODEL), lambda i: (0, 0)),
                  pl.BlockSpec((1, D_MODEL), lambda i: (0, 0))],
        out_specs=pl.BlockSpec((tr, D_MODEL), lambda i: (i, 0)),
        compiler_params=_cparams(("arbitrary",)),
        name="ln_mod",
    )(x, shift, scale)


COL_QKV = 0
COL_Z = 3 * GDN_WIDTH
COL_XB = COL_Z + GDN_WIDTH
COL_BG = COL_XB + CONV_WIDTH
COL_CG = COL_BG + CONV_WIDTH
COL_GATES = COL_CG + CONV_WIDTH
COL_BA = COL_GATES + 2 * D_MODEL
W_CAT_COLS = COL_BA + LANES


def _conv3_rows(x, taps_ref, group):
    rows = x.shape[0]
    pos = lax.broadcasted_iota(jnp.int32, x.shape, 0) % group
    prev = jnp.where(pos == 0, 0.0, pltpu.roll(x, 1, 0))
    nxt = jnp.where(pos == group - 1, 0.0, pltpu.roll(x, rows - 1, 0))
    return taps_ref[0:1, :] * prev + taps_ref[1:2, :] * x + taps_ref[2:3, :] * nxt


def _ip_qkv_kernel(group, x_ref, w_ref, taps_ref, o_ref):
    acc = jnp.dot(x_ref[...], w_ref[...], preferred_element_type=F32)
    y = _silu(_conv3_rows(acc, taps_ref, group))
    j = pl.program_id(0)
    tiles_per_part = GDN_WIDTH // acc.shape[1]
    for hh in range(acc.shape[1] // HEAD_DIM):
        ys = y[:, hh * HEAD_DIM:(hh + 1) * HEAD_DIM]
        r = lax.rsqrt(jnp.sum(ys * ys, axis=-1, keepdims=True) + 1e-6)
        scale = jnp.where(j < tiles_per_part, r * HEAD_DIM ** -0.5,
                          jnp.where(j < 2 * tiles_per_part, r, 1.0))
        o_ref[hh] = (ys * scale).astype(o_ref.dtype)


def _ip_z_kernel(x_ref, w_ref, nw_ref, o_ref):
    acc = jnp.dot(x_ref[...], w_ref[...], preferred_element_type=F32)
    for hh in range(acc.shape[1] // HEAD_DIM):
        zs = acc[:, hh * HEAD_DIM:(hh + 1) * HEAD_DIM]
        o_ref[hh] = _silu(zs) * nw_ref[...]


def _ip_xbc_kernel(group, x_ref, wxb_ref, wbg_ref, wcg_ref, taps_ref, o_ref):
    x = x_ref[...]
    xb = jnp.dot(x, wxb_ref[...], preferred_element_type=F32)
    bg = jnp.dot(x, wbg_ref[...], preferred_element_type=F32)
    cg = jnp.dot(x, wcg_ref[...], preferred_element_type=F32)
    o_ref[...] = (bg * _conv3_rows(cg * xb, taps_ref, group)).astype(o_ref.dtype)


def _ip_gates_kernel(x_ref, w_ref, o_ref):
    acc = jnp.dot(x_ref[...], w_ref[...], preferred_element_type=F32)
    o_ref[...] = jax.nn.sigmoid(acc).astype(o_ref.dtype)


def _ip_ba_kernel(x_ref, w_ref, alog_ref, dtb_ref, o_ref):
    acc = jnp.dot(x_ref[...], w_ref[...], preferred_element_type=F32)
    lane = lax.broadcasted_iota(jnp.int32, acc.shape, 1)
    a = acc + dtb_ref[...]
    softplus = jnp.maximum(a, 0.0) + jnp.log(1.0 + jnp.exp(-jnp.abs(a)))
    o_ref[...] = jnp.where(lane < 2 * HEADS, jax.nn.sigmoid(acc), -jnp.exp(alog_ref[...]) * softplus)


def _inproj(kernel, h, w_cat, col_starts, tn, n_tiles, extra, extra_specs, out_shape, out_spec, tm):
    t = h.shape[0]
    w_specs = [pl.BlockSpec((D_MODEL, tn), functools.partial(lambda j, i, o: (0, o + j), o=c // tn))
               for c in col_starts]
    return pl.pallas_call(
        kernel,
        out_shape=out_shape,
        grid=(n_tiles, t // tm),
        in_specs=[pl.BlockSpec((tm, D_MODEL), lambda j, i: (i, 0))] + w_specs + extra_specs,
        out_specs=out_spec,
        compiler_params=_cparams(("arbitrary", "arbitrary")),
        name="inproj",
    )(h, *([w_cat] * len(col_starts)), *extra)


def _inproj_qkv(h, w_cat, taps, group, tm):
    t = h.shape[0]
    tn = 1024
    hpt = tn // HEAD_DIM
    return _inproj(
        functools.partial(_ip_qkv_kernel, group), h, w_cat, [COL_QKV], tn, 3 * GDN_WIDTH // tn,
        [taps], [pl.BlockSpec((3, tn), lambda j, i: (0, j))],
        jax.ShapeDtypeStruct((3 * HEADS, t, HEAD_DIM), BF16),
        pl.BlockSpec((hpt, tm, HEAD_DIM), lambda j, i: (j, i, 0)), tm)


def _inproj_z(h, w_cat, norm_w, tm):
    t = h.shape[0]
    tn = 1024
    hpt = tn // HEAD_DIM
    return _inproj(
        _ip_z_kernel, h, w_cat, [COL_Z], tn, GDN_WIDTH // tn,
        [norm_w], [pl.BlockSpec((1, HEAD_DIM), lambda j, i: (0, 0))],
        jax.ShapeDtypeStruct((HEADS, t, HEAD_DIM), F32),
        pl.BlockSpec((hpt, tm, HEAD_DIM), lambda j, i: (j, i, 0)), tm)


def _inproj_xbc(h, w_cat, taps, group, tm):
    t = h.shape[0]
    tn = 512
    return _inproj(
        functools.partial(_ip_xbc_kernel, group), h, w_cat, [COL_XB, COL_BG, COL_CG], tn,
        CONV_WIDTH // tn, [taps], [pl.BlockSpec((3, tn), lambda j, i: (0, j))],
        jax.ShapeDtypeStruct((t, CONV_WIDTH), BF16),
        pl.BlockSpec((tm, tn), lambda j, i: (i, j)), tm)


def _inproj_gates(h, w_cat, tm):
    t = h.shape[0]
    tn = 1024
    return _inproj(
        _ip_gates_kernel, h, w_cat, [COL_GATES], tn, 2 * D_MODEL // tn, [], [],
        jax.ShapeDtypeStruct((t, 2 * D_MODEL), BF16),
        pl.BlockSpec((tm, tn), lambda j, i: (i, j)), tm)


def _inproj_ba(h, w_cat, alog_row, dtb_row, tm):
    t = h.shape[0]
    return _inproj(
        _ip_ba_kernel, h, w_cat, [COL_BA], LANES, 1, [alog_row, dtb_row],
        [pl.BlockSpec((1, LANES), lambda j, i: (0, 0))] * 2,
        jax.ShapeDtypeStruct((t, LANES), F32),
        pl.BlockSpec((tm, LANES), lambda j, i: (i, 0)), tm)


def _gdn_scal_kernel(g_ref, gc_ref, ep_ref, egl_ref):
    d = pl.program_id(0)
    r = lax.broadcasted_iota(jnp.int32, (CHUNK, CHUNK), 0)
    c = lax.broadcasted_iota(jnp.int32, (CHUNK, CHUNK), 1)
    tri = jnp.where(d == 0, (c <= r).astype(F32), (c >= r).astype(F32))
    gc = jnp.dot(tri, g_ref[0], preferred_element_type=F32, precision=HIGHEST)
    gl = jnp.where(d == 0, gc[CHUNK - 1:CHUNK, :], gc[0:1, :])
    gc_ref[0] = gc
    ep_ref[0] = jnp.exp(gl - gc)
    egl_ref[0] = jnp.exp(gl)


def _gdn_scal_call(g_s):
    u = g_s.shape[2]
    tu = 512 if u % 512 == 0 else UNITS
    spec = pl.BlockSpec((1, CHUNK, tu), lambda d, i: (d, 0, i))
    return pl.pallas_call(
        _gdn_scal_kernel,
        out_shape=(jax.ShapeDtypeStruct(g_s.shape, F32), jax.ShapeDtypeStruct(g_s.shape, F32),
                   jax.ShapeDtypeStruct((2, 1, u), F32)),
        grid=(2, u // tu),
        in_specs=[spec],
        out_specs=(spec, spec, pl.BlockSpec((1, 1, tu), lambda d, i: (d, 0, i))),
        compiler_params=_cparams(("arbitrary", "arbitrary")),
        name="gdn_scal",
    )(g_s)


def _nt_dot(a, b):
    return lax.dot_general(a, b, (((1,), (1,)), ((), ())), preferred_element_type=F32)


def _gdn_gram_kernel(cb, q_ref, k_ref, g_ref, qk_ref, kt_ref):
    r = lax.broadcasted_iota(jnp.int32, (HEAD_DIM, HEAD_DIM), 0)
    c = lax.broadcasted_iota(jnp.int32, (HEAD_DIM, HEAD_DIM), 1)
    eye = (r == c).astype(BF16)
    for ci in range(cb):
        k = k_ref[0, ci * CHUNK:(ci + 1) * CHUNK, :]
        q = q_ref[0, ci * CHUNK:(ci + 1) * CHUNK, :]
        g_ref[ci] = _nt_dot(k, k)
        qk_ref[ci] = _nt_dot(q, k)
        kt_ref[ci] = _nt_dot(eye, k).astype(BF16)


def _gdn_gram_call(qkv, nc):
    cb = min(8, nc)
    ncb = nc // cb
    u = HEADS * nc
    return pl.pallas_call(
        functools.partial(_gdn_gram_kernel, cb),
        out_shape=(jax.ShapeDtypeStruct((u, CHUNK, CHUNK), F32),
                   jax.ShapeDtypeStruct((u, CHUNK, CHUNK), F32),
                   jax.ShapeDtypeStruct((u, HEAD_DIM, CHUNK), BF16)),
        grid=(HEADS, ncb),
        in_specs=[pl.BlockSpec((1, cb * CHUNK, HEAD_DIM), lambda h, c: (h, c, 0)),
                  pl.BlockSpec((1, cb * CHUNK, HEAD_DIM), lambda h, c: (HEADS + h, c, 0))],
        out_specs=(pl.BlockSpec((cb, CHUNK, CHUNK), lambda h, c: (h * ncb + c, 0, 0)),
                   pl.BlockSpec((cb, CHUNK, CHUNK), lambda h, c: (h * ncb + c, 0, 0)),
                   pl.BlockSpec((cb, HEAD_DIM, CHUNK), lambda h, c: (h * ncb + c, 0, 0))),
        compiler_params=_cparams(("arbitrary", "arbitrary")),
        name="gdn_gram",
    )(qkv, qkv)


NB = CHUNK // SUBLANES


def _row_bcast(ref, row):
    return jnp.broadcast_to(ref[pl.ds(row, 1), :], (SUBLANES, UNITS))


def _gdn_inv_kernel(bwd, g_ref, qk_ref, beta_ref, gc_ref, twtu_ref, atde_ref,
                    g_s, qk_s, l_s, t_s, tw_s, tu_s, at_s, de_s, e_s):
    pos = (lambda a: CHUNK - 1 - a) if bwd else (lambda a: a)
    blk = (lambda b: NB - 1 - b) if bwd else (lambda b: b)

    for pp in range(CHUNK // 2):
        sl = slice(pp * LANES, (pp + 1) * LANES)
        g_s[sl, :] = g_ref[:, sl].T
        qk_s[sl, :] = qk_ref[:, sl].T
    e_s[...] = jnp.exp(gc_ref[0])

    @pl.when(pl.program_id(0) == 0)
    def _():
        for ref in (tw_s, tu_s, at_s, de_s):
            ref[...] = jnp.zeros(ref.shape, F32)

    sub = lax.broadcasted_iota(jnp.int32, (SUBLANES, UNITS), 0)
    zero = jnp.zeros((SUBLANES, UNITS), F32)
    rs = range(SUBLANES)
    own = [(SUBLANES - 1 - r) if bwd else r for r in rs]
    earlier = [(sub > own[r]) if bwd else (sub < own[r]) for r in rs]

    def tile_ds(p, b):
        return pl.ds(pl.multiple_of(p * CHUNK + b * SUBLANES, SUBLANES), SUBLANES)

    def cols_ds(b):
        return pl.ds(pl.multiple_of(b * SUBLANES, SUBLANES), SUBLANES)

    def row_block(ib, carry):
        b_own = blk(ib)
        ps = [pos(ib * SUBLANES + r) for r in rs]
        gc_p = [_row_bcast(gc_ref.at[0], p) for p in ps]
        beta_p = [_row_bcast(beta_ref.at[0], p) for p in ps]

        def weights_offdiag(bc, c):
            b = blk(bc)
            gc_c = gc_ref[0, cols_ds(b), :]
            for r in rs:
                dec = jnp.exp(gc_p[r] - gc_c)
                l_s[tile_ds(ps[r], b), :] = beta_p[r] * g_s[tile_ds(ps[r], b), :] * dec
                at_s[tile_ds(ps[r], b), :] = qk_s[tile_ds(ps[r], b), :] * dec
            return c

        lax.fori_loop(0, ib, weights_offdiag, 0)
        gc_c = gc_ref[0, cols_ds(b_own), :]
        for r in rs:
            t = tile_ds(ps[r], b_own)
            dec = jnp.exp(gc_p[r] - gc_c)
            l_s[t, :] = jnp.where(earlier[r], beta_p[r] * g_s[t, :] * dec, 0.0)
            at_s[t, :] = jnp.where(sub == own[r], qk_s[t, :], jnp.where(earlier[r], qk_s[t, :] * dec, 0.0))
            de_s[t, :] = jnp.where(sub == own[r], _row_bcast(e_s, ps[r]), 0.0)

        def finish(b, acc):
            done = []
            for r in rs:
                a_r = acc[r]
                for kk in range(r):
                    a_r = a_r - _row_bcast(l_s, ps[r] * CHUNK + ps[kk]) * done[kk]
                done.append(a_r)
            beta_c = beta_ref[0, cols_ds(b), :]
            be_c = beta_c * e_s[cols_ds(b), :]
            for r in rs:
                t_s[tile_ds(ps[r], b), :] = done[r]
                tu_s[tile_ds(ps[r], b), :] = done[r] * beta_c
                tw_s[tile_ds(ps[r], b), :] = done[r] * be_c

        def subst_offdiag(bc, c):
            b = blk(bc)

            def k_block(kb, acc):
                acc = list(acc)
                for kk in rs:
                    pk = pos(kb * SUBLANES + kk)
                    t_k = t_s[tile_ds(pk, b), :]
                    for r in rs:
                        acc[r] = acc[r] - _row_bcast(l_s, ps[r] * CHUNK + pk) * t_k
                return tuple(acc)

            finish(b, lax.fori_loop(bc, ib, k_block, (zero,) * SUBLANES))
            return c

        lax.fori_loop(0, ib, subst_offdiag, 0)
        finish(b_own, [jnp.where(sub == own[r], 1.0, 0.0) for r in rs])
        return carry

    lax.fori_loop(0, NB, row_block, 0)

    for p in range(CHUNK):
        rows = slice(p * CHUNK, (p + 1) * CHUNK)
        sl = slice(p * LANES, (p + 1) * LANES)
        twtu_ref[:, sl] = jnp.concatenate([tw_s[rows, :], tu_s[rows, :]], axis=0).T.astype(BF16)
        atde_ref[:, sl] = jnp.concatenate([at_s[rows, :], de_s[rows, :]], axis=0).T.astype(BF16)


def _gdn_inv_call(bwd, g_flat, qk_flat, beta_s, gc_s):
    u = g_flat.shape[0]
    d = 1 if bwd else 0
    mat = pl.BlockSpec((UNITS, CHUNK * CHUNK), lambda i: (i, 0))
    sc = pl.BlockSpec((1, CHUNK, UNITS), lambda i: (d, 0, i))
    out = pl.BlockSpec((UNITS, CHUNK * LANES), lambda i: (i, 0))
    soa = pltpu.VMEM((CHUNK * CHUNK, UNITS), F32)
    return pl.pallas_call(
        functools.partial(_gdn_inv_kernel, bwd),
        out_shape=(jax.ShapeDtypeStruct((u, CHUNK * LANES), BF16),
                   jax.ShapeDtypeStruct((u, CHUNK * LANES), BF16)),
        grid=(u // UNITS,),
        in_specs=[mat, mat, sc, sc],
        out_specs=(out, out),
        scratch_shapes=[soa] * 8 + [pltpu.VMEM((CHUNK, UNITS), F32)],
        compiler_params=_cparams(("arbitrary",)),
        name="gdn_inv_bwd" if bwd else "gdn_inv_fwd",
    )(g_flat, qk_flat, beta_s, gc_s)


def _gdn_scan_kernel(hb, cb, egl_ref, q_ref, k_ref, v_ref, kt_ref, twtu_ref, atde_ref, ep_ref, s0_ref,
                     o_ref, sfin_ref, s_scr):
    d = pl.program_id(0)
    hg = pl.program_id(1)
    ci = pl.program_id(2)
    ncb = pl.num_programs(2)

    @pl.when(ci == 0)
    def _():
        s_scr[...] = s0_ref[0]

    cblk = ci + d * (ncb - 1 - 2 * ci)
    zeros = jnp.zeros((CHUNK, HEAD_DIM), BF16)

    def chunk_step(cc, carry):
        c = cc + d * (cb - 1 - 2 * cc)
        r0 = pl.multiple_of(c * CHUNK, CHUNK)
        for hh in range(hb):
            k = k_ref[hh, pl.ds(r0, CHUNK), :]
            q = q_ref[hh, pl.ds(r0, CHUNK), :]
            v = v_ref[hh, pl.ds(r0, CHUNK), :]
            rhs = jnp.concatenate([jnp.concatenate([k, zeros], axis=1),
                                   jnp.concatenate([zeros, v], axis=1)], axis=0)
            wu = jnp.dot(twtu_ref[0, hh, c], rhs, preferred_element_type=F32)
            w = wu[:, :HEAD_DIM]
            u = wu[:, HEAD_DIM:]
            s = s_scr[hh]
            x = jnp.dot(jnp.concatenate([w.astype(BF16), q], axis=0), s.astype(BF16),
                        preferred_element_type=F32)
            v_new = u - x[:CHUNK]
            o = jnp.dot(atde_ref[0, hh, c], jnp.concatenate([v_new, x[CHUNK:]], axis=0).astype(BF16),
                        preferred_element_type=F32)
            o_ref[0, hh, pl.ds(r0, CHUNK), :] = o
            kdt = (kt_ref[hh, c].astype(F32) * ep_ref[0, hh, pl.ds(c, 1), :]).astype(BF16)
            egl = egl_ref[d, hg * hb + hh, cblk * cb + c]
            s_scr[hh] = egl * s + jnp.dot(kdt, v_new.astype(BF16), preferred_element_type=F32)
        return carry

    lax.fori_loop(0, cb, chunk_step, 0)

    @pl.when(ci == ncb - 1)
    def _():
        sfin_ref[0] = s_scr[...]


def _gdn_scan_call(qkv, kt, twtu, atde, ep, egl, s0, nc):
    t = qkv.shape[1]
    hb = 4
    cb = min(8, nc)
    ncb = nc // cb
    hgs = HEADS // hb
    nat = lambda d, c: c + d * (ncb - 1 - 2 * c)
    tok = lambda part: pl.BlockSpec((hb, cb * CHUNK, HEAD_DIM),
                                    lambda d, hg, c, egl, part=part: (part * hgs + hg, nat(d, c), 0))
    per_dir = lambda last: pl.BlockSpec((1, hb, cb, CHUNK, last),
                                        lambda d, hg, c, egl: (d, hg, nat(d, c), 0, 0))
    grid_spec = pltpu.PrefetchScalarGridSpec(
        num_scalar_prefetch=1,
        grid=(2, hgs, ncb),
        in_specs=[tok(0), tok(1), tok(2),
                  pl.BlockSpec((hb, cb, HEAD_DIM, CHUNK), lambda d, hg, c, egl: (hg, nat(d, c), 0, 0)),
                  per_dir(LANES), per_dir(LANES),
                  pl.BlockSpec((1, hb, cb, CHUNK), lambda d, hg, c, egl: (d, hg, nat(d, c), 0)),
                  pl.BlockSpec((1, hb, HEAD_DIM, HEAD_DIM), lambda d, hg, c, egl: (d, hg, 0, 0))],
        out_specs=(pl.BlockSpec((1, hb, cb * CHUNK, HEAD_DIM), lambda d, hg, c, egl: (d, hg, nat(d, c), 0)),
                   pl.BlockSpec((1, hb, HEAD_DIM, HEAD_DIM), lambda d, hg, c, egl: (d, hg, 0, 0))),
        scratch_shapes=[pltpu.VMEM((hb, HEAD_DIM, HEAD_DIM), F32)],
    )
    return pl.pallas_call(
        functools.partial(_gdn_scan_kernel, hb, cb),
        out_shape=(jax.ShapeDtypeStruct((2, HEADS, t, HEAD_DIM), F32),
                   jax.ShapeDtypeStruct((2, HEADS, HEAD_DIM, HEAD_DIM), F32)),
        grid_spec=grid_spec,
        compiler_params=_cparams(("arbitrary", "arbitrary", "arbitrary")),
        name="gdn_scan",
    )(egl, qkv, qkv, qkv, kt.reshape(HEADS, nc, HEAD_DIM, CHUNK),
      twtu.reshape(2, HEADS, nc, CHUNK, LANES), atde.reshape(2, HEADS, nc, CHUNK, LANES), ep, s0)


def _gdn(qkv, bg, s0):
    t = qkv.shape[1]
    nc = t // CHUNK
    u = HEADS * nc
    up = -(-u // UNITS) * UNITS
    to_soa = lambda a: jnp.pad(jnp.transpose(a.reshape(nc, CHUNK, 2, HEADS), (2, 1, 3, 0)).reshape(2, CHUNK, u),
                               ((0, 0), (0, 0), (0, up - u)))
    beta_s = to_soa(bg[:, :2 * HEADS])
    g_s = to_soa(bg[:, 2 * HEADS:4 * HEADS])
    gc_s, ep_s, egl_s = _gdn_scal_call(g_s)
    gram, qk, kt = _gdn_gram_call(qkv, nc)
    pad_u = lambda a: jnp.pad(a.reshape(u, -1), ((0, up - u), (0, 0)))
    g_flat, qk_flat = pad_u(gram), pad_u(qk)
    outs = [_gdn_inv_call(bwd, g_flat, qk_flat, beta_s, gc_s) for bwd in (False, True)]
    twtu = jnp.stack([o[0][:u] for o in outs])
    atde = jnp.stack([o[1][:u] for o in outs])
    ep = jnp.transpose(ep_s[:, :, :u].reshape(2, CHUNK, HEADS, nc), (0, 2, 3, 1))
    egl = egl_s[:, 0, :u].reshape(2, HEADS, nc)
    return _gdn_scan_call(qkv, kt, twtu, atde, ep, egl, s0, nc)


def _merge_kernel(o_ref, zs_ref, yc_ref, ga_ref, gb_ref, wa_ref, wb_ref, m_ref, yg_s):
    @pl.when(pl.program_id(1) == 0)
    def _():
        for hh in range(HEADS):
            o = o_ref[0, hh] + o_ref[1, hh]
            y = o * lax.rsqrt(jnp.mean(o * o, axis=-1, keepdims=True) + RMS_EPS) * zs_ref[hh]
            yg_s[:, hh * HEAD_DIM:(hh + 1) * HEAD_DIM] = y.astype(BF16)

    pa = jnp.dot(yg_s[...], wa_ref[...], preferred_element_type=F32)
    pb = jnp.dot(yc_ref[...], wb_ref[...], preferred_element_type=F32)
    m_ref[...] = (ga_ref[...].astype(F32) * pa + gb_ref[...].astype(F32) * pb).astype(m_ref.dtype)


def _merge_call(o, zs, yconv, gates, wa, wb):
    t = yconv.shape[0]
    tm, tn = 512, 512
    nj = D_MODEL // tn
    return pl.pallas_call(
        _merge_kernel,
        out_shape=jax.ShapeDtypeStruct((t, D_MODEL), BF16),
        grid=(t // tm, nj),
        in_specs=[pl.BlockSpec((2, HEADS, tm, HEAD_DIM), lambda i, j: (0, 0, i, 0)),
                  pl.BlockSpec((HEADS, tm, HEAD_DIM), lambda i, j: (0, i, 0)),
                  pl.BlockSpec((tm, CONV_WIDTH), lambda i, j: (i, 0)),
                  pl.BlockSpec((tm, tn), lambda i, j: (i, j)),
                  pl.BlockSpec((tm, tn), lambda i, j: (i, nj + j)),
                  pl.BlockSpec((GDN_WIDTH, tn), lambda i, j: (0, j)),
                  pl.BlockSpec((CONV_WIDTH, tn), lambda i, j: (0, j))],
        out_specs=pl.BlockSpec((tm, tn), lambda i, j: (i, j)),
        scratch_shapes=[pltpu.VMEM((tm, GDN_WIDTH), BF16)],
        compiler_params=_cparams(("arbitrary", "arbitrary")),
        name="merge",
    )(o, zs, yconv, gates, gates, wa, wb)


def _out_kernel(m_ref, w_ref, x_ref, mod_ref, ln_ref, wr_ref, x1_ref, tok_ref, logit_ref, acc_s):
    kk = pl.program_id(1)

    @pl.when(kk == 0)
    def _():
        acc_s[...] = jnp.zeros(acc_s.shape, F32)

    acc_s[...] += jnp.dot(m_ref[...], w_ref[...], preferred_element_type=F32)

    @pl.when(kk == pl.num_programs(1) - 1)
    def _():
        x1 = _layer_norm(DEEPNORM_ALPHA * x_ref[...] + mod_ref[0:1, :] * acc_s[...])
        x1 = x1 * ln_ref[0:1, :] + ln_ref[1:2, :]
        x1_ref[...] = x1
        tok = _layer_norm(x1) * (1.0 + mod_ref[2:3, :]) + mod_ref[1:2, :]
        tok_ref[...] = tok
        logit_ref[...] = jnp.dot(tok, wr_ref[...], preferred_element_type=F32, precision=HIGHEST)


def _out_call(m, w_out, x, mod3, ln1, w_router):
    t = m.shape[0]
    tm, tk = 256, 512
    return pl.pallas_call(
        _out_kernel,
        out_shape=(jax.ShapeDtypeStruct((t, D_MODEL), F32),
                   jax.ShapeDtypeStruct((t, D_MODEL), F32),
                   jax.ShapeDtypeStruct((t, LANES), F32)),
        grid=(t // tm, D_MODEL // tk),
        in_specs=[pl.BlockSpec((tm, tk), lambda i, k: (i, k)),
                  pl.BlockSpec((tk, D_MODEL), lambda i, k: (k, 0)),
                  pl.BlockSpec((tm, D_MODEL), lambda i, k: (i, 0)),
                  pl.BlockSpec((SUBLANES, D_MODEL), lambda i, k: (0, 0)),
                  pl.BlockSpec((SUBLANES, D_MODEL), lambda i, k: (0, 0)),
                  pl.BlockSpec((D_MODEL, LANES), lambda i, k: (0, 0))],
        out_specs=(pl.BlockSpec((tm, D_MODEL), lambda i, k: (i, 0)),
                   pl.BlockSpec((tm, D_MODEL), lambda i, k: (i, 0)),
                   pl.BlockSpec((tm, LANES), lambda i, k: (i, 0))),
        scratch_shapes=[pltpu.VMEM((tm, D_MODEL), F32)],
        compiler_params=_cparams(("arbitrary", "arbitrary")),
        name="out_proj",
    )(m, w_out, x, mod3, ln1, w_router)


MOE_BM = 256
MOE_HC = 256


def _gather_kernel(rows, idx_ref, src_ref, dst_ref, sem):
    def copy(r):
        return pltpu.make_async_copy(src_ref.at[pl.ds(idx_ref[0, 0, r], 1), :],
                                     dst_ref.at[pl.ds(r, 1), :], sem)

    def start(r, c):
        copy(r).start()
        return c

    def wait(r, c):
        copy(r).wait()
        return c

    lax.fori_loop(0, rows, start, 0)
    lax.fori_loop(0, rows, wait, 0)


def _gather_call(src, idx):
    n = idx.shape[0]
    rows = 256
    return pl.pallas_call(
        functools.partial(_gather_kernel, rows),
        out_shape=jax.ShapeDtypeStruct((n, src.shape[1]), src.dtype),
        grid=(n // rows,),
        in_specs=[pl.BlockSpec((1, 1, rows), lambda i: (i, 0, 0), memory_space=pltpu.SMEM),
                  pl.BlockSpec(memory_space=pl.ANY)],
        out_specs=pl.BlockSpec((rows, src.shape[1]), lambda i: (i, 0)),
        scratch_shapes=[pltpu.SemaphoreType.DMA(())],
        compiler_params=_cparams(("arbitrary",)),
        name="moe_gather",
    )(idx.reshape(n // rows, 1, rows), src)


def _expert_kernel(be_ref, nu_ref, x_ref, wg_ref, wu_ref, wd_ref, rw_ref, y_ref):
    b = pl.program_id(0)
    hc = pl.program_id(1)

    @pl.when((b >= nu_ref[0]) & (hc == 0))
    def _():
        y_ref[...] = jnp.zeros(y_ref.shape, F32)

    @pl.when(b < nu_ref[0])
    def _():
        x = x_ref[...].astype(BF16)
        hg = jnp.dot(x, wg_ref[0].astype(BF16), preferred_element_type=F32)
        hu = jnp.dot(x, wu_ref[0].astype(BF16), preferred_element_type=F32)
        hid = (_silu(hg) * hu).astype(BF16)
        part = jnp.dot(hid, wd_ref[0].astype(BF16), preferred_element_type=F32)

        @pl.when(hc == 0)
        def _():
            y_ref[...] = part

        @pl.when(hc > 0)
        def _():
            y_ref[...] += part

        @pl.when(hc == pl.num_programs(1) - 1)
        def _():
            y_ref[...] = y_ref[...] * rw_ref[...]


def _expert_call(xs, wg, wu, wd, row_w, block_e, n_used):
    n = xs.shape[0]
    nb = n // MOE_BM
    nh = EXPERT_HIDDEN // MOE_HC
    blk = lambda b, nu: jnp.minimum(b, nu[0] - 1)
    grid_spec = pltpu.PrefetchScalarGridSpec(
        num_scalar_prefetch=2,
        grid=(nb, nh),
        in_specs=[pl.BlockSpec((MOE_BM, D_MODEL), lambda b, h, be, nu: (blk(b, nu), 0)),
                  pl.BlockSpec((1, D_MODEL, MOE_HC), lambda b, h, be, nu: (be[blk(b, nu)], 0, h)),
                  pl.BlockSpec((1, D_MODEL, MOE_HC), lambda b, h, be, nu: (be[blk(b, nu)], 0, h)),
                  pl.BlockSpec((1, MOE_HC, D_MODEL), lambda b, h, be, nu: (be[blk(b, nu)], h, 0)),
                  pl.BlockSpec((MOE_BM, 1), lambda b, h, be, nu: (blk(b, nu), 0))],
        out_specs=pl.BlockSpec((MOE_BM, D_MODEL), lambda b, h, be, nu: (b, 0)),
    )
    return pl.pallas_call(
        _expert_kernel,
        out_shape=jax.ShapeDtypeStruct((n, D_MODEL), F32),
        grid_spec=grid_spec,
        compiler_params=_cparams(("arbitrary", "arbitrary")),
        name="moe_experts",
    )(block_e, n_used, xs, wg, wu, wd, row_w)


def _combine_kernel(rows, d_ref, y_ref, x1_ref, mod_ref, ln_ref, o_ref, buf, sem):
    def copy(slot, r):
        return pltpu.make_async_copy(y_ref.at[pl.ds(d_ref[0, slot, r], 1), :],
                                     buf.at[slot, pl.ds(r, 1), :], sem)

    def start(r, c):
        copy(0, r).start()
        copy(1, r).start()
        return c

    def wait(r, c):
        copy(0, r).wait()
        copy(1, r).wait()
        return c

    lax.fori_loop(0, rows, start, 0)
    lax.fori_loop(0, rows, wait, 0)
    f = buf[0] + buf[1]
    y = _layer_norm(DEEPNORM_ALPHA * x1_ref[...] + mod_ref[0:1, :] * f)
    o_ref[...] = y * ln_ref[0:1, :] + ln_ref[1:2, :]


def _combine_call(ys, dest, x1, mod_row, ln2):
    t = x1.shape[0]
    rows = 128
    dest3 = jnp.transpose(dest.reshape(t // rows, rows, 2), (0, 2, 1))
    return pl.pallas_call(
        functools.partial(_combine_kernel, rows),
        out_shape=jax.ShapeDtypeStruct((t, D_MODEL), F32),
        grid=(t // rows,),
        in_specs=[pl.BlockSpec((1, 2, rows), lambda i: (i, 0, 0), memory_space=pltpu.SMEM),
                  pl.BlockSpec(memory_space=pl.ANY),
                  pl.BlockSpec((rows, D_MODEL), lambda i: (i, 0)),
                  pl.BlockSpec((SUBLANES, D_MODEL), lambda i: (0, 0)),
                  pl.BlockSpec((SUBLANES, D_MODEL), lambda i: (0, 0))],
        out_specs=pl.BlockSpec((rows, D_MODEL), lambda i: (i, 0)),
        scratch_shapes=[pltpu.VMEM((2, rows, D_MODEL), F32), pltpu.SemaphoreType.DMA(())],
        compiler_params=_cparams(("arbitrary",)),
        name="moe_combine",
    )(dest3, ys, x1, mod_row, ln2)


def _route(logits, b_group, b_expert):
    t = logits.shape[0]
    p_group = jax.nn.softmax(logits[:, :N_GROUPS] + b_group, axis=-1)
    group = jnp.argmax(p_group, axis=-1)
    gate_group = jnp.take_along_axis(p_group, group[:, None], axis=-1)
    le = (logits[:, N_GROUPS:N_GROUPS + N_EXPERTS] + b_expert).reshape(t, N_GROUPS, EXPERTS_PER_GROUP)
    le = jnp.take_along_axis(le, group[:, None, None], axis=1)[:, 0]
    top_p, top_i = lax.top_k(jax.nn.softmax(le, axis=-1), 2)
    weights = gate_group * top_p / jnp.sum(top_p, axis=-1, keepdims=True)
    expert_id = group[:, None] * EXPERTS_PER_GROUP + top_i
    return expert_id.astype(jnp.int32), weights


def _moe(tok, logits, b_group, b_expert, wg, wu, wd, x1, mod_row, ln2):
    t = tok.shape[0]
    expert_id, weights = _route(logits, b_group, b_expert)
    e_flat = expert_id.reshape(-1)
    n_assign = e_flat.shape[0]
    onehot = (e_flat[:, None] == jnp.arange(N_EXPERTS)[None, :]).astype(jnp.int32)
    rank = jnp.take_along_axis(jnp.cumsum(onehot, axis=0), e_flat[:, None], axis=1)[:, 0] - 1
    counts = jnp.sum(onehot, axis=0)
    padded = (counts + MOE_BM - 1) // MOE_BM * MOE_BM
    pad_end = jnp.cumsum(padded)
    dest = (pad_end - padded)[e_flat] + rank
    n_blocks = -(-n_assign // MOE_BM) + N_EXPERTS
    n_rows = n_blocks * MOE_BM
    src_tok = jnp.zeros((n_rows,), jnp.int32).at[dest].set(jnp.arange(n_assign, dtype=jnp.int32) // 2)
    row_w = jnp.zeros((n_rows, 1), F32).at[dest, 0].set(weights.reshape(-1))
    block_e = jnp.minimum(jnp.searchsorted(pad_end, jnp.arange(n_blocks) * MOE_BM, side="right"),
                          N_EXPERTS - 1).astype(jnp.int32)
    n_used = (pad_end[-1] // MOE_BM).astype(jnp.int32).reshape(1)
    xs = _gather_call(tok, src_tok)
    ys = _expert_call(xs, wg, wu, wd, row_w, block_e, n_used)
    return _combine_call(ys, dest.reshape(t, 2).astype(jnp.int32), x1, mod_row, ln2)


def _pad_rows(v, rows=SUBLANES):
    return jnp.pad(v, ((0, rows - v.shape[0]), (0, 0)))


def _layer(x, ctx, c, c_ctx, w_ada, b_ada, w_in, conv_qkv, a_log, dt_bias, gdn_norm_w, conv_b,
           w_branch_a, w_branch_b, w_out, ln1_g, ln1_b, w_router_group, b_router_group,
           w_router_expert, b_router_expert, w_exp_gate, w_exp_up, w_exp_down, ln2_g, ln2_b):
    t = x.shape[0]
    mod = _mod_call(_pad_rows(jnp.stack([c, c_ctx])), w_ada, b_ada.reshape(1, -1))
    mod_lat = mod[0].reshape(N_MOD, D_MODEL)
    mod_ctx = mod[1].reshape(N_MOD, D_MODEL)

    sizes = np.cumsum([3 * GDN_WIDTH, GDN_WIDTH, 2 * HEADS, 2 * HEADS, 3 * CONV_WIDTH])
    w_qkv, w_z, w_beta, w_a, w_xbc, w_gates = jnp.split(w_in, sizes.tolist(), axis=1)
    w_cat = jnp.concatenate(
        [w_qkv, w_z, w_xbc, w_gates, w_beta, w_a, jnp.zeros((D_MODEL, LANES - 4 * HEADS), w_in.dtype)],
        axis=1).astype(BF16)
    lane_row = lambda v: jnp.pad(v.reshape(1, -1), ((0, 0), (2 * HEADS, LANES - 4 * HEADS)))
    alog_row, dtb_row = lane_row(a_log), lane_row(dt_bias)

    h_ctx = _ln_mod_call(ctx, mod_ctx[0:1], mod_ctx[1:2])
    tc = ctx.shape[0]
    qkv_c = _inproj_qkv(h_ctx, w_cat, conv_qkv, tc, tc)
    bg_c = _inproj_ba(h_ctx, w_cat, alog_row, dtb_row, tc)
    s0 = jnp.zeros((2, HEADS, HEAD_DIM, HEAD_DIM), F32)
    _, s_ctx = _gdn(qkv_c, bg_c, s0)

    tm = 512
    h = _ln_mod_call(x, mod_lat[0:1], mod_lat[1:2])
    qkv = _inproj_qkv(h, w_cat, conv_qkv, GRID_W, tm)
    zs = _inproj_z(h, w_cat, gdn_norm_w.reshape(1, HEAD_DIM), tm)
    yconv = _inproj_xbc(h, w_cat, conv_b, GRID_W, tm)
    gates = _inproj_gates(h, w_cat, tm)
    bg = _inproj_ba(h, w_cat, alog_row, dtb_row, tm)
    o, _ = _gdn(qkv, bg, s_ctx)
    m = _merge_call(o, zs, yconv, gates, w_branch_a.astype(BF16), w_branch_b.astype(BF16))

    w_router = jnp.pad(jnp.concatenate([w_router_group, w_router_expert], axis=1),
                       ((0, 0), (0, LANES - N_GROUPS - N_EXPERTS)))
    x1, tok, logits = _out_call(m, w_out.astype(BF16), x, _pad_rows(mod_lat[2:5]),
                                _pad_rows(jnp.stack([ln1_g, ln1_b])), w_router)

    return _moe(tok, logits, b_router_group, b_router_expert, w_exp_gate, w_exp_up, w_exp_down,
                x1, _pad_rows(mod_lat[5:6]), _pad_rows(jnp.stack([ln2_g, ln2_b])))


def kernel(x, c, ctx, c_ctx, w_ada, b_ada, w_in, conv_qkv, a_log, dt_bias, gdn_norm_w, conv_b,
           w_branch_a, w_branch_b, w_out, ln1_g, ln1_b, w_router_group, b_router_group,
           w_router_expert, b_router_expert, w_exp_gate, w_exp_up, w_exp_down, ln2_g, ln2_b):
    assert x.shape[0] == 1 and w_ada.shape[0] == 1, "single batch element, single layer"
    out = _layer(x[0], ctx[0], c[0], c_ctx, w_ada[0], b_ada[0], w_in[0], conv_qkv[0],
                 a_log[0].reshape(-1), dt_bias[0].reshape(-1), gdn_norm_w[0], conv_b[0],
                 w_branch_a[0], w_branch_b[0], w_out[0], ln1_g[0], ln1_b[0],
                 w_router_group[0], b_router_group[0], w_router_expert[0], b_router_expert[0],
                 w_exp_gate[0], w_exp_up[0], w_exp_down[0], ln2_g[0], ln2_b[0])
    return out[None]
```

```python
import functools

import numpy as np
import jax
import jax.numpy as jnp
from jax import lax
from jax.experimental import pallas as pl
from jax.experimental.pallas import tpu as pltpu

F32 = jnp.float32
BF16 = jnp.bfloat16
HIGHEST = lax.Precision.HIGHEST

D_MODEL = 4096
GRID_W = 64
CHUNK = 64
HEADS = 16
HEAD_DIM = 128
GDN_WIDTH = HEADS * HEAD_DIM
CONV_WIDTH = D_MODEL // 2
N_GROUPS = 8
EXPERTS_PER_GROUP = 8
N_EXPERTS = 64
EXPERT_HIDDEN = 768
N_MOD = 6
DEEPNORM_ALPHA = 2.0 ** 0.25
LN_EPS = 1e-6
RMS_EPS = 1e-6

LANES = 128
SUBLANES = 8
UNITS = 128
VMEM_LIMIT = 52 * 1024 * 1024


def _cparams(sem, vmem=VMEM_LIMIT):
    return pltpu.CompilerParams(dimension_semantics=sem, vmem_limit_bytes=vmem)


def _silu(x):
    return x * jax.nn.sigmoid(x)


def _layer_norm(x):
    mu = jnp.mean(x, axis=-1, keepdims=True)
    xc = x - mu
    var = jnp.mean(xc * xc, axis=-1, keepdims=True)
    return xc * lax.rsqrt(var + LN_EPS)


def _mod_kernel(c_ref, w_ref, b_ref, o_ref):
    s = _silu(c_ref[...])
    o_ref[...] = jnp.dot(s, w_ref[...], preferred_element_type=F32, precision=HIGHEST) + b_ref[...]


def _mod_call(cs, w_ada, b_ada):
    n = w_ada.shape[1]
    tn = 512
    return pl.pallas_call(
        _mod_kernel,
        out_shape=jax.ShapeDtypeStruct((SUBLANES, n), F32),
        grid=(n // tn,),
        in_specs=[pl.BlockSpec((SUBLANES, D_MODEL), lambda j: (0, 0)),
                  pl.BlockSpec((D_MODEL, tn), lambda j: (0, j)),
                  pl.BlockSpec((1, tn), lambda j: (0, j))],
        out_specs=pl.BlockSpec((SUBLANES, tn), lambda j: (0, j)),
        compiler_params=_cparams(("arbitrary",)),
        name="mod",
    )(cs, w_ada, b_ada)


def _ln_mod_kernel(x_ref, shift_ref, scale_ref, o_ref):
    y = _layer_norm(x_ref[...])
    o_ref[...] = (y * (1.0 + scale_ref[...]) + shift_ref[...]).astype(o_ref.dtype)


def _ln_mod_call(x, shift, scale):
    t = x.shape[0]
    tr = min(256, t)
    return pl.pallas_call(
        _ln_mod_kernel,
        out_shape=jax.ShapeDtypeStruct((t, D_MODEL), BF16),
        grid=(t // tr,),
        in_specs=[pl.BlockSpec((tr, D_MODEL), lambda i: (i, 0)),
                  pl.BlockSpec((1, D_MODEL), lambda i: (0, 0)),
                  pl.BlockSpec((1, D_MODEL), lambda i: (0, 0))],
        out_specs=pl.BlockSpec((tr, D_MODEL), lambda i: (i, 0)),
        compiler_params=_cparams(("arbitrary",)),
        name="ln_mod",
    )(x, shift, scale)


COL_QKV = 0
COL_Z = 3 * GDN_WIDTH
COL_XB = 0
COL_BG = CONV_WIDTH
COL_CG = 2 * CONV_WIDTH
COL_GATES = 3 * CONV_WIDTH
COL_BA = 0


def _conv3_rows(x, taps_ref, group):
    rows = x.shape[0]
    pos = lax.broadcasted_iota(jnp.int32, x.shape, 0) % group
    prev = jnp.where(pos == 0, 0.0, pltpu.roll(x, 1, 0))
    nxt = jnp.where(pos == group - 1, 0.0, pltpu.roll(x, rows - 1, 0))
    return taps_ref[0:1, :] * prev + taps_ref[1:2, :] * x + taps_ref[2:3, :] * nxt


def _ip_qkv_kernel(group, x_ref, w_ref, taps_ref, o_ref):
    acc = jnp.dot(x_ref[...], w_ref[...], preferred_element_type=F32)
    y = _silu(_conv3_rows(acc, taps_ref, group))
    j = pl.program_id(0)
    tiles_per_part = GDN_WIDTH // acc.shape[1]
    for hh in range(acc.shape[1] // HEAD_DIM):
        ys = y[:, hh * HEAD_DIM:(hh + 1) * HEAD_DIM]
        r = lax.rsqrt(jnp.sum(ys * ys, axis=-1, keepdims=True) + 1e-6)
        scale = jnp.where(j < tiles_per_part, r * HEAD_DIM ** -0.5,
                          jnp.where(j < 2 * tiles_per_part, r, 1.0))
        o_ref[hh] = (ys * scale).astype(o_ref.dtype)


def _ip_z_kernel(x_ref, w_ref, nw_ref, o_ref):
    acc = jnp.dot(x_ref[...], w_ref[...], preferred_element_type=F32)
    for hh in range(acc.shape[1] // HEAD_DIM):
        zs = acc[:, hh * HEAD_DIM:(hh + 1) * HEAD_DIM]
        o_ref[hh] = _silu(zs) * nw_ref[...]


def _ip_xbc_kernel(group, x_ref, wxb_ref, wbg_ref, wcg_ref, taps_ref, o_ref):
    x = x_ref[...]
    xb = jnp.dot(x, wxb_ref[...], preferred_element_type=F32)
    bg = jnp.dot(x, wbg_ref[...], preferred_element_type=F32)
    cg = jnp.dot(x, wcg_ref[...], preferred_element_type=F32)
    o_ref[...] = (bg * _conv3_rows(cg * xb, taps_ref, group)).astype(o_ref.dtype)


def _ip_gates_kernel(x_ref, w_ref, o_ref):
    acc = jnp.dot(x_ref[...], w_ref[...], preferred_element_type=F32)
    o_ref[...] = jax.nn.sigmoid(acc).astype(o_ref.dtype)


def _ip_ba_kernel(x_ref, w_ref, alog_ref, dtb_ref, o_ref):
    acc = jnp.dot(x_ref[...], w_ref[...], preferred_element_type=F32)
    lane = lax.broadcasted_iota(jnp.int32, acc.shape, 1)
    a = acc + dtb_ref[...]
    softplus = jnp.maximum(a, 0.0) + jnp.log(1.0 + jnp.exp(-jnp.abs(a)))
    o_ref[...] = jnp.where(lane < 2 * HEADS, jax.nn.sigmoid(acc), -jnp.exp(alog_ref[...]) * softplus)


def _inproj(kernel, h, w_cat, col_starts, tn, n_tiles, extra, extra_specs, out_shape, out_spec, tm):
    t = h.shape[0]
    w_specs = [pl.BlockSpec((D_MODEL, tn), functools.partial(lambda j, i, o: (0, o + j), o=c // tn))
               for c in col_starts]
    return pl.pallas_call(
        kernel,
        out_shape=out_shape,
        grid=(n_tiles, t // tm),
        in_specs=[pl.BlockSpec((tm, D_MODEL), lambda j, i: (i, 0))] + w_specs + extra_specs,
        out_specs=out_spec,
        compiler_params=_cparams(("arbitrary", "arbitrary")),
        name="inproj",
    )(h, *([w_cat] * len(col_starts)), *extra)


def _inproj_qkv(h, w_cat, taps, group, tm):
    t = h.shape[0]
    tn = 1024
    hpt = tn // HEAD_DIM
    return _inproj(
        functools.partial(_ip_qkv_kernel, group), h, w_cat, [COL_QKV], tn, 3 * GDN_WIDTH // tn,
        [taps], [pl.BlockSpec((3, tn), lambda j, i: (0, j))],
        jax.ShapeDtypeStruct((3 * HEADS, t, HEAD_DIM), BF16),
        pl.BlockSpec((hpt, tm, HEAD_DIM), lambda j, i: (j, i, 0)), tm)


def _inproj_z(h, w_cat, norm_w, tm):
    t = h.shape[0]
    tn = 1024
    hpt = tn // HEAD_DIM
    return _inproj(
        _ip_z_kernel, h, w_cat, [COL_Z], tn, GDN_WIDTH // tn,
        [norm_w], [pl.BlockSpec((1, HEAD_DIM), lambda j, i: (0, 0))],
        jax.ShapeDtypeStruct((HEADS, t, HEAD_DIM), F32),
        pl.BlockSpec((hpt, tm, HEAD_DIM), lambda j, i: (j, i, 0)), tm)


def _inproj_xbc(h, w_cat, taps, group, tm):
    t = h.shape[0]
    tn = 512
    return _inproj(
        functools.partial(_ip_xbc_kernel, group), h, w_cat, [COL_XB, COL_BG, COL_CG], tn,
        CONV_WIDTH // tn, [taps], [pl.BlockSpec((3, tn), lambda j, i: (0, j))],
        jax.ShapeDtypeStruct((t, CONV_WIDTH), BF16),
        pl.BlockSpec((tm, tn), lambda j, i: (i, j)), tm)


def _inproj_gates(h, w_cat, tm):
    t = h.shape[0]
    tn = 1024
    return _inproj(
        _ip_gates_kernel, h, w_cat, [COL_GATES], tn, 2 * D_MODEL // tn, [], [],
        jax.ShapeDtypeStruct((t, 2 * D_MODEL), BF16),
        pl.BlockSpec((tm, tn), lambda j, i: (i, j)), tm)


def _inproj_ba(h, w_cat, alog_row, dtb_row, tm):
    t = h.shape[0]
    return _inproj(
        _ip_ba_kernel, h, w_cat, [COL_BA], LANES, 1, [alog_row, dtb_row],
        [pl.BlockSpec((1, LANES), lambda j, i: (0, 0))] * 2,
        jax.ShapeDtypeStruct((t, LANES), F32),
        pl.BlockSpec((tm, LANES), lambda j, i: (i, 0)), tm)


def _gdn_scal_kernel(g_ref, gc_ref, ep_ref, egl_ref):
    d = pl.program_id(0)
    r = lax.broadcasted_iota(jnp.int32, (CHUNK, CHUNK), 0)
    c = lax.broadcasted_iota(jnp.int32, (CHUNK, CHUNK), 1)
    tri = jnp.where(d == 0, (c <= r).astype(F32), (c >= r).astype(F32))
    gc = jnp.dot(tri, g_ref[0], preferred_element_type=F32, precision=HIGHEST)
    gl = jnp.where(d == 0, gc[CHUNK - 1:CHUNK, :], gc[0:1, :])
    gc_ref[0] = gc
    ep_ref[0] = jnp.exp(gl - gc)
    egl_ref[0] = jnp.exp(gl)


def _gdn_scal_call(g_s):
    u = g_s.shape[2]
    tu = 512 if u % 512 == 0 else UNITS
    spec = pl.BlockSpec((1, CHUNK, tu), lambda d, i: (d, 0, i))
    return pl.pallas_call(
        _gdn_scal_kernel,
        out_shape=(jax.ShapeDtypeStruct(g_s.shape, F32), jax.ShapeDtypeStruct(g_s.shape, F32),
                   jax.ShapeDtypeStruct((2, 1, u), F32)),
        grid=(2, u // tu),
        in_specs=[spec],
        out_specs=(spec, spec, pl.BlockSpec((1, 1, tu), lambda d, i: (d, 0, i))),
        compiler_params=_cparams(("arbitrary", "arbitrary")),
        name="gdn_scal",
    )(g_s)


def _nt_dot(a, b):
    return lax.dot_general(a, b, (((1,), (1,)), ((), ())), preferred_element_type=F32)


def _gdn_gram_kernel(cb, q_ref, k_ref, g_ref, qk_ref, kt_ref):
    r = lax.broadcasted_iota(jnp.int32, (HEAD_DIM, HEAD_DIM), 0)
    c = lax.broadcasted_iota(jnp.int32, (HEAD_DIM, HEAD_DIM), 1)
    eye = (r == c).astype(BF16)
    for ci in range(cb):
        k = k_ref[0, ci * CHUNK:(ci + 1) * CHUNK, :]
        q = q_ref[0, ci * CHUNK:(ci + 1) * CHUNK, :]
        g_ref[ci] = _nt_dot(k, k)
        qk_ref[ci] = _nt_dot(q, k)
        kt_ref[ci] = _nt_dot(eye, k).astype(BF16)


def _gdn_gram_call(qkv, nc):
    cb = min(8, nc)
    ncb = nc // cb
    u = HEADS * nc
    return pl.pallas_call(
        functools.partial(_gdn_gram_kernel, cb),
        out_shape=(jax.ShapeDtypeStruct((u, CHUNK, CHUNK), F32),
                   jax.ShapeDtypeStruct((u, CHUNK, CHUNK), F32),
                   jax.ShapeDtypeStruct((u, HEAD_DIM, CHUNK), BF16)),
        grid=(HEADS, ncb),
        in_specs=[pl.BlockSpec((1, cb * CHUNK, HEAD_DIM), lambda h, c: (h, c, 0)),
                  pl.BlockSpec((1, cb * CHUNK, HEAD_DIM), lambda h, c: (HEADS + h, c, 0))],
        out_specs=(pl.BlockSpec((cb, CHUNK, CHUNK), lambda h, c: (h * ncb + c, 0, 0)),
                   pl.BlockSpec((cb, CHUNK, CHUNK), lambda h, c: (h * ncb + c, 0, 0)),
                   pl.BlockSpec((cb, HEAD_DIM, CHUNK), lambda h, c: (h * ncb + c, 0, 0))),
        compiler_params=_cparams(("arbitrary", "arbitrary")),
        name="gdn_gram",
    )(qkv, qkv)


NB = CHUNK // SUBLANES


def _row_bcast(ref, row):
    return jnp.broadcast_to(ref[pl.ds(row, 1), :], (SUBLANES, UNITS))


def _gdn_inv_kernel(g_ref, qk_ref, beta_ref, gc_ref, twtu_ref, atde_ref, *scratch):
    for bwd in (False, True):
        @pl.when(pl.program_id(0) == int(bwd))
        def _(bwd=bwd):
            _gdn_inv_body(bwd, g_ref, qk_ref, beta_ref, gc_ref, twtu_ref.at[0], atde_ref.at[0], *scratch)


def _gdn_inv_body(bwd, g_ref, qk_ref, beta_ref, gc_ref, twtu_ref, atde_ref,
                  g_s, qk_s, l_s, t_s, tw_s, tu_s, at_s, de_s, e_s):
    pos = (lambda a: CHUNK - 1 - a) if bwd else (lambda a: a)
    blk = (lambda b: NB - 1 - b) if bwd else (lambda b: b)

    for pp in range(CHUNK // 2):
        sl = slice(pp * LANES, (pp + 1) * LANES)
        g_s[sl, :] = g_ref[:, sl].T
        qk_s[sl, :] = qk_ref[:, sl].T
    e_s[...] = jnp.exp(gc_ref[0])

    @pl.when(pl.program_id(1) == 0)
    def _():
        for ref in (tw_s, tu_s, at_s, de_s):
            ref[...] = jnp.zeros(ref.shape, F32)

    sub = lax.broadcasted_iota(jnp.int32, (SUBLANES, UNITS), 0)
    zero = jnp.zeros((SUBLANES, UNITS), F32)
    rs = range(SUBLANES)
    own = [(SUBLANES - 1 - r) if bwd else r for r in rs]
    earlier = [(sub > own[r]) if bwd else (sub < own[r]) for r in rs]

    def tile_ds(p, b):
        return pl.ds(pl.multiple_of(p * CHUNK + b * SUBLANES, SUBLANES), SUBLANES)

    def cols_ds(b):
        return pl.ds(pl.multiple_of(b * SUBLANES, SUBLANES), SUBLANES)

    def row_block(ib, carry):
        b_own = blk(ib)
        ps = [pos(ib * SUBLANES + r) for r in rs]
        gc_p = [_row_bcast(gc_ref.at[0], p) for p in ps]
        beta_p = [_row_bcast(beta_ref.at[0], p) for p in ps]

        def weights_offdiag(bc, c):
            b = blk(bc)
            gc_c = gc_ref[0, cols_ds(b), :]
            for r in rs:
                dec = jnp.exp(gc_p[r] - gc_c)
                l_s[tile_ds(ps[r], b), :] = beta_p[r] * g_s[tile_ds(ps[r], b), :] * dec
                at_s[tile_ds(ps[r], b), :] = qk_s[tile_ds(ps[r], b), :] * dec
            return c

        lax.fori_loop(0, ib, weights_offdiag, 0)
        gc_c = gc_ref[0, cols_ds(b_own), :]
        for r in rs:
            t = tile_ds(ps[r], b_own)
            dec = jnp.exp(gc_p[r] - gc_c)
            l_s[t, :] = jnp.where(earlier[r], beta_p[r] * g_s[t, :] * dec, 0.0)
            at_s[t, :] = jnp.where(sub == own[r], qk_s[t, :], jnp.where(earlier[r], qk_s[t, :] * dec, 0.0))
            de_s[t, :] = jnp.where(sub == own[r], _row_bcast(e_s, ps[r]), 0.0)

        def finish(b, acc):
            done = []
            for r in rs:
                a_r = acc[r]
                for kk in range(r):
                    a_r = a_r - _row_bcast(l_s, ps[r] * CHUNK + ps[kk]) * done[kk]
                done.append(a_r)
            beta_c = beta_ref[0, cols_ds(b), :]
            be_c = beta_c * e_s[cols_ds(b), :]
            for r in rs:
                t_s[tile_ds(ps[r], b), :] = done[r]
                tu_s[tile_ds(ps[r], b), :] = done[r] * beta_c
                tw_s[tile_ds(ps[r], b), :] = done[r] * be_c

        def subst_offdiag(bc, c):
            b = blk(bc)

            def k_block(kb, acc):
                acc = list(acc)
                for kk in rs:
                    pk = pos(kb * SUBLANES + kk)
                    t_k = t_s[tile_ds(pk, b), :]
                    for r in rs:
                        acc[r] = acc[r] - _row_bcast(l_s, ps[r] * CHUNK + pk) * t_k
                return tuple(acc)

            finish(b, lax.fori_loop(bc, ib, k_block, (zero,) * SUBLANES))
            return c

        lax.fori_loop(0, ib, subst_offdiag, 0)
        finish(b_own, [jnp.where(sub == own[r], 1.0, 0.0) for r in rs])
        return carry

    lax.fori_loop(0, NB, row_block, 0)

    for p in range(CHUNK):
        rows = slice(p * CHUNK, (p + 1) * CHUNK)
        sl = slice(p * LANES, (p + 1) * LANES)
        twtu_ref[:, sl] = jnp.concatenate([tw_s[rows, :], tu_s[rows, :]], axis=0).T.astype(BF16)
        atde_ref[:, sl] = jnp.concatenate([at_s[rows, :], de_s[rows, :]], axis=0).T.astype(BF16)


def _gdn_inv_call(g_flat, qk_flat, beta_s, gc_s):
    u = g_flat.shape[0]
    mat = pl.BlockSpec((UNITS, CHUNK * CHUNK), lambda d, i: (i, 0))
    sc = pl.BlockSpec((1, CHUNK, UNITS), lambda d, i: (d, 0, i))
    out = pl.BlockSpec((1, UNITS, CHUNK * LANES), lambda d, i: (d, i, 0))
    soa = pltpu.VMEM((CHUNK * CHUNK, UNITS), F32)
    return pl.pallas_call(
        _gdn_inv_kernel,
        out_shape=(jax.ShapeDtypeStruct((2, u, CHUNK * LANES), BF16),
                   jax.ShapeDtypeStruct((2, u, CHUNK * LANES), BF16)),
        grid=(2, u // UNITS),
        in_specs=[mat, mat, sc, sc],
        out_specs=(out, out),
        scratch_shapes=[soa] * 8 + [pltpu.VMEM((CHUNK, UNITS), F32)],
        compiler_params=_cparams(("arbitrary", "arbitrary")),
        name="gdn_inv",
    )(g_flat, qk_flat, beta_s, gc_s)


def _gdn_scan_kernel(hb, cb, egl_ref, q_ref, k_ref, v_ref, kt_ref, twtu_ref, atde_ref, ep_ref, s0_ref,
                     o_ref, sfin_ref, s_scr, w_scr, u_scr):
    d = pl.program_id(0)
    hg = pl.program_id(1)
    ci = pl.program_id(2)
    ncb = pl.num_programs(2)

    @pl.when(ci == 0)
    def _():
        s_scr[...] = s0_ref[0]

    cblk = ci + d * (ncb - 1 - 2 * ci)
    zeros = jnp.zeros((CHUNK, HEAD_DIM), BF16)

    def prepare(c, carry):
        r0 = pl.multiple_of(c * CHUNK, CHUNK)
        for hh in range(hb):
            k = k_ref[hh, pl.ds(r0, CHUNK), :]
            v = v_ref[hh, pl.ds(r0, CHUNK), :]
            rhs = jnp.concatenate([jnp.concatenate([k, zeros], axis=1),
                                   jnp.concatenate([zeros, v], axis=1)], axis=0)
            wu = jnp.dot(twtu_ref[0, hh, c], rhs, preferred_element_type=F32)
            w_scr[c, hh] = wu[:, :HEAD_DIM].astype(BF16)
            u_scr[c, hh] = wu[:, HEAD_DIM:]
        return carry

    lax.fori_loop(0, cb, prepare, 0)

    def chunk_step(cc, carry):
        c = cc + d * (cb - 1 - 2 * cc)
        r0 = pl.multiple_of(c * CHUNK, CHUNK)
        heads = range(hb)
        s = [s_scr[hh] for hh in heads]
        x = [jnp.dot(jnp.concatenate([w_scr[c, hh], q_ref[hh, pl.ds(r0, CHUNK), :]], axis=0),
                     s[hh].astype(BF16), preferred_element_type=F32) for hh in heads]
        v_new = [u_scr[c, hh] - x[hh][:CHUNK] for hh in heads]
        for hh in heads:
            kdt = (kt_ref[hh, c].astype(F32) * ep_ref[0, hh, pl.ds(c, 1), :]).astype(BF16)
            egl = egl_ref[d, hg * hb + hh, cblk * cb + c]
            s_scr[hh] = egl * s[hh] + jnp.dot(kdt, v_new[hh].astype(BF16), preferred_element_type=F32)
        for hh in heads:
            rhs = jnp.concatenate([v_new[hh], x[hh][CHUNK:]], axis=0).astype(BF16)
            o_ref[0, hh, pl.ds(r0, CHUNK), :] = jnp.dot(atde_ref[0, hh, c], rhs, preferred_element_type=F32)
        return carry

    lax.fori_loop(0, cb, chunk_step, 0)

    @pl.when(ci == ncb - 1)
    def _():
        sfin_ref[0] = s_scr[...]


def _gdn_scan_call(qkv, kt, twtu, atde, ep, egl, s0, nc):
    t = qkv.shape[1]
    hb = 8
    cb = min(8, nc)
    ncb = nc // cb
    hgs = HEADS // hb
    nat = lambda d, c: c + d * (ncb - 1 - 2 * c)
    tok = lambda part: pl.BlockSpec((hb, cb * CHUNK, HEAD_DIM),
                                    lambda d, hg, c, egl, part=part: (part * hgs + hg, nat(d, c), 0))
    per_dir = lambda last: pl.BlockSpec((1, hb, cb, CHUNK, last),
                                        lambda d, hg, c, egl: (d, hg, nat(d, c), 0, 0))
    grid_spec = pltpu.PrefetchScalarGridSpec(
        num_scalar_prefetch=1,
        grid=(2, hgs, ncb),
        in_specs=[tok(0), tok(1), tok(2),
                  pl.BlockSpec((hb, cb, HEAD_DIM, CHUNK), lambda d, hg, c, egl: (hg, nat(d, c), 0, 0)),
                  per_dir(LANES), per_dir(LANES),
                  pl.BlockSpec((1, hb, cb, CHUNK), lambda d, hg, c, egl: (d, hg, nat(d, c), 0)),
                  pl.BlockSpec((1, hb, HEAD_DIM, HEAD_DIM), lambda d, hg, c, egl: (d, hg, 0, 0))],
        out_specs=(pl.BlockSpec((1, hb, cb * CHUNK, HEAD_DIM), lambda d, hg, c, egl: (d, hg, nat(d, c), 0)),
                   pl.BlockSpec((1, hb, HEAD_DIM, HEAD_DIM), lambda d, hg, c, egl: (d, hg, 0, 0))),
        scratch_shapes=[pltpu.VMEM((hb, HEAD_DIM, HEAD_DIM), F32),
                        pltpu.VMEM((cb, hb, CHUNK, HEAD_DIM), BF16),
                        pltpu.VMEM((cb, hb, CHUNK, HEAD_DIM), F32)],
    )
    return pl.pallas_call(
        functools.partial(_gdn_scan_kernel, hb, cb),
        out_shape=(jax.ShapeDtypeStruct((2, HEADS, t, HEAD_DIM), F32),
                   jax.ShapeDtypeStruct((2, HEADS, HEAD_DIM, HEAD_DIM), F32)),
        grid_spec=grid_spec,
        compiler_params=_cparams(("arbitrary", "arbitrary", "arbitrary")),
        name="gdn_scan",
    )(egl, qkv, qkv, qkv, kt.reshape(HEADS, nc, HEAD_DIM, CHUNK),
      twtu.reshape(2, HEADS, nc, CHUNK, LANES), atde.reshape(2, HEADS, nc, CHUNK, LANES), ep, s0)


def _gdn(qkv, bg, s0):
    t = qkv.shape[1]
    nc = t // CHUNK
    u = HEADS * nc
    up = -(-u // UNITS) * UNITS
    to_soa = lambda a: jnp.pad(jnp.transpose(a.reshape(nc, CHUNK, 2, HEADS), (2, 1, 3, 0)).reshape(2, CHUNK, u),
                               ((0, 0), (0, 0), (0, up - u)))
    beta_s = to_soa(bg[:, :2 * HEADS])
    g_s = to_soa(bg[:, 2 * HEADS:4 * HEADS])
    gc_s, ep_s, egl_s = _gdn_scal_call(g_s)
    gram, qk, kt = _gdn_gram_call(qkv, nc)
    pad_u = lambda a: jnp.pad(a.reshape(u, -1), ((0, up - u), (0, 0)))
    g_flat, qk_flat = pad_u(gram), pad_u(qk)
    twtu, atde = (a[:, :u] for a in _gdn_inv_call(g_flat, qk_flat, beta_s, gc_s))
    ep = jnp.transpose(ep_s[:, :, :u].reshape(2, CHUNK, HEADS, nc), (0, 2, 3, 1))
    egl = egl_s[:, 0, :u].reshape(2, HEADS, nc)
    return _gdn_scan_call(qkv, kt, twtu, atde, ep, egl, s0, nc)


def _merge_kernel(o_ref, zs_ref, yc_ref, ga_ref, gb_ref, wa_ref, wb_ref, m_ref, yg_s):
    @pl.when(pl.program_id(1) == 0)
    def _():
        for hh in range(HEADS):
            o = o_ref[0, hh] + o_ref[1, hh]
            y = o * lax.rsqrt(jnp.mean(o * o, axis=-1, keepdims=True) + RMS_EPS) * zs_ref[hh]
            yg_s[:, hh * HEAD_DIM:(hh + 1) * HEAD_DIM] = y.astype(BF16)

    pa = jnp.dot(yg_s[...], wa_ref[...], preferred_element_type=F32)
    pb = jnp.dot(yc_ref[...], wb_ref[...], preferred_element_type=F32)
    m_ref[...] = (ga_ref[...].astype(F32) * pa + gb_ref[...].astype(F32) * pb).astype(m_ref.dtype)


def _merge_call(o, zs, yconv, gates, wa, wb):
    t = yconv.shape[0]
    tm, tn = 512, 512
    nj = D_MODEL // tn
    return pl.pallas_call(
        _merge_kernel,
        out_shape=jax.ShapeDtypeStruct((t, D_MODEL), BF16),
        grid=(t // tm, nj),
        in_specs=[pl.BlockSpec((2, HEADS, tm, HEAD_DIM), lambda i, j: (0, 0, i, 0)),
                  pl.BlockSpec((HEADS, tm, HEAD_DIM), lambda i, j: (0, i, 0)),
                  pl.BlockSpec((tm, CONV_WIDTH), lambda i, j: (i, 0)),
                  pl.BlockSpec((tm, tn), lambda i, j: (i, j)),
                  pl.BlockSpec((tm, tn), lambda i, j: (i, nj + j)),
                  pl.BlockSpec((GDN_WIDTH, tn), lambda i, j: (0, j)),
                  pl.BlockSpec((CONV_WIDTH, tn), lambda i, j: (0, j))],
        out_specs=pl.BlockSpec((tm, tn), lambda i, j: (i, j)),
        scratch_shapes=[pltpu.VMEM((tm, GDN_WIDTH), BF16)],
        compiler_params=_cparams(("arbitrary", "arbitrary")),
        name="merge",
    )(o, zs, yconv, gates, gates, wa, wb)


LN_ROWS = 16


def _out_kernel(m_ref, w_ref, x_ref, mod_ref, ln_ref, wrh_ref, wrl_ref, x1_ref, tok_ref, logit_ref, acc_s):
    kk = pl.program_id(1)
    part = jnp.dot(m_ref[...], w_ref[...], preferred_element_type=F32)

    @pl.when(kk == 0)
    def _():
        acc_s[...] = part

    @pl.when(kk > 0)
    def _():
        acc_s[...] += part

    @pl.when(kk == pl.num_programs(1) - 1)
    def _():
        def rows_step(ci, c):
            sl = pl.ds(pl.multiple_of(ci * LN_ROWS, LN_ROWS), LN_ROWS)
            x1 = _layer_norm(DEEPNORM_ALPHA * x_ref[sl, :] + mod_ref[0:1, :] * acc_s[sl, :])
            x1 = x1 * ln_ref[0:1, :] + ln_ref[1:2, :]
            x1_ref[sl, :] = x1
            tok_ref[sl, :] = _layer_norm(x1) * (1.0 + mod_ref[2:3, :]) + mod_ref[1:2, :]
            return c

        lax.fori_loop(0, acc_s.shape[0] // LN_ROWS, rows_step, 0)
        tok = tok_ref[...]
        t_hi = tok.astype(BF16)
        t_lo = (tok - t_hi.astype(F32)).astype(BF16)
        logit_ref[...] = (jnp.dot(t_hi, wrh_ref[...], preferred_element_type=F32)
                          + jnp.dot(t_lo, wrh_ref[...], preferred_element_type=F32)
                          + jnp.dot(t_hi, wrl_ref[...], preferred_element_type=F32))


def _out_call(m, w_out, x, mod3, ln1, w_router):
    wr_hi = w_router.astype(BF16)
    wr_lo = (w_router - wr_hi.astype(F32)).astype(BF16)
    t = m.shape[0]
    tm, tk = 256, 512
    return pl.pallas_call(
        _out_kernel,
        out_shape=(jax.ShapeDtypeStruct((t, D_MODEL), F32),
                   jax.ShapeDtypeStruct((t, D_MODEL), F32),
                   jax.ShapeDtypeStruct((t, LANES), F32)),
        grid=(t // tm, D_MODEL // tk),
        in_specs=[pl.BlockSpec((tm, tk), lambda i, k: (i, k)),
                  pl.BlockSpec((tk, D_MODEL), lambda i, k: (k, 0)),
                  pl.BlockSpec((tm, D_MODEL), lambda i, k: (i, 0)),
                  pl.BlockSpec((SUBLANES, D_MODEL), lambda i, k: (0, 0)),
                  pl.BlockSpec((SUBLANES, D_MODEL), lambda i, k: (0, 0)),
                  pl.BlockSpec((D_MODEL, LANES), lambda i, k: (0, 0)),
                  pl.BlockSpec((D_MODEL, LANES), lambda i, k: (0, 0))],
        out_specs=(pl.BlockSpec((tm, D_MODEL), lambda i, k: (i, 0)),
                   pl.BlockSpec((tm, D_MODEL), lambda i, k: (i, 0)),
                   pl.BlockSpec((tm, LANES), lambda i, k: (i, 0))),
        scratch_shapes=[pltpu.VMEM((tm, D_MODEL), F32)],
        compiler_params=_cparams(("arbitrary", "arbitrary")),
        name="out_proj",
    )(m, w_out, x, mod3, ln1, wr_hi, wr_lo)


MOE_BM = 256
MOE_HC = 256


def _gather_kernel(rows, nu_ref, idx_ref, nxt_ref, src_ref, dst_ref, buf, sem):
    i = pl.program_id(0)
    nu = nu_ref[0]
    slot = i % 2

    def copy(iref, s, r):
        return pltpu.make_async_copy(src_ref.at[pl.ds(iref[0, 0, r], 1), :],
                                     buf.at[s, pl.ds(r, 1), :], sem.at[s])

    def issue(iref, s):
        def body(r, c):
            copy(iref, s, r).start()
            return c
        lax.fori_loop(0, rows, body, 0, unroll=8)

    @pl.when(i == 0)
    def _():
        issue(idx_ref, 0)

    @pl.when(i + 1 < nu)
    def _():
        issue(nxt_ref, 1 - slot)

    @pl.when(i < nu)
    def _():
        def body(r, c):
            copy(idx_ref, slot, r).wait()
            return c
        lax.fori_loop(0, rows, body, 0, unroll=8)
        dst_ref[...] = buf[slot].astype(dst_ref.dtype)

    @pl.when(i >= nu)
    def _():
        dst_ref[...] = jnp.zeros(dst_ref.shape, dst_ref.dtype)


def _gather_call(src, idx, n_used):
    n = idx.shape[0]
    rows = MOE_BM
    nblk = n // rows
    idx3 = idx.reshape(nblk, 1, rows)
    grid_spec = pltpu.PrefetchScalarGridSpec(
        num_scalar_prefetch=1,
        grid=(nblk,),
        in_specs=[pl.BlockSpec((1, 1, rows), lambda i, nu: (i, 0, 0), memory_space=pltpu.SMEM),
                  pl.BlockSpec((1, 1, rows), lambda i, nu: (jnp.minimum(i + 1, nblk - 1), 0, 0),
                               memory_space=pltpu.SMEM),
                  pl.BlockSpec(memory_space=pl.ANY)],
        out_specs=pl.BlockSpec((rows, src.shape[1]), lambda i, nu: (i, 0)),
        scratch_shapes=[pltpu.VMEM((2, rows, src.shape[1]), src.dtype), pltpu.SemaphoreType.DMA((2,))],
    )
    return pl.pallas_call(
        functools.partial(_gather_kernel, rows),
        out_shape=jax.ShapeDtypeStruct((n, src.shape[1]), BF16),
        grid_spec=grid_spec,
        compiler_params=_cparams(("arbitrary",)),
        name="moe_gather",
    )(n_used, idx3, idx3, src)


def _expert_changed(be_ref, nu_ref, b):
    bb = jnp.minimum(b, nu_ref[0] - 1)
    return (b == 0) | (be_ref[bb] != be_ref[jnp.maximum(bb - 1, 0)])


def _moe_hidden_kernel(be_ref, nu_ref, x_ref, wg_ref, wu_ref, h_ref, wg_s, wu_s):
    b = pl.program_id(1)
    live = b < nu_ref[0]

    @pl.when(live & _expert_changed(be_ref, nu_ref, b))
    def _():
        wg_s[...] = wg_ref[0].astype(BF16)
        wu_s[...] = wu_ref[0].astype(BF16)

    @pl.when(live)
    def _():
        x = x_ref[...]
        hg = jnp.dot(x, wg_s[...], preferred_element_type=F32)
        hu = jnp.dot(x, wu_s[...], preferred_element_type=F32)
        h_ref[...] = (_silu(hg) * hu).astype(h_ref.dtype)

    @pl.when(jnp.logical_not(live))
    def _():
        h_ref[...] = jnp.zeros(h_ref.shape, h_ref.dtype)


def _moe_down_kernel(be_ref, nu_ref, h_ref, wd_ref, rw_ref, y_ref, wd_s):
    b = pl.program_id(0)
    live = b < nu_ref[0]

    @pl.when(live & _expert_changed(be_ref, nu_ref, b))
    def _():
        wd_s[...] = wd_ref[0].astype(BF16)

    @pl.when(live)
    def _():
        y_ref[...] = jnp.dot(h_ref[...], wd_s[...], preferred_element_type=F32) * rw_ref[...]

    @pl.when(jnp.logical_not(live))
    def _():
        y_ref[...] = jnp.zeros(y_ref.shape, F32)


def _expert_call(xs, wg, wu, wd, row_w, block_e, n_used):
    n = xs.shape[0]
    nb = n // MOE_BM
    nh = EXPERT_HIDDEN // MOE_HC
    blk = lambda b, nu: jnp.minimum(b, nu[0] - 1)
    hid = pl.pallas_call(
        _moe_hidden_kernel,
        out_shape=jax.ShapeDtypeStruct((n, EXPERT_HIDDEN), BF16),
        grid_spec=pltpu.PrefetchScalarGridSpec(
            num_scalar_prefetch=2,
            grid=(nh, nb),
            in_specs=[pl.BlockSpec((MOE_BM, D_MODEL), lambda h, b, be, nu: (blk(b, nu), 0)),
                      pl.BlockSpec((1, D_MODEL, MOE_HC), lambda h, b, be, nu: (be[blk(b, nu)], 0, h)),
                      pl.BlockSpec((1, D_MODEL, MOE_HC), lambda h, b, be, nu: (be[blk(b, nu)], 0, h))],
            out_specs=pl.BlockSpec((MOE_BM, MOE_HC), lambda h, b, be, nu: (b, h)),
            scratch_shapes=[pltpu.VMEM((D_MODEL, MOE_HC), BF16)] * 2),
        compiler_params=_cparams(("arbitrary", "arbitrary")),
        name="moe_hidden",
    )(block_e, n_used, xs, wg, wu)
    return pl.pallas_call(
        _moe_down_kernel,
        out_shape=jax.ShapeDtypeStruct((n, D_MODEL), F32),
        grid_spec=pltpu.PrefetchScalarGridSpec(
            num_scalar_prefetch=2,
            grid=(nb,),
            in_specs=[pl.BlockSpec((MOE_BM, EXPERT_HIDDEN), lambda b, be, nu: (blk(b, nu), 0)),
                      pl.BlockSpec((1, EXPERT_HIDDEN, D_MODEL), lambda b, be, nu: (be[blk(b, nu)], 0, 0)),
                      pl.BlockSpec((MOE_BM, 1), lambda b, be, nu: (blk(b, nu), 0))],
            out_specs=pl.BlockSpec((MOE_BM, D_MODEL), lambda b, be, nu: (b, 0)),
            scratch_shapes=[pltpu.VMEM((EXPERT_HIDDEN, D_MODEL), BF16)]),
        compiler_params=_cparams(("arbitrary",)),
        name="moe_down",
    )(block_e, n_used, hid, wd, row_w)


def _combine_kernel(rows, d_ref, nxt_ref, y_ref, x1_ref, mod_ref, ln_ref, o_ref, buf, sem):
    i = pl.program_id(0)
    slot = i % 2

    def copy(iref, s, k, r):
        return pltpu.make_async_copy(y_ref.at[pl.ds(iref[0, k, r], 1), :],
                                     buf.at[s, k, pl.ds(r, 1), :], sem.at[s])

    def issue(iref, s):
        def body(r, c):
            copy(iref, s, 0, r).start()
            copy(iref, s, 1, r).start()
            return c
        lax.fori_loop(0, rows, body, 0, unroll=8)

    @pl.when(i == 0)
    def _():
        issue(d_ref, 0)

    @pl.when(i + 1 < pl.num_programs(0))
    def _():
        issue(nxt_ref, 1 - slot)

    def wait_body(r, c):
        copy(d_ref, slot, 0, r).wait()
        copy(d_ref, slot, 1, r).wait()
        return c

    lax.fori_loop(0, rows, wait_body, 0, unroll=8)

    def rows_step(ci, c):
        sl = pl.ds(pl.multiple_of(ci * LN_ROWS, LN_ROWS), LN_ROWS)
        f = buf[slot, 0, sl, :] + buf[slot, 1, sl, :]
        y = _layer_norm(DEEPNORM_ALPHA * x1_ref[sl, :] + mod_ref[0:1, :] * f)
        o_ref[sl, :] = y * ln_ref[0:1, :] + ln_ref[1:2, :]
        return c

    lax.fori_loop(0, rows // LN_ROWS, rows_step, 0)


def _combine_call(ys, dest, x1, mod_row, ln2):
    t = x1.shape[0]
    rows = 128
    nblk = t // rows
    dest3 = jnp.transpose(dest.reshape(nblk, rows, 2), (0, 2, 1))
    return pl.pallas_call(
        functools.partial(_combine_kernel, rows),
        out_shape=jax.ShapeDtypeStruct((t, D_MODEL), F32),
        grid=(nblk,),
        in_specs=[pl.BlockSpec((1, 2, rows), lambda i: (i, 0, 0), memory_space=pltpu.SMEM),
                  pl.BlockSpec((1, 2, rows), lambda i: (jnp.minimum(i + 1, nblk - 1), 0, 0),
                               memory_space=pltpu.SMEM),
                  pl.BlockSpec(memory_space=pl.ANY),
                  pl.BlockSpec((rows, D_MODEL), lambda i: (i, 0)),
                  pl.BlockSpec((SUBLANES, D_MODEL), lambda i: (0, 0)),
                  pl.BlockSpec((SUBLANES, D_MODEL), lambda i: (0, 0))],
        out_specs=pl.BlockSpec((rows, D_MODEL), lambda i: (i, 0)),
        scratch_shapes=[pltpu.VMEM((2, 2, rows, D_MODEL), F32), pltpu.SemaphoreType.DMA((2,))],
        compiler_params=_cparams(("arbitrary",)),
        name="moe_combine",
    )(dest3, dest3, ys, x1, mod_row, ln2)


def _route(logits, b_group, b_expert):
    t = logits.shape[0]
    p_group = jax.nn.softmax(logits[:, :N_GROUPS] + b_group, axis=-1)
    group = jnp.argmax(p_group, axis=-1)
    gate_group = jnp.take_along_axis(p_group, group[:, None], axis=-1)
    le = (logits[:, N_GROUPS:N_GROUPS + N_EXPERTS] + b_expert).reshape(t, N_GROUPS, EXPERTS_PER_GROUP)
    le = jnp.take_along_axis(le, group[:, None, None], axis=1)[:, 0]
    top_p, top_i = lax.top_k(jax.nn.softmax(le, axis=-1), 2)
    weights = gate_group * top_p / jnp.sum(top_p, axis=-1, keepdims=True)
    expert_id = group[:, None] * EXPERTS_PER_GROUP + top_i
    return expert_id.astype(jnp.int32), weights


def _moe(tok, logits, b_group, b_expert, wg, wu, wd, x1, mod_row, ln2):
    t = tok.shape[0]
    expert_id, weights = _route(logits, b_group, b_expert)
    e_flat = expert_id.reshape(-1)
    n_assign = e_flat.shape[0]
    onehot = (e_flat[:, None] == jnp.arange(N_EXPERTS)[None, :]).astype(jnp.int32)
    rank = jnp.take_along_axis(jnp.cumsum(onehot, axis=0), e_flat[:, None], axis=1)[:, 0] - 1
    counts = jnp.sum(onehot, axis=0)
    padded = (counts + MOE_BM - 1) // MOE_BM * MOE_BM
    pad_end = jnp.cumsum(padded)
    dest = (pad_end - padded)[e_flat] + rank
    n_blocks = -(-n_assign // MOE_BM) + N_EXPERTS
    n_rows = n_blocks * MOE_BM
    src_tok = jnp.zeros((n_rows,), jnp.int32).at[dest].set(jnp.arange(n_assign, dtype=jnp.int32) // 2)
    row_w = jnp.zeros((n_rows, 1), F32).at[dest, 0].set(weights.reshape(-1))
    block_e = jnp.minimum(jnp.searchsorted(pad_end, jnp.arange(n_blocks) * MOE_BM, side="right"),
                          N_EXPERTS - 1).astype(jnp.int32)
    n_used = (pad_end[-1] // MOE_BM).astype(jnp.int32).reshape(1)
    xs = _gather_call(tok, src_tok, n_used)
    ys = _expert_call(xs, wg, wu, wd, row_w, block_e, n_used)
    return _combine_call(ys, dest.reshape(t, 2).astype(jnp.int32), x1, mod_row, ln2)


def _pad_rows(v, rows=SUBLANES):
    return jnp.pad(v, ((0, rows - v.shape[0]), (0, 0)))


def _layer(x, ctx, c, c_ctx, w_ada, b_ada, w_in, conv_qkv, a_log, dt_bias, gdn_norm_w, conv_b,
           w_branch_a, w_branch_b, w_out, ln1_g, ln1_b, w_router_group, b_router_group,
           w_router_expert, b_router_expert, w_exp_gate, w_exp_up, w_exp_down, ln2_g, ln2_b):
    t = x.shape[0]
    mod = _mod_call(_pad_rows(jnp.stack([c, c_ctx])), w_ada, b_ada.reshape(1, -1))
    mod_lat = mod[0].reshape(N_MOD, D_MODEL)
    mod_ctx = mod[1].reshape(N_MOD, D_MODEL)

    n_a = 4 * GDN_WIDTH
    n_ba = 4 * HEADS
    w_pa = w_in[:, :n_a].astype(BF16)
    w_pb = w_in[:, n_a + n_ba:].astype(BF16)
    w_pc = jnp.pad(w_in[:, n_a:n_a + n_ba], ((0, 0), (0, LANES - n_ba))).astype(BF16)
    lane_row = lambda v: jnp.pad(v.reshape(1, -1), ((0, 0), (2 * HEADS, LANES - 4 * HEADS)))
    alog_row, dtb_row = lane_row(a_log), lane_row(dt_bias)

    h_ctx = _ln_mod_call(ctx, mod_ctx[0:1], mod_ctx[1:2])
    tc = ctx.shape[0]
    qkv_c = _inproj_qkv(h_ctx, w_pa, conv_qkv, tc, tc)
    bg_c = _inproj_ba(h_ctx, w_pc, alog_row, dtb_row, tc)
    s0 = jnp.zeros((2, HEADS, HEAD_DIM, HEAD_DIM), F32)
    _, s_ctx = _gdn(qkv_c, bg_c, s0)

    tm = 512
    h = _ln_mod_call(x, mod_lat[0:1], mod_lat[1:2])
    qkv = _inproj_qkv(h, w_pa, conv_qkv, GRID_W, tm)
    zs = _inproj_z(h, w_pa, gdn_norm_w.reshape(1, HEAD_DIM), tm)
    yconv = _inproj_xbc(h, w_pb, conv_b, GRID_W, tm)
    gates = _inproj_gates(h, w_pb, tm)
    bg = _inproj_ba(h, w_pc, alog_row, dtb_row, tm)
    o, _ = _gdn(qkv, bg, s_ctx)
    m = _merge_call(o, zs, yconv, gates, w_branch_a.astype(BF16), w_branch_b.astype(BF16))

    w_router = jnp.pad(jnp.concatenate([w_router_group, w_router_expert], axis=1),
                       ((0, 0), (0, LANES - N_GROUPS - N_EXPERTS)))
    x1, tok, logits = _out_call(m, w_out.astype(BF16), x, _pad_rows(mod_lat[2:5]),
                                _pad_rows(jnp.stack([ln1_g, ln1_b])), w_router)

    return _moe(tok, logits, b_router_group, b_router_expert, w_exp_gate, w_exp_up, w_exp_down,
                x1, _pad_rows(mod_lat[5:6]), _pad_rows(jnp.stack([ln2_g, ln2_b])))


def kernel(x, c, ctx, c_ctx, w_ada, b_ada, w_in, conv_qkv, a_log, dt_bias, gdn_norm_w, conv_b,
           w_branch_a, w_branch_b, w_out, ln1_g, ln1_b, w_router_group, b_router_group,
           w_router_expert, b_router_expert, w_exp_gate, w_exp_up, w_exp_down, ln2_g, ln2_b):
    assert x.shape[0] == 1 and w_ada.shape[0] == 1, "single batch element, single layer"
    out = _layer(x[0], ctx[0], c[0], c_ctx, w_ada[0], b_ada[0], w_in[0], conv_qkv[0],
                 a_log[0].reshape(-1), dt_bias[0].reshape(-1), gdn_norm_w[0], conv_b[0],
                 w_branch_a[0], w_branch_b[0], w_out[0], ln1_g[0], ln1_b[0],
                 w_router_group[0], b_router_group[0], w_router_expert[0], b_router_expert[0],
                 w_exp_gate[0], w_exp_up[0], w_exp_down[0], ln2_g[0], ln2_b[0])
    return out[None]
```

```python
import functools

import numpy as np
import jax
import jax.numpy as jnp
from jax import lax
from jax.experimental import pallas as pl
from jax.experimental.pallas import tpu as pltpu

F32 = jnp.float32
BF16 = jnp.bfloat16
HIGHEST = lax.Precision.HIGHEST

D_MODEL = 4096
GRID_W = 64
CHUNK = 64
HEADS = 16
HEAD_DIM = 128
GDN_WIDTH = HEADS * HEAD_DIM
CONV_WIDTH = D_MODEL // 2
N_GROUPS = 8
EXPERTS_PER_GROUP = 8
N_EXPERTS = 64
EXPERT_HIDDEN = 768
N_MOD = 6
DEEPNORM_ALPHA = 2.0 ** 0.25
LN_EPS = 1e-6
RMS_EPS = 1e-6

LANES = 128
SUBLANES = 8
UNITS = 128
VMEM_LIMIT = 52 * 1024 * 1024


def _cparams(sem, vmem=VMEM_LIMIT):
    return pltpu.CompilerParams(dimension_semantics=sem, vmem_limit_bytes=vmem)


def _silu(x):
    return x * jax.nn.sigmoid(x)


def _layer_norm(x):
    mu = jnp.mean(x, axis=-1, keepdims=True)
    xc = x - mu
    var = jnp.mean(xc * xc, axis=-1, keepdims=True)
    return xc * lax.rsqrt(var + LN_EPS)


def _mod_kernel(c_ref, w_ref, b_ref, o_ref):
    s = _silu(c_ref[...])
    o_ref[...] = jnp.dot(s, w_ref[...], preferred_element_type=F32, precision=HIGHEST) + b_ref[...]


def _mod_call(cs, w_ada, b_ada):
    n = w_ada.shape[1]
    tn = 512
    return pl.pallas_call(
        _mod_kernel,
        out_shape=jax.ShapeDtypeStruct((SUBLANES, n), F32),
        grid=(n // tn,),
        in_specs=[pl.BlockSpec((SUBLANES, D_MODEL), lambda j: (0, 0)),
                  pl.BlockSpec((D_MODEL, tn), lambda j: (0, j)),
                  pl.BlockSpec((1, tn), lambda j: (0, j))],
        out_specs=pl.BlockSpec((SUBLANES, tn), lambda j: (0, j)),
        compiler_params=_cparams(("arbitrary",)),
        name="mod",
    )(cs, w_ada, b_ada)


def _ln_mod_kernel(x_ref, shift_ref, scale_ref, o_ref):
    y = _layer_norm(x_ref[...])
    o_ref[...] = (y * (1.0 + scale_ref[...]) + shift_ref[...]).astype(o_ref.dtype)


def _ln_mod_call(x, shift, scale):
    t = x.shape[0]
    tr = min(256, t)
    return pl.pallas_call(
        _ln_mod_kernel,
        out_shape=jax.ShapeDtypeStruct((t, D_MODEL), BF16),
        grid=(t // tr,),
        in_specs=[pl.BlockSpec((tr, D_MODEL), lambda i: (i, 0)),
                  pl.BlockSpec((1, D_MODEL), lambda i: (0, 0)),
                  pl.BlockSpec((1, D_MODEL), lambda i: (0, 0))],
        out_specs=pl.BlockSpec((tr, D_MODEL), lambda i: (i, 0)),
        compiler_params=_cparams(("arbitrary",)),
        name="ln_mod",
    )(x, shift, scale)


COL_QKV = 0
COL_Z = 3 * GDN_WIDTH
COL_XB = 0
COL_BG = CONV_WIDTH
COL_CG = 2 * CONV_WIDTH
COL_GATES = 3 * CONV_WIDTH
COL_BA = 0


def _conv3_rows(x, taps_ref, group):
    rows = x.shape[0]
    pos = lax.broadcasted_iota(jnp.int32, x.shape, 0) % group
    prev = jnp.where(pos == 0, 0.0, pltpu.roll(x, 1, 0))
    nxt = jnp.where(pos == group - 1, 0.0, pltpu.roll(x, rows - 1, 0))
    return taps_ref[0:1, :] * prev + taps_ref[1:2, :] * x + taps_ref[2:3, :] * nxt


def _ip_qkv_kernel(group, x_ref, w_ref, taps_ref, o_ref):
    acc = jnp.dot(x_ref[...], w_ref[...], preferred_element_type=F32)
    y = _silu(_conv3_rows(acc, taps_ref, group))
    j = pl.program_id(0)
    tiles_per_part = GDN_WIDTH // acc.shape[1]
    for hh in range(acc.shape[1] // HEAD_DIM):
        ys = y[:, hh * HEAD_DIM:(hh + 1) * HEAD_DIM]
        r = lax.rsqrt(jnp.sum(ys * ys, axis=-1, keepdims=True) + 1e-6)
        scale = jnp.where(j < tiles_per_part, r * HEAD_DIM ** -0.5,
                          jnp.where(j < 2 * tiles_per_part, r, 1.0))
        o_ref[hh] = (ys * scale).astype(o_ref.dtype)


def _ip_z_kernel(x_ref, w_ref, nw_ref, o_ref):
    acc = jnp.dot(x_ref[...], w_ref[...], preferred_element_type=F32)
    for hh in range(acc.shape[1] // HEAD_DIM):
        zs = acc[:, hh * HEAD_DIM:(hh + 1) * HEAD_DIM]
        o_ref[hh] = _silu(zs) * nw_ref[...]


def _ip_xbc_kernel(group, x_ref, wxb_ref, wbg_ref, wcg_ref, taps_ref, o_ref):
    x = x_ref[...]
    xb = jnp.dot(x, wxb_ref[...], preferred_element_type=F32)
    bg = jnp.dot(x, wbg_ref[...], preferred_element_type=F32)
    cg = jnp.dot(x, wcg_ref[...], preferred_element_type=F32)
    o_ref[...] = (bg * _conv3_rows(cg * xb, taps_ref, group)).astype(o_ref.dtype)


def _ip_gates_kernel(x_ref, w_ref, o_ref):
    acc = jnp.dot(x_ref[...], w_ref[...], preferred_element_type=F32)
    o_ref[...] = jax.nn.sigmoid(acc).astype(o_ref.dtype)


def _ip_ba_kernel(x_ref, w_ref, alog_ref, dtb_ref, o_ref):
    acc = jnp.dot(x_ref[...], w_ref[...], preferred_element_type=F32)
    lane = lax.broadcasted_iota(jnp.int32, acc.shape, 1)
    a = acc + dtb_ref[...]
    softplus = jnp.maximum(a, 0.0) + jnp.log(1.0 + jnp.exp(-jnp.abs(a)))
    o_ref[...] = jnp.where(lane < 2 * HEADS, jax.nn.sigmoid(acc), -jnp.exp(alog_ref[...]) * softplus)


def _inproj(kernel, h, w_cat, col_starts, tn, n_tiles, extra, extra_specs, out_shape, out_spec, tm):
    t = h.shape[0]
    w_specs = [pl.BlockSpec((D_MODEL, tn), functools.partial(lambda j, i, o: (0, o + j), o=c // tn))
               for c in col_starts]
    return pl.pallas_call(
        kernel,
        out_shape=out_shape,
        grid=(n_tiles, t // tm),
        in_specs=[pl.BlockSpec((tm, D_MODEL), lambda j, i: (i, 0))] + w_specs + extra_specs,
        out_specs=out_spec,
        compiler_params=_cparams(("arbitrary", "arbitrary")),
        name="inproj",
    )(h, *([w_cat] * len(col_starts)), *extra)


def _inproj_qkv(h, w_cat, taps, group, tm):
    t = h.shape[0]
    tn = 1024
    hpt = tn // HEAD_DIM
    return _inproj(
        functools.partial(_ip_qkv_kernel, group), h, w_cat, [COL_QKV], tn, 3 * GDN_WIDTH // tn,
        [taps], [pl.BlockSpec((3, tn), lambda j, i: (0, j))],
        jax.ShapeDtypeStruct((3 * HEADS, t, HEAD_DIM), BF16),
        pl.BlockSpec((hpt, tm, HEAD_DIM), lambda j, i: (j, i, 0)), tm)


def _inproj_z(h, w_cat, norm_w, tm):
    t = h.shape[0]
    tn = 1024
    hpt = tn // HEAD_DIM
    return _inproj(
        _ip_z_kernel, h, w_cat, [COL_Z], tn, GDN_WIDTH // tn,
        [norm_w], [pl.BlockSpec((1, HEAD_DIM), lambda j, i: (0, 0))],
        jax.ShapeDtypeStruct((HEADS, t, HEAD_DIM), F32),
        pl.BlockSpec((hpt, tm, HEAD_DIM), lambda j, i: (j, i, 0)), tm)


def _inproj_xbc(h, w_cat, taps, group, tm):
    t = h.shape[0]
    tn = 512
    return _inproj(
        functools.partial(_ip_xbc_kernel, group), h, w_cat, [COL_XB, COL_BG, COL_CG], tn,
        CONV_WIDTH // tn, [taps], [pl.BlockSpec((3, tn), lambda j, i: (0, j))],
        jax.ShapeDtypeStruct((t, CONV_WIDTH), BF16),
        pl.BlockSpec((tm, tn), lambda j, i: (i, j)), tm)


def _inproj_gates(h, w_cat, tm):
    t = h.shape[0]
    tn = 1024
    return _inproj(
        _ip_gates_kernel, h, w_cat, [COL_GATES], tn, 2 * D_MODEL // tn, [], [],
        jax.ShapeDtypeStruct((t, 2 * D_MODEL), BF16),
        pl.BlockSpec((tm, tn), lambda j, i: (i, j)), tm)


def _inproj_ba(h, w_cat, alog_row, dtb_row, tm):
    t = h.shape[0]
    return _inproj(
        _ip_ba_kernel, h, w_cat, [COL_BA], LANES, 1, [alog_row, dtb_row],
        [pl.BlockSpec((1, LANES), lambda j, i: (0, 0))] * 2,
        jax.ShapeDtypeStruct((t, LANES), F32),
        pl.BlockSpec((tm, LANES), lambda j, i: (i, 0)), tm)


def _gdn_scal_kernel(g_ref, gc_ref, ep_ref, egl_ref):
    d = pl.program_id(0)
    r = lax.broadcasted_iota(jnp.int32, (CHUNK, CHUNK), 0)
    c = lax.broadcasted_iota(jnp.int32, (CHUNK, CHUNK), 1)
    tri = jnp.where(d == 0, (c <= r).astype(F32), (c >= r).astype(F32))
    gc = jnp.dot(tri, g_ref[0], preferred_element_type=F32, precision=HIGHEST)
    gl = jnp.where(d == 0, gc[CHUNK - 1:CHUNK, :], gc[0:1, :])
    gc_ref[0] = gc
    ep_ref[0] = jnp.exp(gl - gc)
    egl_ref[0] = jnp.exp(gl)


def _gdn_scal_call(g_s):
    u = g_s.shape[2]
    tu = 512 if u % 512 == 0 else UNITS
    spec = pl.BlockSpec((1, CHUNK, tu), lambda d, i: (d, 0, i))
    return pl.pallas_call(
        _gdn_scal_kernel,
        out_shape=(jax.ShapeDtypeStruct(g_s.shape, F32), jax.ShapeDtypeStruct(g_s.shape, F32),
                   jax.ShapeDtypeStruct((2, 1, u), F32)),
        grid=(2, u // tu),
        in_specs=[spec],
        out_specs=(spec, spec, pl.BlockSpec((1, 1, tu), lambda d, i: (d, 0, i))),
        compiler_params=_cparams(("arbitrary", "arbitrary")),
        name="gdn_scal",
    )(g_s)


def _nt_dot(a, b):
    return lax.dot_general(a, b, (((1,), (1,)), ((), ())), preferred_element_type=F32)


def _gdn_gram_kernel(cb, q_ref, k_ref, g_ref, qk_ref, kt_ref):
    r = lax.broadcasted_iota(jnp.int32, (HEAD_DIM, HEAD_DIM), 0)
    c = lax.broadcasted_iota(jnp.int32, (HEAD_DIM, HEAD_DIM), 1)
    eye = (r == c).astype(BF16)
    for ci in range(cb):
        k = k_ref[0, ci * CHUNK:(ci + 1) * CHUNK, :]
        q = q_ref[0, ci * CHUNK:(ci + 1) * CHUNK, :]
        g_ref[ci] = _nt_dot(k, k)
        qk_ref[ci] = _nt_dot(q, k)
        kt_ref[ci] = _nt_dot(eye, k).astype(BF16)


def _gdn_gram_call(qkv, nc):
    cb = min(32, nc)
    ncb = nc // cb
    u = HEADS * nc
    return pl.pallas_call(
        functools.partial(_gdn_gram_kernel, cb),
        out_shape=(jax.ShapeDtypeStruct((u, CHUNK, CHUNK), F32),
                   jax.ShapeDtypeStruct((u, CHUNK, CHUNK), F32),
                   jax.ShapeDtypeStruct((u, HEAD_DIM, CHUNK), BF16)),
        grid=(HEADS, ncb),
        in_specs=[pl.BlockSpec((1, cb * CHUNK, HEAD_DIM), lambda h, c: (h, c, 0)),
                  pl.BlockSpec((1, cb * CHUNK, HEAD_DIM), lambda h, c: (HEADS + h, c, 0))],
        out_specs=(pl.BlockSpec((cb, CHUNK, CHUNK), lambda h, c: (h * ncb + c, 0, 0)),
                   pl.BlockSpec((cb, CHUNK, CHUNK), lambda h, c: (h * ncb + c, 0, 0)),
                   pl.BlockSpec((cb, HEAD_DIM, CHUNK), lambda h, c: (h * ncb + c, 0, 0))),
        compiler_params=_cparams(("arbitrary", "arbitrary")),
        name="gdn_gram",
    )(qkv, qkv)


NB = CHUNK // SUBLANES


def _row_bcast(ref, row):
    return jnp.broadcast_to(ref[pl.ds(row, 1), :], (SUBLANES, UNITS))


def _gdn_inv_kernel(g_ref, qk_ref, beta_ref, gc_ref, twtu_ref, atde_ref, *scratch):
    for bwd in (False, True):
        @pl.when(pl.program_id(0) == int(bwd))
        def _(bwd=bwd):
            _gdn_inv_body(bwd, g_ref, qk_ref, beta_ref, gc_ref, twtu_ref.at[0], atde_ref.at[0], *scratch)


def _gdn_inv_body(bwd, g_ref, qk_ref, beta_ref, gc_ref, twtu_ref, atde_ref,
                  g_s, qk_s, l_s, t_s, tw_s, tu_s, at_s, de_s, e_s):
    pos = (lambda a: CHUNK - 1 - a) if bwd else (lambda a: a)
    blk = (lambda b: NB - 1 - b) if bwd else (lambda b: b)

    for pp in range(CHUNK // 2):
        sl = slice(pp * LANES, (pp + 1) * LANES)
        g_s[sl, :] = g_ref[:, sl].T
        qk_s[sl, :] = qk_ref[:, sl].T
    e_s[...] = jnp.exp(gc_ref[0])

    @pl.when(pl.program_id(1) == 0)
    def _():
        for ref in (tw_s, tu_s, at_s, de_s):
            ref[...] = jnp.zeros(ref.shape, F32)

    sub = lax.broadcasted_iota(jnp.int32, (SUBLANES, UNITS), 0)
    zero = jnp.zeros((SUBLANES, UNITS), F32)
    rs = range(SUBLANES)
    own = [(SUBLANES - 1 - r) if bwd else r for r in rs]
    earlier = [(sub > own[r]) if bwd else (sub < own[r]) for r in rs]

    def tile_ds(p, b):
        return pl.ds(pl.multiple_of(p * CHUNK + b * SUBLANES, SUBLANES), SUBLANES)

    def cols_ds(b):
        return pl.ds(pl.multiple_of(b * SUBLANES, SUBLANES), SUBLANES)

    def row_block(ib, carry):
        b_own = blk(ib)
        ps = [pos(ib * SUBLANES + r) for r in rs]
        gc_p = [_row_bcast(gc_ref.at[0], p) for p in ps]
        beta_p = [_row_bcast(beta_ref.at[0], p) for p in ps]

        def weights_offdiag(bc, c):
            b = blk(bc)
            gc_c = gc_ref[0, cols_ds(b), :]
            for r in rs:
                dec = jnp.exp(gc_p[r] - gc_c)
                l_s[tile_ds(ps[r], b), :] = beta_p[r] * g_s[tile_ds(ps[r], b), :] * dec
                at_s[tile_ds(ps[r], b), :] = qk_s[tile_ds(ps[r], b), :] * dec
            return c

        lax.fori_loop(0, ib, weights_offdiag, 0)
        gc_c = gc_ref[0, cols_ds(b_own), :]
        for r in rs:
            t = tile_ds(ps[r], b_own)
            dec = jnp.exp(gc_p[r] - gc_c)
            l_s[t, :] = jnp.where(earlier[r], beta_p[r] * g_s[t, :] * dec, 0.0)
            at_s[t, :] = jnp.where(sub == own[r], qk_s[t, :], jnp.where(earlier[r], qk_s[t, :] * dec, 0.0))
            de_s[t, :] = jnp.where(sub == own[r], _row_bcast(e_s, ps[r]), 0.0)

        def finish(b, acc):
            done = []
            for r in rs:
                a_r = acc[r]
                for kk in range(r):
                    a_r = a_r - _row_bcast(l_s, ps[r] * CHUNK + ps[kk]) * done[kk]
                done.append(a_r)
            beta_c = beta_ref[0, cols_ds(b), :]
            be_c = beta_c * e_s[cols_ds(b), :]
            for r in rs:
                t_s[tile_ds(ps[r], b), :] = done[r]
                tu_s[tile_ds(ps[r], b), :] = done[r] * beta_c
                tw_s[tile_ds(ps[r], b), :] = done[r] * be_c

        def subst_offdiag(bc, c):
            b = blk(bc)

            def k_block(kb, acc):
                acc = list(acc)
                for kk in rs:
                    pk = pos(kb * SUBLANES + kk)
                    t_k = t_s[tile_ds(pk, b), :]
                    for r in rs:
                        acc[r] = acc[r] - _row_bcast(l_s, ps[r] * CHUNK + pk) * t_k
                return tuple(acc)

            finish(b, lax.fori_loop(bc, ib, k_block, (zero,) * SUBLANES))
            return c

        lax.fori_loop(0, ib, subst_offdiag, 0)
        finish(b_own, [jnp.where(sub == own[r], 1.0, 0.0) for r in rs])
        return carry

    lax.fori_loop(0, NB, row_block, 0)

    for p in range(CHUNK):
        rows = slice(p * CHUNK, (p + 1) * CHUNK)
        sl = slice(p * LANES, (p + 1) * LANES)
        twtu_ref[:, sl] = jnp.concatenate([tw_s[rows, :], tu_s[rows, :]], axis=0).T.astype(BF16)
        atde_ref[:, sl] = jnp.concatenate([at_s[rows, :], de_s[rows, :]], axis=0).T.astype(BF16)


def _gdn_inv_call(g_flat, qk_flat, beta_s, gc_s):
    u = g_flat.shape[0]
    mat = pl.BlockSpec((UNITS, CHUNK * CHUNK), lambda d, i: (i, 0))
    sc = pl.BlockSpec((1, CHUNK, UNITS), lambda d, i: (d, 0, i))
    out = pl.BlockSpec((1, UNITS, CHUNK * LANES), lambda d, i: (d, i, 0))
    soa = pltpu.VMEM((CHUNK * CHUNK, UNITS), F32)
    return pl.pallas_call(
        _gdn_inv_kernel,
        out_shape=(jax.ShapeDtypeStruct((2, u, CHUNK * LANES), BF16),
                   jax.ShapeDtypeStruct((2, u, CHUNK * LANES), BF16)),
        grid=(2, u // UNITS),
        in_specs=[mat, mat, sc, sc],
        out_specs=(out, out),
        scratch_shapes=[soa] * 8 + [pltpu.VMEM((CHUNK, UNITS), F32)],
        compiler_params=_cparams(("arbitrary", "arbitrary")),
        name="gdn_inv",
    )(g_flat, qk_flat, beta_s, gc_s)


def _gdn_scan_kernel(hb, cb, egl_ref, q_ref, k_ref, v_ref, kt_ref, twtu_ref, atde_ref, ep_ref, s0_ref,
                     o_ref, sfin_ref, s_scr, w_scr, u_scr):
    d = pl.program_id(0)
    hg = pl.program_id(1)
    ci = pl.program_id(2)
    ncb = pl.num_programs(2)

    @pl.when(ci == 0)
    def _():
        s_scr[...] = s0_ref[0]

    cblk = ci + d * (ncb - 1 - 2 * ci)
    zeros = jnp.zeros((CHUNK, HEAD_DIM), BF16)

    def prepare(c, carry):
        r0 = pl.multiple_of(c * CHUNK, CHUNK)
        for hh in range(hb):
            k = k_ref[hh, pl.ds(r0, CHUNK), :]
            v = v_ref[hh, pl.ds(r0, CHUNK), :]
            rhs = jnp.concatenate([jnp.concatenate([k, zeros], axis=1),
                                   jnp.concatenate([zeros, v], axis=1)], axis=0)
            wu = jnp.dot(twtu_ref[0, hh, c], rhs, preferred_element_type=F32)
            w_scr[c, hh] = wu[:, :HEAD_DIM].astype(BF16)
            u_scr[c, hh] = wu[:, HEAD_DIM:]
        return carry

    lax.fori_loop(0, cb, prepare, 0)

    def chunk_step(cc, carry):
        c = cc + d * (cb - 1 - 2 * cc)
        r0 = pl.multiple_of(c * CHUNK, CHUNK)
        heads = range(hb)
        s = [s_scr[hh] for hh in heads]
        x = [jnp.dot(jnp.concatenate([w_scr[c, hh], q_ref[hh, pl.ds(r0, CHUNK), :]], axis=0),
                     s[hh].astype(BF16), preferred_element_type=F32) for hh in heads]
        v_new = [u_scr[c, hh] - x[hh][:CHUNK] for hh in heads]
        for hh in heads:
            kdt = (kt_ref[hh, c].astype(F32) * ep_ref[0, hh, pl.ds(c, 1), :]).astype(BF16)
            egl = egl_ref[d, hg * hb + hh, cblk * cb + c]
            s_scr[hh] = egl * s[hh] + jnp.dot(kdt, v_new[hh].astype(BF16), preferred_element_type=F32)
        for hh in heads:
            rhs = jnp.concatenate([v_new[hh], x[hh][CHUNK:]], axis=0).astype(BF16)
            o_ref[0, hh, pl.ds(r0, CHUNK), :] = jnp.dot(atde_ref[0, hh, c], rhs, preferred_element_type=F32)
        return carry

    lax.fori_loop(0, cb, chunk_step, 0)

    @pl.when(ci == ncb - 1)
    def _():
        sfin_ref[0] = s_scr[...]


def _gdn_scan_call(qkv, kt, twtu, atde, ep, egl, s0, nc):
    t = qkv.shape[1]
    hb = 8
    cb = min(8, nc)
    ncb = nc // cb
    hgs = HEADS // hb
    nat = lambda d, c: c + d * (ncb - 1 - 2 * c)
    tok = lambda part: pl.BlockSpec((hb, cb * CHUNK, HEAD_DIM),
                                    lambda d, hg, c, egl, part=part: (part * hgs + hg, nat(d, c), 0))
    per_dir = lambda last: pl.BlockSpec((1, hb, cb, CHUNK, last),
                                        lambda d, hg, c, egl: (d, hg, nat(d, c), 0, 0))
    grid_spec = pltpu.PrefetchScalarGridSpec(
        num_scalar_prefetch=1,
        grid=(2, hgs, ncb),
        in_specs=[tok(0), tok(1), tok(2),
                  pl.BlockSpec((hb, cb, HEAD_DIM, CHUNK), lambda d, hg, c, egl: (hg, nat(d, c), 0, 0)),
                  per_dir(LANES), per_dir(LANES),
                  pl.BlockSpec((1, hb, cb, CHUNK), lambda d, hg, c, egl: (d, hg, nat(d, c), 0)),
                  pl.BlockSpec((1, hb, HEAD_DIM, HEAD_DIM), lambda d, hg, c, egl: (d, hg, 0, 0))],
        out_specs=(pl.BlockSpec((1, hb, cb * CHUNK, HEAD_DIM), lambda d, hg, c, egl: (d, hg, nat(d, c), 0)),
                   pl.BlockSpec((1, hb, HEAD_DIM, HEAD_DIM), lambda d, hg, c, egl: (d, hg, 0, 0))),
        scratch_shapes=[pltpu.VMEM((hb, HEAD_DIM, HEAD_DIM), F32),
                        pltpu.VMEM((cb, hb, CHUNK, HEAD_DIM), BF16),
                        pltpu.VMEM((cb, hb, CHUNK, HEAD_DIM), F32)],
    )
    return pl.pallas_call(
        functools.partial(_gdn_scan_kernel, hb, cb),
        out_shape=(jax.ShapeDtypeStruct((2, HEADS, t, HEAD_DIM), F32),
                   jax.ShapeDtypeStruct((2, HEADS, HEAD_DIM, HEAD_DIM), F32)),
        grid_spec=grid_spec,
        compiler_params=_cparams(("arbitrary", "arbitrary", "arbitrary")),
        name="gdn_scan",
    )(egl, qkv, qkv, qkv, kt.reshape(HEADS, nc, HEAD_DIM, CHUNK),
      twtu.reshape(2, HEADS, nc, CHUNK, LANES), atde.reshape(2, HEADS, nc, CHUNK, LANES), ep, s0)


def _gdn(qkv, bg, s0):
    t = qkv.shape[1]
    nc = t // CHUNK
    u = HEADS * nc
    up = -(-u // UNITS) * UNITS
    to_soa = lambda a: jnp.pad(jnp.transpose(a.reshape(nc, CHUNK, 2, HEADS), (2, 1, 3, 0)).reshape(2, CHUNK, u),
                               ((0, 0), (0, 0), (0, up - u)))
    beta_s = to_soa(bg[:, :2 * HEADS])
    g_s = to_soa(bg[:, 2 * HEADS:4 * HEADS])
    gc_s, ep_s, egl_s = _gdn_scal_call(g_s)
    gram, qk, kt = _gdn_gram_call(qkv, nc)
    pad_u = lambda a: jnp.pad(a.reshape(u, -1), ((0, up - u), (0, 0)))
    g_flat, qk_flat = pad_u(gram), pad_u(qk)
    twtu, atde = (a[:, :u] for a in _gdn_inv_call(g_flat, qk_flat, beta_s, gc_s))
    ep = jnp.transpose(ep_s[:, :, :u].reshape(2, CHUNK, HEADS, nc), (0, 2, 3, 1))
    egl = egl_s[:, 0, :u].reshape(2, HEADS, nc)
    return _gdn_scan_call(qkv, kt, twtu, atde, ep, egl, s0, nc)


def _merge_kernel(o_ref, zs_ref, yc_ref, ga_ref, gb_ref, wa_ref, wb_ref, m_ref, yg_s):
    @pl.when(pl.program_id(1) == 0)
    def _():
        for hh in range(HEADS):
            o = o_ref[0, hh] + o_ref[1, hh]
            y = o * lax.rsqrt(jnp.mean(o * o, axis=-1, keepdims=True) + RMS_EPS) * zs_ref[hh]
            yg_s[:, hh * HEAD_DIM:(hh + 1) * HEAD_DIM] = y.astype(BF16)

    pa = jnp.dot(yg_s[...], wa_ref[...], preferred_element_type=F32)
    pb = jnp.dot(yc_ref[...], wb_ref[...], preferred_element_type=F32)
    m_ref[...] = (ga_ref[...].astype(F32) * pa + gb_ref[...].astype(F32) * pb).astype(m_ref.dtype)


def _merge_call(o, zs, yconv, gates, wa, wb):
    t = yconv.shape[0]
    tm, tn = 512, 512
    nj = D_MODEL // tn
    return pl.pallas_call(
        _merge_kernel,
        out_shape=jax.ShapeDtypeStruct((t, D_MODEL), BF16),
        grid=(t // tm, nj),
        in_specs=[pl.BlockSpec((2, HEADS, tm, HEAD_DIM), lambda i, j: (0, 0, i, 0)),
                  pl.BlockSpec((HEADS, tm, HEAD_DIM), lambda i, j: (0, i, 0)),
                  pl.BlockSpec((tm, CONV_WIDTH), lambda i, j: (i, 0)),
                  pl.BlockSpec((tm, tn), lambda i, j: (i, j)),
                  pl.BlockSpec((tm, tn), lambda i, j: (i, nj + j)),
                  pl.BlockSpec((GDN_WIDTH, tn), lambda i, j: (0, j)),
                  pl.BlockSpec((CONV_WIDTH, tn), lambda i, j: (0, j))],
        out_specs=pl.BlockSpec((tm, tn), lambda i, j: (i, j)),
        scratch_shapes=[pltpu.VMEM((tm, GDN_WIDTH), BF16)],
        compiler_params=_cparams(("arbitrary", "arbitrary")),
        name="merge",
    )(o, zs, yconv, gates, gates, wa, wb)


LN_ROWS = 64


def _outmm_kernel(m_ref, w_ref, o_ref):
    o_ref[...] = jnp.dot(m_ref[...], w_ref[...], preferred_element_type=F32)


def _outmm_call(m, w_out):
    t = m.shape[0]
    tm, tn = 512, 1024
    return pl.pallas_call(
        _outmm_kernel,
        out_shape=jax.ShapeDtypeStruct((t, D_MODEL), F32),
        grid=(D_MODEL // tn, t // tm),
        in_specs=[pl.BlockSpec((tm, D_MODEL), lambda j, i: (i, 0)),
                  pl.BlockSpec((D_MODEL, tn), lambda j, i: (0, j))],
        out_specs=pl.BlockSpec((tm, tn), lambda j, i: (i, j)),
        compiler_params=_cparams(("arbitrary", "arbitrary")),
        name="out_proj",
    )(m, w_out)


def _post_mix_kernel(mix_ref, x_ref, mod_ref, ln_ref, wrh_ref, wrl_ref, x1_ref, tok_ref, logit_ref):
    def rows_step(ci, c):
        sl = pl.ds(pl.multiple_of(ci * LN_ROWS, LN_ROWS), LN_ROWS)
        x1 = _layer_norm(DEEPNORM_ALPHA * x_ref[sl, :] + mod_ref[0:1, :] * mix_ref[sl, :])
        x1 = x1 * ln_ref[0:1, :] + ln_ref[1:2, :]
        x1_ref[sl, :] = x1
        tok_ref[sl, :] = _layer_norm(x1) * (1.0 + mod_ref[2:3, :]) + mod_ref[1:2, :]
        return c

    lax.fori_loop(0, x_ref.shape[0] // LN_ROWS, rows_step, 0)
    tok = tok_ref[...]
    t_hi = tok.astype(BF16)
    t_lo = (tok - t_hi.astype(F32)).astype(BF16)
    logit_ref[...] = (jnp.dot(t_hi, wrh_ref[...], preferred_element_type=F32)
                      + jnp.dot(t_lo, wrh_ref[...], preferred_element_type=F32)
                      + jnp.dot(t_hi, wrl_ref[...], preferred_element_type=F32))


def _post_mix_call(mix, x, mod3, ln1, w_router):
    wr_hi = w_router.astype(BF16)
    wr_lo = (w_router - wr_hi.astype(F32)).astype(BF16)
    t = x.shape[0]
    tm = 256
    row = pl.BlockSpec((tm, D_MODEL), lambda i: (i, 0))
    vec = pl.BlockSpec((SUBLANES, D_MODEL), lambda i: (0, 0))
    wr = pl.BlockSpec((D_MODEL, LANES), lambda i: (0, 0))
    return pl.pallas_call(
        _post_mix_kernel,
        out_shape=(jax.ShapeDtypeStruct((t, D_MODEL), F32),
                   jax.ShapeDtypeStruct((t, D_MODEL), F32),
                   jax.ShapeDtypeStruct((t, LANES), F32)),
        grid=(t // tm,),
        in_specs=[row, row, vec, vec, wr, wr],
        out_specs=(row, row, pl.BlockSpec((tm, LANES), lambda i: (i, 0))),
        compiler_params=_cparams(("arbitrary",)),
        name="post_mix",
    )(mix, x, mod3, ln1, wr_hi, wr_lo)


MOE_BM = 256
MOE_HC = 256


def _gather_kernel(rows, nu_ref, idx_ref, nxt_ref, src_ref, dst_ref, buf, sem):
    i = pl.program_id(0)
    nu = nu_ref[0]
    slot = i % 2

    def copy(iref, s, r):
        return pltpu.make_async_copy(src_ref.at[pl.ds(iref[0, 0, r], 1), :],
                                     buf.at[s, pl.ds(r, 1), :], sem.at[s])

    def issue(iref, s):
        def body(r, c):
            copy(iref, s, r).start()
            return c
        lax.fori_loop(0, rows, body, 0, unroll=8)

    @pl.when(i == 0)
    def _():
        issue(idx_ref, 0)

    @pl.when(i + 1 < nu)
    def _():
        issue(nxt_ref, 1 - slot)

    @pl.when(i < nu)
    def _():
        def body(r, c):
            copy(idx_ref, slot, r).wait()
            return c
        lax.fori_loop(0, rows, body, 0, unroll=8)
        dst_ref[...] = buf[slot].astype(dst_ref.dtype)

    @pl.when(i >= nu)
    def _():
        dst_ref[...] = jnp.zeros(dst_ref.shape, dst_ref.dtype)


def _gather_call(src, idx, n_used):
    n = idx.shape[0]
    rows = MOE_BM
    nblk = n // rows
    idx3 = idx.reshape(nblk, 1, rows)
    grid_spec = pltpu.PrefetchScalarGridSpec(
        num_scalar_prefetch=1,
        grid=(nblk,),
        in_specs=[pl.BlockSpec((1, 1, rows), lambda i, nu: (i, 0, 0), memory_space=pltpu.SMEM),
                  pl.BlockSpec((1, 1, rows), lambda i, nu: (jnp.minimum(i + 1, nblk - 1), 0, 0),
                               memory_space=pltpu.SMEM),
                  pl.BlockSpec(memory_space=pl.ANY)],
        out_specs=pl.BlockSpec((rows, src.shape[1]), lambda i, nu: (i, 0)),
        scratch_shapes=[pltpu.VMEM((2, rows, src.shape[1]), src.dtype), pltpu.SemaphoreType.DMA((2,))],
    )
    return pl.pallas_call(
        functools.partial(_gather_kernel, rows),
        out_shape=jax.ShapeDtypeStruct((n, src.shape[1]), BF16),
        grid_spec=grid_spec,
        compiler_params=_cparams(("arbitrary",)),
        name="moe_gather",
    )(n_used, idx3, idx3, src)


def _expert_changed(be_ref, nu_ref, b):
    bb = jnp.minimum(b, nu_ref[0] - 1)
    return (b == 0) | (be_ref[bb] != be_ref[jnp.maximum(bb - 1, 0)])


def _moe_hidden_kernel(be_ref, nu_ref, x_ref, wg_ref, wu_ref, h_ref, wg_s, wu_s):
    b = pl.program_id(0)
    hc = pl.program_id(1)
    live = b < nu_ref[0]

    @pl.when(live & _expert_changed(be_ref, nu_ref, b))
    def _():
        wg_s[hc] = wg_ref[0].astype(BF16)
        wu_s[hc] = wu_ref[0].astype(BF16)

    @pl.when(live)
    def _():
        x = x_ref[...]
        hg = jnp.dot(x, wg_s[hc], preferred_element_type=F32)
        hu = jnp.dot(x, wu_s[hc], preferred_element_type=F32)
        h_ref[...] = (_silu(hg) * hu).astype(h_ref.dtype)

    @pl.when(jnp.logical_not(live))
    def _():
        h_ref[...] = jnp.zeros(h_ref.shape, h_ref.dtype)


def _moe_down_kernel(be_ref, nu_ref, h_ref, wd_ref, y_ref, wd_s):
    b = pl.program_id(0)
    live = b < nu_ref[0]

    @pl.when(live & _expert_changed(be_ref, nu_ref, b))
    def _():
        wd_s[...] = wd_ref[0].astype(BF16)

    @pl.when(live)
    def _():
        y_ref[...] = jnp.dot(h_ref[...], wd_s[...], preferred_element_type=F32)

    @pl.when(jnp.logical_not(live))
    def _():
        y_ref[...] = jnp.zeros(y_ref.shape, F32)


def _expert_call(xs, wg, wu, wd, block_e, n_used):
    n = xs.shape[0]
    nb = n // MOE_BM
    nh = EXPERT_HIDDEN // MOE_HC
    blk = lambda b, nu: jnp.minimum(b, nu[0] - 1)

    def w_chunk(b, h, be, nu):
        bb = blk(b, nu)
        first = (b < nu[0]) & ((bb == 0) | (be[bb] != be[jnp.maximum(bb - 1, 0)]))
        return be[bb], 0, jnp.where(first, h, nh - 1)

    hid = pl.pallas_call(
        _moe_hidden_kernel,
        out_shape=jax.ShapeDtypeStruct((n, EXPERT_HIDDEN), BF16),
        grid_spec=pltpu.PrefetchScalarGridSpec(
            num_scalar_prefetch=2,
            grid=(nb, nh),
            in_specs=[pl.BlockSpec((MOE_BM, D_MODEL), lambda b, h, be, nu: (blk(b, nu), 0)),
                      pl.BlockSpec((1, D_MODEL, MOE_HC), w_chunk),
                      pl.BlockSpec((1, D_MODEL, MOE_HC), w_chunk)],
            out_specs=pl.BlockSpec((MOE_BM, MOE_HC), lambda b, h, be, nu: (b, h)),
            scratch_shapes=[pltpu.VMEM((nh, D_MODEL, MOE_HC), BF16)] * 2),
        compiler_params=_cparams(("arbitrary", "arbitrary")),
        name="moe_hidden",
    )(block_e, n_used, xs, wg, wu)
    return pl.pallas_call(
        _moe_down_kernel,
        out_shape=jax.ShapeDtypeStruct((n, D_MODEL), F32),
        grid_spec=pltpu.PrefetchScalarGridSpec(
            num_scalar_prefetch=2,
            grid=(nb,),
            in_specs=[pl.BlockSpec((MOE_BM, EXPERT_HIDDEN), lambda b, be, nu: (blk(b, nu), 0)),
                      pl.BlockSpec((1, EXPERT_HIDDEN, D_MODEL), lambda b, be, nu: (be[blk(b, nu)], 0, 0))],
            out_specs=pl.BlockSpec((MOE_BM, D_MODEL), lambda b, be, nu: (b, 0)),
            scratch_shapes=[pltpu.VMEM((EXPERT_HIDDEN, D_MODEL), BF16)]),
        compiler_params=_cparams(("arbitrary",)),
        name="moe_down",
    )(block_e, n_used, hid, wd)


def _combine_kernel(rows, d_ref, nxt_ref, y_ref, x1_ref, w_ref, mod_ref, ln_ref, o_ref, buf, sem):
    i = pl.program_id(0)
    slot = i % 2

    def copy(iref, s, k, r):
        return pltpu.make_async_copy(y_ref.at[pl.ds(iref[0, k, r], 1), :],
                                     buf.at[s, k, pl.ds(r, 1), :], sem.at[s])

    def issue(iref, s):
        def body(r, c):
            copy(iref, s, 0, r).start()
            copy(iref, s, 1, r).start()
            return c
        lax.fori_loop(0, rows, body, 0, unroll=8)

    @pl.when(i == 0)
    def _():
        issue(d_ref, 0)

    @pl.when(i + 1 < pl.num_programs(0))
    def _():
        issue(nxt_ref, 1 - slot)

    def wait_body(r, c):
        copy(d_ref, slot, 0, r).wait()
        copy(d_ref, slot, 1, r).wait()
        return c

    lax.fori_loop(0, rows, wait_body, 0, unroll=8)

    def rows_step(ci, c):
        sl = pl.ds(pl.multiple_of(ci * LN_ROWS, LN_ROWS), LN_ROWS)
        f = w_ref[sl, 0:1] * buf[slot, 0, sl, :] + w_ref[sl, 1:2] * buf[slot, 1, sl, :]
        y = _layer_norm(DEEPNORM_ALPHA * x1_ref[sl, :] + mod_ref[0:1, :] * f)
        o_ref[sl, :] = y * ln_ref[0:1, :] + ln_ref[1:2, :]
        return c

    lax.fori_loop(0, rows // LN_ROWS, rows_step, 0)


def _combine_call(ys, dest, weights, x1, mod_row, ln2):
    t = x1.shape[0]
    rows = 128
    nblk = t // rows
    dest3 = jnp.transpose(dest.reshape(nblk, rows, 2), (0, 2, 1))
    return pl.pallas_call(
        functools.partial(_combine_kernel, rows),
        out_shape=jax.ShapeDtypeStruct((t, D_MODEL), F32),
        grid=(nblk,),
        in_specs=[pl.BlockSpec((1, 2, rows), lambda i: (i, 0, 0), memory_space=pltpu.SMEM),
                  pl.BlockSpec((1, 2, rows), lambda i: (jnp.minimum(i + 1, nblk - 1), 0, 0),
                               memory_space=pltpu.SMEM),
                  pl.BlockSpec(memory_space=pl.ANY),
                  pl.BlockSpec((rows, D_MODEL), lambda i: (i, 0)),
                  pl.BlockSpec((rows, 2), lambda i: (i, 0)),
                  pl.BlockSpec((SUBLANES, D_MODEL), lambda i: (0, 0)),
                  pl.BlockSpec((SUBLANES, D_MODEL), lambda i: (0, 0))],
        out_specs=pl.BlockSpec((rows, D_MODEL), lambda i: (i, 0)),
        scratch_shapes=[pltpu.VMEM((2, 2, rows, D_MODEL), F32), pltpu.SemaphoreType.DMA((2,))],
        compiler_params=_cparams(("arbitrary",)),
        name="moe_combine",
    )(dest3, dest3, ys, x1, weights, mod_row, ln2)


def _route(logits, b_group, b_expert):
    t = logits.shape[0]
    p_group = jax.nn.softmax(logits[:, :N_GROUPS] + b_group, axis=-1)
    group = jnp.argmax(p_group, axis=-1)
    gate_group = jnp.take_along_axis(p_group, group[:, None], axis=-1)
    le = (logits[:, N_GROUPS:N_GROUPS + N_EXPERTS] + b_expert).reshape(t, N_GROUPS, EXPERTS_PER_GROUP)
    le = jnp.take_along_axis(le, group[:, None, None], axis=1)[:, 0]
    top_p, top_i = lax.top_k(jax.nn.softmax(le, axis=-1), 2)
    weights = gate_group * top_p / jnp.sum(top_p, axis=-1, keepdims=True)
    expert_id = group[:, None] * EXPERTS_PER_GROUP + top_i
    return expert_id.astype(jnp.int32), weights


def _moe(tok, logits, b_group, b_expert, wg, wu, wd, x1, mod_row, ln2):
    t = tok.shape[0]
    expert_id, weights = _route(logits, b_group, b_expert)
    e_flat = expert_id.reshape(-1)
    n_assign = e_flat.shape[0]
    onehot = (e_flat[:, None] == jnp.arange(N_EXPERTS)[None, :]).astype(jnp.int32)
    rank = jnp.take_along_axis(jnp.cumsum(onehot, axis=0), e_flat[:, None], axis=1)[:, 0] - 1
    counts = jnp.sum(onehot, axis=0)
    padded = (counts + MOE_BM - 1) // MOE_BM * MOE_BM
    pad_end = jnp.cumsum(padded)
    dest = (pad_end - padded)[e_flat] + rank
    n_blocks = -(-n_assign // MOE_BM) + N_EXPERTS
    n_rows = n_blocks * MOE_BM
    src_tok = jnp.zeros((n_rows,), jnp.int32).at[dest].set(jnp.arange(n_assign, dtype=jnp.int32) // 2)
    block_e = jnp.minimum(jnp.searchsorted(pad_end, jnp.arange(n_blocks) * MOE_BM, side="right"),
                          N_EXPERTS - 1).astype(jnp.int32)
    n_used = (pad_end[-1] // MOE_BM).astype(jnp.int32).reshape(1)
    xs = _gather_call(tok, src_tok, n_used)
    ys = _expert_call(xs, wg, wu, wd, block_e, n_used)
    return _combine_call(ys, dest.reshape(t, 2).astype(jnp.int32), weights, x1, mod_row, ln2)


def _pad_rows(v, rows=SUBLANES):
    return jnp.pad(v, ((0, rows - v.shape[0]), (0, 0)))


def _layer(x, ctx, c, c_ctx, w_ada, b_ada, w_in, conv_qkv, a_log, dt_bias, gdn_norm_w, conv_b,
           w_branch_a, w_branch_b, w_out, ln1_g, ln1_b, w_router_group, b_router_group,
           w_router_expert, b_router_expert, w_exp_gate, w_exp_up, w_exp_down, ln2_g, ln2_b):
    t = x.shape[0]
    mod = _mod_call(_pad_rows(jnp.stack([c, c_ctx])), w_ada, b_ada.reshape(1, -1))
    mod_lat = mod[0].reshape(N_MOD, D_MODEL)
    mod_ctx = mod[1].reshape(N_MOD, D_MODEL)

    n_a = 4 * GDN_WIDTH
    n_ba = 4 * HEADS
    w_pa = w_in[:, :n_a].astype(BF16)
    w_pb = w_in[:, n_a + n_ba:].astype(BF16)
    w_pc = jnp.pad(w_in[:, n_a:n_a + n_ba], ((0, 0), (0, LANES - n_ba))).astype(BF16)
    lane_row = lambda v: jnp.pad(v.reshape(1, -1), ((0, 0), (2 * HEADS, LANES - 4 * HEADS)))
    alog_row, dtb_row = lane_row(a_log), lane_row(dt_bias)

    h_ctx = _ln_mod_call(ctx, mod_ctx[0:1], mod_ctx[1:2])
    tc = ctx.shape[0]
    qkv_c = _inproj_qkv(h_ctx, w_pa, conv_qkv, tc, tc)
    bg_c = _inproj_ba(h_ctx, w_pc, alog_row, dtb_row, tc)
    s0 = jnp.zeros((2, HEADS, HEAD_DIM, HEAD_DIM), F32)
    _, s_ctx = _gdn(qkv_c, bg_c, s0)

    tm = 512
    h = _ln_mod_call(x, mod_lat[0:1], mod_lat[1:2])
    qkv = _inproj_qkv(h, w_pa, conv_qkv, GRID_W, tm)
    zs = _inproj_z(h, w_pa, gdn_norm_w.reshape(1, HEAD_DIM), tm)
    yconv = _inproj_xbc(h, w_pb, conv_b, GRID_W, tm)
    gates = _inproj_gates(h, w_pb, tm)
    bg = _inproj_ba(h, w_pc, alog_row, dtb_row, tm)
    o, _ = _gdn(qkv, bg, s_ctx)
    m = _merge_call(o, zs, yconv, gates, w_branch_a.astype(BF16), w_branch_b.astype(BF16))

    w_router = jnp.pad(jnp.concatenate([w_router_group, w_router_expert], axis=1),
                       ((0, 0), (0, LANES - N_GROUPS - N_EXPERTS)))
    mix = _outmm_call(m, w_out.astype(BF16))
    x1, tok, logits = _post_mix_call(mix, x, _pad_rows(mod_lat[2:5]),
                                     _pad_rows(jnp.stack([ln1_g, ln1_b])), w_router)

    return _moe(tok, logits, b_router_group, b_router_expert, w_exp_gate, w_exp_up, w_exp_down,
                x1, _pad_rows(mod_lat[5:6]), _pad_rows(jnp.stack([ln2_g, ln2_b])))


def kernel(x, c, ctx, c_ctx, w_ada, b_ada, w_in, conv_qkv, a_log, dt_bias, gdn_norm_w, conv_b,
           w_branch_a, w_branch_b, w_out, ln1_g, ln1_b, w_router_group, b_router_group,
           w_router_expert, b_router_expert, w_exp_gate, w_exp_up, w_exp_down, ln2_g, ln2_b):
    assert x.shape[0] == 1 and w_ada.shape[0] == 1, "single batch element, single layer"
    out = _layer(x[0], ctx[0], c[0], c_ctx, w_ada[0], b_ada[0], w_in[0], conv_qkv[0],
                 a_log[0].reshape(-1), dt_bias[0].reshape(-1), gdn_norm_w[0], conv_b[0],
                 w_branch_a[0], w_branch_b[0], w_out[0], ln1_g[0], ln1_b[0],
                 w_router_group[0], b_router_group[0], w_router_expert[0], b_router_expert[0],
                 w_exp_gate[0], w_exp_up[0], w_exp_down[0], ln2_g[0], ln2_b[0])
    return out[None]
```

```python
import functools

import numpy as np
import jax
import jax.numpy as jnp
from jax import lax
from jax.experimental import pallas as pl
from jax.experimental.pallas import tpu as pltpu

F32 = jnp.float32
BF16 = jnp.bfloat16
HIGHEST = lax.Precision.HIGHEST

D_MODEL = 4096
GRID_W = 64
CHUNK = 64
HEADS = 16
HEAD_DIM = 128
GDN_WIDTH = HEADS * HEAD_DIM
CONV_WIDTH = D_MODEL // 2
N_GROUPS = 8
EXPERTS_PER_GROUP = 8
N_EXPERTS = 64
EXPERT_HIDDEN = 768
N_MOD = 6
DEEPNORM_ALPHA = 2.0 ** 0.25
LN_EPS = 1e-6
RMS_EPS = 1e-6

LANES = 128
SUBLANES = 8
UNITS = 128
VMEM_LIMIT = 52 * 1024 * 1024


def _cparams(sem, vmem=VMEM_LIMIT):
    return pltpu.CompilerParams(dimension_semantics=sem, vmem_limit_bytes=vmem)


def _silu(x):
    return x * jax.nn.sigmoid(x)


def _layer_norm(x):
    mu = jnp.mean(x, axis=-1, keepdims=True)
    xc = x - mu
    var = jnp.mean(xc * xc, axis=-1, keepdims=True)
    return xc * lax.rsqrt(var + LN_EPS)


def _mod_kernel(c_ref, w_ref, b_ref, o_ref):
    s = _silu(c_ref[...])
    o_ref[...] = jnp.dot(s, w_ref[...], preferred_element_type=F32, precision=HIGHEST) + b_ref[...]


def _mod_call(cs, w_ada, b_ada):
    n = w_ada.shape[1]
    tn = 512
    return pl.pallas_call(
        _mod_kernel,
        out_shape=jax.ShapeDtypeStruct((SUBLANES, n), F32),
        grid=(n // tn,),
        in_specs=[pl.BlockSpec((SUBLANES, D_MODEL), lambda j: (0, 0)),
                  pl.BlockSpec((D_MODEL, tn), lambda j: (0, j)),
                  pl.BlockSpec((1, tn), lambda j: (0, j))],
        out_specs=pl.BlockSpec((SUBLANES, tn), lambda j: (0, j)),
        compiler_params=_cparams(("arbitrary",)),
        name="mod",
    )(cs, w_ada, b_ada)


def _ln_mod_kernel(x_ref, shift_ref, scale_ref, o_ref):
    y = _layer_norm(x_ref[...])
    o_ref[...] = (y * (1.0 + scale_ref[...]) + shift_ref[...]).astype(o_ref.dtype)


def _ln_mod_call(x, shift, scale):
    t = x.shape[0]
    tr = min(256, t)
    return pl.pallas_call(
        _ln_mod_kernel,
        out_shape=jax.ShapeDtypeStruct((t, D_MODEL), BF16),
        grid=(t // tr,),
        in_specs=[pl.BlockSpec((tr, D_MODEL), lambda i: (i, 0)),
                  pl.BlockSpec((1, D_MODEL), lambda i: (0, 0)),
                  pl.BlockSpec((1, D_MODEL), lambda i: (0, 0))],
        out_specs=pl.BlockSpec((tr, D_MODEL), lambda i: (i, 0)),
        compiler_params=_cparams(("arbitrary",)),
        name="ln_mod",
    )(x, shift, scale)


COL_QKV = 0
COL_Z = 3 * GDN_WIDTH
COL_XB = 0
COL_BG = CONV_WIDTH
COL_CG = 2 * CONV_WIDTH
COL_GATES = 3 * CONV_WIDTH
COL_BA = 0


def _conv3_rows(x, taps_ref, group):
    rows = x.shape[0]
    pos = lax.broadcasted_iota(jnp.int32, x.shape, 0) % group
    prev = jnp.where(pos == 0, 0.0, pltpu.roll(x, 1, 0))
    nxt = jnp.where(pos == group - 1, 0.0, pltpu.roll(x, rows - 1, 0))
    return taps_ref[0:1, :] * prev + taps_ref[1:2, :] * x + taps_ref[2:3, :] * nxt


def _ip_qkv_kernel(group, x_ref, w_ref, taps_ref, o_ref):
    acc = jnp.dot(x_ref[...], w_ref[...], preferred_element_type=F32)
    y = _silu(_conv3_rows(acc, taps_ref, group))
    j = pl.program_id(0)
    tiles_per_part = GDN_WIDTH // acc.shape[1]
    for hh in range(acc.shape[1] // HEAD_DIM):
        ys = y[:, hh * HEAD_DIM:(hh + 1) * HEAD_DIM]
        r = lax.rsqrt(jnp.sum(ys * ys, axis=-1, keepdims=True) + 1e-6)
        scale = jnp.where(j < tiles_per_part, r * HEAD_DIM ** -0.5,
                          jnp.where(j < 2 * tiles_per_part, r, 1.0))
        o_ref[hh] = (ys * scale).astype(o_ref.dtype)


def _ip_z_kernel(x_ref, w_ref, nw_ref, o_ref):
    acc = jnp.dot(x_ref[...], w_ref[...], preferred_element_type=F32)
    for hh in range(acc.shape[1] // HEAD_DIM):
        zs = acc[:, hh * HEAD_DIM:(hh + 1) * HEAD_DIM]
        o_ref[hh] = _silu(zs) * nw_ref[...]


def _ip_xbc_kernel(group, x_ref, wxb_ref, wbg_ref, wcg_ref, taps_ref, o_ref):
    x = x_ref[...]
    xb = jnp.dot(x, wxb_ref[...], preferred_element_type=F32)
    bg = jnp.dot(x, wbg_ref[...], preferred_element_type=F32)
    cg = jnp.dot(x, wcg_ref[...], preferred_element_type=F32)
    o_ref[...] = (bg * _conv3_rows(cg * xb, taps_ref, group)).astype(o_ref.dtype)


def _ip_gates_kernel(x_ref, w_ref, o_ref):
    acc = jnp.dot(x_ref[...], w_ref[...], preferred_element_type=F32)
    o_ref[...] = jax.nn.sigmoid(acc).astype(o_ref.dtype)


def _ip_ba_kernel(x_ref, w_ref, alog_ref, dtb_ref, o_ref):
    acc = jnp.dot(x_ref[...], w_ref[...], preferred_element_type=F32)
    lane = lax.broadcasted_iota(jnp.int32, acc.shape, 1)
    a = acc + dtb_ref[...]
    softplus = jnp.maximum(a, 0.0) + jnp.log(1.0 + jnp.exp(-jnp.abs(a)))
    o_ref[...] = jnp.where(lane < 2 * HEADS, jax.nn.sigmoid(acc), -jnp.exp(alog_ref[...]) * softplus)


def _inproj(kernel, h, w_cat, col_starts, tn, n_tiles, extra, extra_specs, out_shape, out_spec, tm):
    t = h.shape[0]
    w_specs = [pl.BlockSpec((D_MODEL, tn), functools.partial(lambda j, i, o: (0, o + j), o=c // tn))
               for c in col_starts]
    return pl.pallas_call(
        kernel,
        out_shape=out_shape,
        grid=(n_tiles, t // tm),
        in_specs=[pl.BlockSpec((tm, D_MODEL), lambda j, i: (i, 0))] + w_specs + extra_specs,
        out_specs=out_spec,
        compiler_params=_cparams(("arbitrary", "arbitrary")),
        name="inproj",
    )(h, *([w_cat] * len(col_starts)), *extra)


def _inproj_qkv(h, w_cat, taps, group, tm):
    t = h.shape[0]
    tn = 1024
    hpt = tn // HEAD_DIM
    return _inproj(
        functools.partial(_ip_qkv_kernel, group), h, w_cat, [COL_QKV], tn, 3 * GDN_WIDTH // tn,
        [taps], [pl.BlockSpec((3, tn), lambda j, i: (0, j))],
        jax.ShapeDtypeStruct((3 * HEADS, t, HEAD_DIM), BF16),
        pl.BlockSpec((hpt, tm, HEAD_DIM), lambda j, i: (j, i, 0)), tm)


def _inproj_z(h, w_cat, norm_w, tm):
    t = h.shape[0]
    tn = 1024
    hpt = tn // HEAD_DIM
    return _inproj(
        _ip_z_kernel, h, w_cat, [COL_Z], tn, GDN_WIDTH // tn,
        [norm_w], [pl.BlockSpec((1, HEAD_DIM), lambda j, i: (0, 0))],
        jax.ShapeDtypeStruct((HEADS, t, HEAD_DIM), F32),
        pl.BlockSpec((hpt, tm, HEAD_DIM), lambda j, i: (j, i, 0)), tm)


def _inproj_xbc(h, w_cat, taps, group, tm):
    t = h.shape[0]
    tn = 512
    return _inproj(
        functools.partial(_ip_xbc_kernel, group), h, w_cat, [COL_XB, COL_BG, COL_CG], tn,
        CONV_WIDTH // tn, [taps], [pl.BlockSpec((3, tn), lambda j, i: (0, j))],
        jax.ShapeDtypeStruct((t, CONV_WIDTH), BF16),
        pl.BlockSpec((tm, tn), lambda j, i: (i, j)), tm)


def _inproj_gates(h, w_cat, tm):
    t = h.shape[0]
    tn = 1024
    return _inproj(
        _ip_gates_kernel, h, w_cat, [COL_GATES], tn, 2 * D_MODEL // tn, [], [],
        jax.ShapeDtypeStruct((t, 2 * D_MODEL), BF16),
        pl.BlockSpec((tm, tn), lambda j, i: (i, j)), tm)


def _inproj_ba(h, w_cat, alog_row, dtb_row, tm):
    t = h.shape[0]
    return _inproj(
        _ip_ba_kernel, h, w_cat, [COL_BA], LANES, 1, [alog_row, dtb_row],
        [pl.BlockSpec((1, LANES), lambda j, i: (0, 0))] * 2,
        jax.ShapeDtypeStruct((t, LANES), F32),
        pl.BlockSpec((tm, LANES), lambda j, i: (i, 0)), tm)


def _gdn_scal_kernel(g_ref, gc_ref, ep_ref, egl_ref):
    d = pl.program_id(0)
    r = lax.broadcasted_iota(jnp.int32, (CHUNK, CHUNK), 0)
    c = lax.broadcasted_iota(jnp.int32, (CHUNK, CHUNK), 1)
    tri = jnp.where(d == 0, (c <= r).astype(F32), (c >= r).astype(F32))
    gc = jnp.dot(tri, g_ref[0], preferred_element_type=F32, precision=HIGHEST)
    gl = jnp.where(d == 0, gc[CHUNK - 1:CHUNK, :], gc[0:1, :])
    gc_ref[0] = gc
    ep_ref[0] = jnp.exp(gl - gc)
    egl_ref[0] = jnp.exp(gl)


def _gdn_scal_call(g_s):
    u = g_s.shape[2]
    tu = 512 if u % 512 == 0 else UNITS
    spec = pl.BlockSpec((1, CHUNK, tu), lambda d, i: (d, 0, i))
    return pl.pallas_call(
        _gdn_scal_kernel,
        out_shape=(jax.ShapeDtypeStruct(g_s.shape, F32), jax.ShapeDtypeStruct(g_s.shape, F32),
                   jax.ShapeDtypeStruct((2, 1, u), F32)),
        grid=(2, u // tu),
        in_specs=[spec],
        out_specs=(spec, spec, pl.BlockSpec((1, 1, tu), lambda d, i: (d, 0, i))),
        compiler_params=_cparams(("arbitrary", "arbitrary")),
        name="gdn_scal",
    )(g_s)


def _nt_dot(a, b):
    return lax.dot_general(a, b, (((1,), (1,)), ((), ())), preferred_element_type=F32)


def _gdn_gram_kernel(cb, q_ref, k_ref, g_ref, qk_ref, kt_ref):
    r = lax.broadcasted_iota(jnp.int32, (HEAD_DIM, HEAD_DIM), 0)
    c = lax.broadcasted_iota(jnp.int32, (HEAD_DIM, HEAD_DIM), 1)
    eye = (r == c).astype(BF16)
    for ci in range(cb):
        k = k_ref[0, ci * CHUNK:(ci + 1) * CHUNK, :]
        q = q_ref[0, ci * CHUNK:(ci + 1) * CHUNK, :]
        g_ref[ci] = _nt_dot(k, k)
        qk_ref[ci] = _nt_dot(q, k)
        kt_ref[ci] = _nt_dot(eye, k).astype(BF16)


def _gdn_gram_call(qkv, nc):
    cb = min(32, nc)
    ncb = nc // cb
    u = HEADS * nc
    return pl.pallas_call(
        functools.partial(_gdn_gram_kernel, cb),
        out_shape=(jax.ShapeDtypeStruct((u, CHUNK, CHUNK), F32),
                   jax.ShapeDtypeStruct((u, CHUNK, CHUNK), F32),
                   jax.ShapeDtypeStruct((u, HEAD_DIM, CHUNK), BF16)),
        grid=(HEADS, ncb),
        in_specs=[pl.BlockSpec((1, cb * CHUNK, HEAD_DIM), lambda h, c: (h, c, 0)),
                  pl.BlockSpec((1, cb * CHUNK, HEAD_DIM), lambda h, c: (HEADS + h, c, 0))],
        out_specs=(pl.BlockSpec((cb, CHUNK, CHUNK), lambda h, c: (h * ncb + c, 0, 0)),
                   pl.BlockSpec((cb, CHUNK, CHUNK), lambda h, c: (h * ncb + c, 0, 0)),
                   pl.BlockSpec((cb, HEAD_DIM, CHUNK), lambda h, c: (h * ncb + c, 0, 0))),
        compiler_params=_cparams(("arbitrary", "arbitrary")),
        name="gdn_gram",
    )(qkv, qkv)


NB = CHUNK // SUBLANES


def _row_bcast(ref, row):
    return jnp.broadcast_to(ref[pl.ds(row, 1), :], (SUBLANES, UNITS))


def _gdn_inv_kernel(g_ref, qk_ref, beta_ref, gc_ref, twtu_ref, atde_ref, *scratch):
    for bwd in (False, True):
        @pl.when(pl.program_id(0) == int(bwd))
        def _(bwd=bwd):
            _gdn_inv_body(bwd, g_ref, qk_ref, beta_ref, gc_ref, twtu_ref.at[0], atde_ref.at[0], *scratch)


def _gdn_inv_body(bwd, g_ref, qk_ref, beta_ref, gc_ref, twtu_ref, atde_ref,
                  g_s, qk_s, l_s, t_s, tw_s, tu_s, at_s, de_s, e_s):
    pos = (lambda a: CHUNK - 1 - a) if bwd else (lambda a: a)
    blk = (lambda b: NB - 1 - b) if bwd else (lambda b: b)

    for pp in range(CHUNK // 2):
        sl = slice(pp * LANES, (pp + 1) * LANES)
        g_s[sl, :] = g_ref[:, sl].T
        qk_s[sl, :] = qk_ref[:, sl].T
    e_s[...] = jnp.exp(gc_ref[0])

    @pl.when(pl.program_id(1) == 0)
    def _():
        for ref in (tw_s, tu_s, at_s, de_s):
            ref[...] = jnp.zeros(ref.shape, F32)

    sub = lax.broadcasted_iota(jnp.int32, (SUBLANES, UNITS), 0)
    zero = jnp.zeros((SUBLANES, UNITS), F32)
    rs = range(SUBLANES)
    own = [(SUBLANES - 1 - r) if bwd else r for r in rs]
    earlier = [(sub > own[r]) if bwd else (sub < own[r]) for r in rs]

    def tile_ds(p, b):
        return pl.ds(pl.multiple_of(p * CHUNK + b * SUBLANES, SUBLANES), SUBLANES)

    def cols_ds(b):
        return pl.ds(pl.multiple_of(b * SUBLANES, SUBLANES), SUBLANES)

    def row_block(ib, carry):
        b_own = blk(ib)
        ps = [pos(ib * SUBLANES + r) for r in rs]
        gc_p = [_row_bcast(gc_ref.at[0], p) for p in ps]
        beta_p = [_row_bcast(beta_ref.at[0], p) for p in ps]

        def weights_offdiag(bc, c):
            b = blk(bc)
            gc_c = gc_ref[0, cols_ds(b), :]
            for r in rs:
                dec = jnp.exp(gc_p[r] - gc_c)
                l_s[tile_ds(ps[r], b), :] = beta_p[r] * g_s[tile_ds(ps[r], b), :] * dec
                at_s[tile_ds(ps[r], b), :] = qk_s[tile_ds(ps[r], b), :] * dec
            return c

        lax.fori_loop(0, ib, weights_offdiag, 0)
        gc_c = gc_ref[0, cols_ds(b_own), :]
        for r in rs:
            t = tile_ds(ps[r], b_own)
            dec = jnp.exp(gc_p[r] - gc_c)
            l_s[t, :] = jnp.where(earlier[r], beta_p[r] * g_s[t, :] * dec, 0.0)
            at_s[t, :] = jnp.where(sub == own[r], qk_s[t, :], jnp.where(earlier[r], qk_s[t, :] * dec, 0.0))
            de_s[t, :] = jnp.where(sub == own[r], _row_bcast(e_s, ps[r]), 0.0)

        def finish(b, acc):
            done = []
            for r in rs:
                a_r = acc[r]
                for kk in range(r):
                    a_r = a_r - _row_bcast(l_s, ps[r] * CHUNK + ps[kk]) * done[kk]
                done.append(a_r)
            beta_c = beta_ref[0, cols_ds(b), :]
            be_c = beta_c * e_s[cols_ds(b), :]
            for r in rs:
                t_s[tile_ds(ps[r], b), :] = done[r]
                tu_s[tile_ds(ps[r], b), :] = done[r] * beta_c
                tw_s[tile_ds(ps[r], b), :] = done[r] * be_c

        def subst_offdiag(bc, c):
            b = blk(bc)

            def k_block(kb, acc):
                acc = list(acc)
                for kk in rs:
                    pk = pos(kb * SUBLANES + kk)
                    t_k = t_s[tile_ds(pk, b), :]
                    for r in rs:
                        acc[r] = acc[r] - _row_bcast(l_s, ps[r] * CHUNK + pk) * t_k
                return tuple(acc)

            finish(b, lax.fori_loop(bc, ib, k_block, (zero,) * SUBLANES))
            return c

        lax.fori_loop(0, ib, subst_offdiag, 0)
        finish(b_own, [jnp.where(sub == own[r], 1.0, 0.0) for r in rs])
        return carry

    lax.fori_loop(0, NB, row_block, 0)

    for p in range(CHUNK):
        rows = slice(p * CHUNK, (p + 1) * CHUNK)
        sl = slice(p * LANES, (p + 1) * LANES)
        twtu_ref[:, sl] = jnp.concatenate([tw_s[rows, :], tu_s[rows, :]], axis=0).T.astype(BF16)
        atde_ref[:, sl] = jnp.concatenate([at_s[rows, :], de_s[rows, :]], axis=0).T.astype(BF16)


def _gdn_inv_call(g_flat, qk_flat, beta_s, gc_s):
    u = g_flat.shape[0]
    mat = pl.BlockSpec((UNITS, CHUNK * CHUNK), lambda d, i: (i, 0))
    sc = pl.BlockSpec((1, CHUNK, UNITS), lambda d, i: (d, 0, i))
    out = pl.BlockSpec((1, UNITS, CHUNK * LANES), lambda d, i: (d, i, 0))
    soa = pltpu.VMEM((CHUNK * CHUNK, UNITS), F32)
    return pl.pallas_call(
        _gdn_inv_kernel,
        out_shape=(jax.ShapeDtypeStruct((2, u, CHUNK * LANES), BF16),
                   jax.ShapeDtypeStruct((2, u, CHUNK * LANES), BF16)),
        grid=(2, u // UNITS),
        in_specs=[mat, mat, sc, sc],
        out_specs=(out, out),
        scratch_shapes=[soa] * 8 + [pltpu.VMEM((CHUNK, UNITS), F32)],
        compiler_params=_cparams(("arbitrary", "arbitrary")),
        name="gdn_inv",
    )(g_flat, qk_flat, beta_s, gc_s)


def _gdn_scan_kernel(hb, cb, egl_ref, q_ref, k_ref, v_ref, kt_ref, twtu_ref, atde_ref, ep_ref, s0_ref,
                     o_ref, sfin_ref, s_scr, w_scr, u_scr):
    d = pl.program_id(0)
    hg = pl.program_id(1)
    ci = pl.program_id(2)
    ncb = pl.num_programs(2)

    @pl.when(ci == 0)
    def _():
        s_scr[...] = s0_ref[0]

    cblk = ci + d * (ncb - 1 - 2 * ci)
    zeros = jnp.zeros((CHUNK, HEAD_DIM), BF16)

    def prepare(c, carry):
        r0 = pl.multiple_of(c * CHUNK, CHUNK)
        for hh in range(hb):
            k = k_ref[hh, pl.ds(r0, CHUNK), :]
            v = v_ref[hh, pl.ds(r0, CHUNK), :]
            rhs = jnp.concatenate([jnp.concatenate([k, zeros], axis=1),
                                   jnp.concatenate([zeros, v], axis=1)], axis=0)
            wu = jnp.dot(twtu_ref[0, hh, c], rhs, preferred_element_type=F32)
            w_scr[c, hh] = wu[:, :HEAD_DIM].astype(BF16)
            u_scr[c, hh] = wu[:, HEAD_DIM:]
        return carry

    lax.fori_loop(0, cb, prepare, 0)

    def chunk_step(cc, carry):
        c = cc + d * (cb - 1 - 2 * cc)
        r0 = pl.multiple_of(c * CHUNK, CHUNK)
        heads = range(hb)
        s = [s_scr[hh] for hh in heads]
        x = [jnp.dot(jnp.concatenate([w_scr[c, hh], q_ref[hh, pl.ds(r0, CHUNK), :]], axis=0),
                     s[hh].astype(BF16), preferred_element_type=F32) for hh in heads]
        v_new = [u_scr[c, hh] - x[hh][:CHUNK] for hh in heads]
        for hh in heads:
            kdt = (kt_ref[hh, c].astype(F32) * ep_ref[0, hh, pl.ds(c, 1), :]).astype(BF16)
            egl = egl_ref[d, hg * hb + hh, cblk * cb + c]
            s_scr[hh] = egl * s[hh] + jnp.dot(kdt, v_new[hh].astype(BF16), preferred_element_type=F32)
        for hh in heads:
            rhs = jnp.concatenate([v_new[hh], x[hh][CHUNK:]], axis=0).astype(BF16)
            o_ref[0, hh, pl.ds(r0, CHUNK), :] = jnp.dot(atde_ref[0, hh, c], rhs,
                                                        preferred_element_type=F32).astype(o_ref.dtype)
        return carry

    lax.fori_loop(0, cb, chunk_step, 0)

    @pl.when(ci == ncb - 1)
    def _():
        sfin_ref[0] = s_scr[...]


def _gdn_scan_call(qkv, kt, twtu, atde, ep, egl, s0, nc):
    t = qkv.shape[1]
    hb = HEADS
    cb = min(8, nc)
    ncb = nc // cb
    hgs = HEADS // hb
    nat = lambda d, c: c + d * (ncb - 1 - 2 * c)
    tok = lambda part: pl.BlockSpec((hb, cb * CHUNK, HEAD_DIM),
                                    lambda d, hg, c, egl, part=part: (part * hgs + hg, nat(d, c), 0))
    per_dir = lambda last: pl.BlockSpec((1, hb, cb, CHUNK, last),
                                        lambda d, hg, c, egl: (d, hg, nat(d, c), 0, 0))
    grid_spec = pltpu.PrefetchScalarGridSpec(
        num_scalar_prefetch=1,
        grid=(2, hgs, ncb),
        in_specs=[tok(0), tok(1), tok(2),
                  pl.BlockSpec((hb, cb, HEAD_DIM, CHUNK), lambda d, hg, c, egl: (hg, nat(d, c), 0, 0)),
                  per_dir(LANES), per_dir(LANES),
                  pl.BlockSpec((1, hb, cb, CHUNK), lambda d, hg, c, egl: (d, hg, nat(d, c), 0)),
                  pl.BlockSpec((1, hb, HEAD_DIM, HEAD_DIM), lambda d, hg, c, egl: (d, hg, 0, 0))],
        out_specs=(pl.BlockSpec((1, hb, cb * CHUNK, HEAD_DIM), lambda d, hg, c, egl: (d, hg, nat(d, c), 0)),
                   pl.BlockSpec((1, hb, HEAD_DIM, HEAD_DIM), lambda d, hg, c, egl: (d, hg, 0, 0))),
        scratch_shapes=[pltpu.VMEM((hb, HEAD_DIM, HEAD_DIM), F32),
                        pltpu.VMEM((cb, hb, CHUNK, HEAD_DIM), BF16),
                        pltpu.VMEM((cb, hb, CHUNK, HEAD_DIM), F32)],
    )
    return pl.pallas_call(
        functools.partial(_gdn_scan_kernel, hb, cb),
        out_shape=(jax.ShapeDtypeStruct((2, HEADS, t, HEAD_DIM), BF16),
                   jax.ShapeDtypeStruct((2, HEADS, HEAD_DIM, HEAD_DIM), F32)),
        grid_spec=grid_spec,
        compiler_params=_cparams(("arbitrary", "arbitrary", "arbitrary")),
        name="gdn_scan",
    )(egl, qkv, qkv, qkv, kt.reshape(HEADS, nc, HEAD_DIM, CHUNK),
      twtu.reshape(2, HEADS, nc, CHUNK, LANES), atde.reshape(2, HEADS, nc, CHUNK, LANES), ep, s0)


def _gdn(qkv, bg, s0):
    t = qkv.shape[1]
    nc = t // CHUNK
    u = HEADS * nc
    up = -(-u // UNITS) * UNITS
    to_soa = lambda a: jnp.pad(jnp.transpose(a.reshape(nc, CHUNK, 2, HEADS), (2, 1, 3, 0)).reshape(2, CHUNK, u),
                               ((0, 0), (0, 0), (0, up - u)))
    beta_s = to_soa(bg[:, :2 * HEADS])
    g_s = to_soa(bg[:, 2 * HEADS:4 * HEADS])
    gc_s, ep_s, egl_s = _gdn_scal_call(g_s)
    gram, qk, kt = _gdn_gram_call(qkv, nc)
    pad_u = lambda a: jnp.pad(a.reshape(u, -1), ((0, up - u), (0, 0)))
    g_flat, qk_flat = pad_u(gram), pad_u(qk)
    twtu, atde = (a[:, :u] for a in _gdn_inv_call(g_flat, qk_flat, beta_s, gc_s))
    ep = jnp.transpose(ep_s[:, :, :u].reshape(2, CHUNK, HEADS, nc), (0, 2, 3, 1))
    egl = egl_s[:, 0, :u].reshape(2, HEADS, nc)
    return _gdn_scan_call(qkv, kt, twtu, atde, ep, egl, s0, nc)


def _merge_kernel(o_ref, zs_ref, yc_ref, ga_ref, gb_ref, wa_ref, wb_ref, m_ref, yg_s):
    @pl.when(pl.program_id(1) == 0)
    def _():
        for hh in range(HEADS):
            o = o_ref[0, hh].astype(F32) + o_ref[1, hh].astype(F32)
            y = o * lax.rsqrt(jnp.mean(o * o, axis=-1, keepdims=True) + RMS_EPS) * zs_ref[hh]
            yg_s[:, hh * HEAD_DIM:(hh + 1) * HEAD_DIM] = y.astype(BF16)

    pa = jnp.dot(yg_s[...], wa_ref[...], preferred_element_type=F32)
    pb = jnp.dot(yc_ref[...], wb_ref[...], preferred_element_type=F32)
    m_ref[...] = (ga_ref[...].astype(F32) * pa + gb_ref[...].astype(F32) * pb).astype(m_ref.dtype)


def _merge_call(o, zs, yconv, gates, wa, wb):
    t = yconv.shape[0]
    tm, tn = 512, 1024
    nj = D_MODEL // tn
    return pl.pallas_call(
        _merge_kernel,
        out_shape=jax.ShapeDtypeStruct((t, D_MODEL), BF16),
        grid=(t // tm, nj),
        in_specs=[pl.BlockSpec((2, HEADS, tm, HEAD_DIM), lambda i, j: (0, 0, i, 0)),
                  pl.BlockSpec((HEADS, tm, HEAD_DIM), lambda i, j: (0, i, 0)),
                  pl.BlockSpec((tm, CONV_WIDTH), lambda i, j: (i, 0)),
                  pl.BlockSpec((tm, tn), lambda i, j: (i, j)),
                  pl.BlockSpec((tm, tn), lambda i, j: (i, nj + j)),
                  pl.BlockSpec((GDN_WIDTH, tn), lambda i, j: (0, j)),
                  pl.BlockSpec((CONV_WIDTH, tn), lambda i, j: (0, j))],
        out_specs=pl.BlockSpec((tm, tn), lambda i, j: (i, j)),
        scratch_shapes=[pltpu.VMEM((tm, GDN_WIDTH), BF16)],
        compiler_params=_cparams(("arbitrary", "arbitrary")),
        name="merge",
    )(o, zs, yconv, gates, gates, wa, wb)


LN_ROWS = 64


def _outmm_kernel(m_ref, w_ref, o_ref):
    o_ref[...] = jnp.dot(m_ref[...], w_ref[...], preferred_element_type=F32)


def _outmm_call(m, w_out):
    t = m.shape[0]
    tm, tn = 512, 1024
    return pl.pallas_call(
        _outmm_kernel,
        out_shape=jax.ShapeDtypeStruct((t, D_MODEL), F32),
        grid=(D_MODEL // tn, t // tm),
        in_specs=[pl.BlockSpec((tm, D_MODEL), lambda j, i: (i, 0)),
                  pl.BlockSpec((D_MODEL, tn), lambda j, i: (0, j))],
        out_specs=pl.BlockSpec((tm, tn), lambda j, i: (i, j)),
        compiler_params=_cparams(("arbitrary", "arbitrary")),
        name="out_proj",
    )(m, w_out)


HALF = D_MODEL // 2


def _bf16_bits(v):
    return pltpu.bitcast(v.astype(BF16).astype(F32), jnp.uint32)


def _post_mix_kernel(mix_ref, x_ref, mod_ref, ln_ref, wrh_ref, wrl_ref, x1_ref, tokp_ref, logit_ref, tok_s):
    def rows_step(ci, c):
        sl = pl.ds(pl.multiple_of(ci * LN_ROWS, LN_ROWS), LN_ROWS)
        x1 = _layer_norm(DEEPNORM_ALPHA * x_ref[sl, :] + mod_ref[0:1, :] * mix_ref[sl, :])
        x1 = x1 * ln_ref[0:1, :] + ln_ref[1:2, :]
        x1_ref[sl, :] = x1
        tok = _layer_norm(x1) * (1.0 + mod_ref[2:3, :]) + mod_ref[1:2, :]
        tok_s[sl, :] = tok
        tokp_ref[sl, :] = _bf16_bits(tok[:, :HALF]) | (_bf16_bits(tok[:, HALF:]) >> 16)
        return c

    lax.fori_loop(0, x_ref.shape[0] // LN_ROWS, rows_step, 0)
    tok = tok_s[...]
    t_hi = tok.astype(BF16)
    t_lo = (tok - t_hi.astype(F32)).astype(BF16)
    logit_ref[...] = (jnp.dot(t_hi, wrh_ref[...], preferred_element_type=F32)
                      + jnp.dot(t_lo, wrh_ref[...], preferred_element_type=F32)
                      + jnp.dot(t_hi, wrl_ref[...], preferred_element_type=F32))


def _post_mix_call(mix, x, mod3, ln1, w_router):
    wr_hi = w_router.astype(BF16)
    wr_lo = (w_router - wr_hi.astype(F32)).astype(BF16)
    t = x.shape[0]
    tm = 256
    row = pl.BlockSpec((tm, D_MODEL), lambda i: (i, 0))
    vec = pl.BlockSpec((SUBLANES, D_MODEL), lambda i: (0, 0))
    wr = pl.BlockSpec((D_MODEL, LANES), lambda i: (0, 0))
    return pl.pallas_call(
        _post_mix_kernel,
        out_shape=(jax.ShapeDtypeStruct((t, D_MODEL), F32),
                   jax.ShapeDtypeStruct((t, HALF), jnp.uint32),
                   jax.ShapeDtypeStruct((t, LANES), F32)),
        grid=(t // tm,),
        in_specs=[row, row, vec, vec, wr, wr],
        out_specs=(row, pl.BlockSpec((tm, HALF), lambda i: (i, 0)), pl.BlockSpec((tm, LANES), lambda i: (i, 0))),
        scratch_shapes=[pltpu.VMEM((tm, D_MODEL), F32)],
        compiler_params=_cparams(("arbitrary",)),
        name="post_mix",
    )(mix, x, mod3, ln1, wr_hi, wr_lo)


MOE_BM = 256
MOE_HC = 256


def _gather_kernel(rows, nu_ref, idx_ref, nxt_ref, src_ref, dst_ref, buf, sem):
    i = pl.program_id(0)
    nu = nu_ref[0]
    slot = i % 2

    def copy(iref, s, r):
        return pltpu.make_async_copy(src_ref.at[pl.ds(iref[0, 0, r], 1), :],
                                     buf.at[s, pl.ds(r, 1), :], sem.at[s])

    def issue(iref, s):
        def body(r, c):
            copy(iref, s, r).start()
            return c
        lax.fori_loop(0, rows, body, 0, unroll=8)

    @pl.when(i == 0)
    def _():
        issue(idx_ref, 0)

    @pl.when(i + 1 < nu)
    def _():
        issue(nxt_ref, 1 - slot)

    @pl.when(i < nu)
    def _():
        def body(r, c):
            copy(idx_ref, slot, r).wait()
            return c
        lax.fori_loop(0, rows, body, 0, unroll=8)
        packed = buf[slot]
        dst_ref[:, :HALF] = pltpu.bitcast(packed & jnp.uint32(0xFFFF0000), F32).astype(BF16)
        dst_ref[:, HALF:] = pltpu.bitcast(packed << 16, F32).astype(BF16)

    @pl.when(i >= nu)
    def _():
        dst_ref[...] = jnp.zeros(dst_ref.shape, dst_ref.dtype)


def _gather_call(src, idx, n_used):
    n = idx.shape[0]
    rows = MOE_BM
    nblk = n // rows
    idx3 = idx.reshape(nblk, 1, rows)
    grid_spec = pltpu.PrefetchScalarGridSpec(
        num_scalar_prefetch=1,
        grid=(nblk,),
        in_specs=[pl.BlockSpec((1, 1, rows), lambda i, nu: (i, 0, 0), memory_space=pltpu.SMEM),
                  pl.BlockSpec((1, 1, rows), lambda i, nu: (jnp.minimum(i + 1, nblk - 1), 0, 0),
                               memory_space=pltpu.SMEM),
                  pl.BlockSpec(memory_space=pl.ANY)],
        out_specs=pl.BlockSpec((rows, D_MODEL), lambda i, nu: (i, 0)),
        scratch_shapes=[pltpu.VMEM((2, rows, src.shape[1]), src.dtype), pltpu.SemaphoreType.DMA((2,))],
    )
    return pl.pallas_call(
        functools.partial(_gather_kernel, rows),
        out_shape=jax.ShapeDtypeStruct((n, D_MODEL), BF16),
        grid_spec=grid_spec,
        compiler_params=_cparams(("arbitrary",)),
        name="moe_gather",
    )(n_used, idx3, idx3, src)


def _expert_changed(be_ref, nu_ref, b):
    bb = jnp.minimum(b, nu_ref[0] - 1)
    return (b == 0) | (be_ref[bb] != be_ref[jnp.maximum(bb - 1, 0)])


def _moe_hidden_kernel(be_ref, nu_ref, x_ref, wg_ref, wu_ref, h_ref, w_s):
    b = pl.program_id(0)
    hc = pl.program_id(1)
    live = b < nu_ref[0]

    @pl.when(live & _expert_changed(be_ref, nu_ref, b))
    def _():
        w_s[hc, :, :MOE_HC] = wg_ref[0].astype(BF16)
        w_s[hc, :, MOE_HC:] = wu_ref[0].astype(BF16)

    @pl.when(live)
    def _():
        gu = jnp.dot(x_ref[...], w_s[hc], preferred_element_type=F32)
        h_ref[...] = (_silu(gu[:, :MOE_HC]) * gu[:, MOE_HC:]).astype(h_ref.dtype)

    @pl.when(jnp.logical_not(live))
    def _():
        h_ref[...] = jnp.zeros(h_ref.shape, h_ref.dtype)


def _moe_down_kernel(be_ref, nu_ref, h_ref, wd_ref, y_ref, wd_s):
    b = pl.program_id(0)
    live = b < nu_ref[0]

    @pl.when(live & _expert_changed(be_ref, nu_ref, b))
    def _():
        wd_s[...] = wd_ref[0].astype(BF16)

    @pl.when(live)
    def _():
        y_ref[...] = jnp.dot(h_ref[...], wd_s[...], preferred_element_type=F32)

    @pl.when(jnp.logical_not(live))
    def _():
        y_ref[...] = jnp.zeros(y_ref.shape, F32)


def _expert_call(xs, wg, wu, wd, block_e, n_used):
    n = xs.shape[0]
    nb = n // MOE_BM
    nh = EXPERT_HIDDEN // MOE_HC
    blk = lambda b, nu: jnp.minimum(b, nu[0] - 1)

    def w_chunk(b, h, be, nu):
        bb = blk(b, nu)
        first = (b < nu[0]) & ((bb == 0) | (be[bb] != be[jnp.maximum(bb - 1, 0)]))
        return be[bb], 0, jnp.where(first, h, nh - 1)

    hid = pl.pallas_call(
        _moe_hidden_kernel,
        out_shape=jax.ShapeDtypeStruct((n, EXPERT_HIDDEN), BF16),
        grid_spec=pltpu.PrefetchScalarGridSpec(
            num_scalar_prefetch=2,
            grid=(nb, nh),
            in_specs=[pl.BlockSpec((MOE_BM, D_MODEL), lambda b, h, be, nu: (blk(b, nu), 0)),
                      pl.BlockSpec((1, D_MODEL, MOE_HC), w_chunk),
                      pl.BlockSpec((1, D_MODEL, MOE_HC), w_chunk)],
            out_specs=pl.BlockSpec((MOE_BM, MOE_HC), lambda b, h, be, nu: (b, h)),
            scratch_shapes=[pltpu.VMEM((nh, D_MODEL, 2 * MOE_HC), BF16)]),
        compiler_params=_cparams(("arbitrary", "arbitrary")),
        name="moe_hidden",
    )(block_e, n_used, xs, wg, wu)
    return pl.pallas_call(
        _moe_down_kernel,
        out_shape=jax.ShapeDtypeStruct((n, D_MODEL), F32),
        grid_spec=pltpu.PrefetchScalarGridSpec(
            num_scalar_prefetch=2,
            grid=(nb,),
            in_specs=[pl.BlockSpec((MOE_BM, EXPERT_HIDDEN), lambda b, be, nu: (blk(b, nu), 0)),
                      pl.BlockSpec((1, EXPERT_HIDDEN, D_MODEL), lambda b, be, nu: (be[blk(b, nu)], 0, 0))],
            out_specs=pl.BlockSpec((MOE_BM, D_MODEL), lambda b, be, nu: (b, 0)),
            scratch_shapes=[pltpu.VMEM((EXPERT_HIDDEN, D_MODEL), BF16)]),
        compiler_params=_cparams(("arbitrary",)),
        name="moe_down",
    )(block_e, n_used, hid, wd)


def _combine_kernel(rows, d_ref, nxt_ref, y_ref, x1_ref, w_ref, mod_ref, ln_ref, o_ref, buf, sem):
    i = pl.program_id(0)
    slot = i % 2

    def copy(iref, s, k, r):
        return pltpu.make_async_copy(y_ref.at[pl.ds(iref[0, k, r], 1), :],
                                     buf.at[s, k, pl.ds(r, 1), :], sem.at[s])

    def issue(iref, s):
        def body(r, c):
            copy(iref, s, 0, r).start()
            copy(iref, s, 1, r).start()
            return c
        lax.fori_loop(0, rows, body, 0, unroll=8)

    @pl.when(i == 0)
    def _():
        issue(d_ref, 0)

    @pl.when(i + 1 < pl.num_programs(0))
    def _():
        issue(nxt_ref, 1 - slot)

    def wait_body(r, c):
        copy(d_ref, slot, 0, r).wait()
        copy(d_ref, slot, 1, r).wait()
        return c

    lax.fori_loop(0, rows, wait_body, 0, unroll=8)

    def rows_step(ci, c):
        sl = pl.ds(pl.multiple_of(ci * LN_ROWS, LN_ROWS), LN_ROWS)
        f = w_ref[sl, 0:1] * buf[slot, 0, sl, :] + w_ref[sl, 1:2] * buf[slot, 1, sl, :]
        y = _layer_norm(DEEPNORM_ALPHA * x1_ref[sl, :] + mod_ref[0:1, :] * f)
        o_ref[sl, :] = y * ln_ref[0:1, :] + ln_ref[1:2, :]
        return c

    lax.fori_loop(0, rows // LN_ROWS, rows_step, 0)


def _combine_call(ys, dest, weights, x1, mod_row, ln2):
    t = x1.shape[0]
    rows = 128
    nblk = t // rows
    dest3 = jnp.transpose(dest.reshape(nblk, rows, 2), (0, 2, 1))
    return pl.pallas_call(
        functools.partial(_combine_kernel, rows),
        out_shape=jax.ShapeDtypeStruct((t, D_MODEL), F32),
        grid=(nblk,),
        in_specs=[pl.BlockSpec((1, 2, rows), lambda i: (i, 0, 0), memory_space=pltpu.SMEM),
                  pl.BlockSpec((1, 2, rows), lambda i: (jnp.minimum(i + 1, nblk - 1), 0, 0),
                               memory_space=pltpu.SMEM),
                  pl.BlockSpec(memory_space=pl.ANY),
                  pl.BlockSpec((rows, D_MODEL), lambda i: (i, 0)),
                  pl.BlockSpec((rows, 2), lambda i: (i, 0)),
                  pl.BlockSpec((SUBLANES, D_MODEL), lambda i: (0, 0)),
                  pl.BlockSpec((SUBLANES, D_MODEL), lambda i: (0, 0))],
        out_specs=pl.BlockSpec((rows, D_MODEL), lambda i: (i, 0)),
        scratch_shapes=[pltpu.VMEM((2, 2, rows, D_MODEL), F32), pltpu.SemaphoreType.DMA((2,))],
        compiler_params=_cparams(("arbitrary",)),
        name="moe_combine",
    )(dest3, dest3, ys, x1, weights, mod_row, ln2)


def _route(logits, b_group, b_expert):
    t = logits.shape[0]
    p_group = jax.nn.softmax(logits[:, :N_GROUPS] + b_group, axis=-1)
    group = jnp.argmax(p_group, axis=-1)
    gate_group = jnp.take_along_axis(p_group, group[:, None], axis=-1)
    le = (logits[:, N_GROUPS:N_GROUPS + N_EXPERTS] + b_expert).reshape(t, N_GROUPS, EXPERTS_PER_GROUP)
    le = jnp.take_along_axis(le, group[:, None, None], axis=1)[:, 0]
    top_p, top_i = lax.top_k(jax.nn.softmax(le, axis=-1), 2)
    weights = gate_group * top_p / jnp.sum(top_p, axis=-1, keepdims=True)
    expert_id = group[:, None] * EXPERTS_PER_GROUP + top_i
    return expert_id.astype(jnp.int32), weights


def _moe(tok, logits, b_group, b_expert, wg, wu, wd, x1, mod_row, ln2):
    t = tok.shape[0]
    expert_id, weights = _route(logits, b_group, b_expert)
    e_flat = expert_id.reshape(-1)
    n_assign = e_flat.shape[0]
    onehot = (e_flat[:, None] == jnp.arange(N_EXPERTS)[None, :]).astype(jnp.int32)
    rank = jnp.take_along_axis(jnp.cumsum(onehot, axis=0), e_flat[:, None], axis=1)[:, 0] - 1
    counts = jnp.sum(onehot, axis=0)
    padded = (counts + MOE_BM - 1) // MOE_BM * MOE_BM
    pad_end = jnp.cumsum(padded)
    dest = (pad_end - padded)[e_flat] + rank
    n_blocks = -(-n_assign // MOE_BM) + N_EXPERTS
    n_rows = n_blocks * MOE_BM
    src_tok = jnp.zeros((n_rows,), jnp.int32).at[dest].set(jnp.arange(n_assign, dtype=jnp.int32) // 2)
    block_e = jnp.minimum(jnp.searchsorted(pad_end, jnp.arange(n_blocks) * MOE_BM, side="right"),
                          N_EXPERTS - 1).astype(jnp.int32)
    n_used = (pad_end[-1] // MOE_BM).astype(jnp.int32).reshape(1)
    xs = _gather_call(tok, src_tok, n_used)
    ys = _expert_call(xs, wg, wu, wd, block_e, n_used)
    return _combine_call(ys, dest.reshape(t, 2).astype(jnp.int32), weights, x1, mod_row, ln2)


def _pad_rows(v, rows=SUBLANES):
    return jnp.pad(v, ((0, rows - v.shape[0]), (0, 0)))


def _layer(x, ctx, c, c_ctx, w_ada, b_ada, w_in, conv_qkv, a_log, dt_bias, gdn_norm_w, conv_b,
           w_branch_a, w_branch_b, w_out, ln1_g, ln1_b, w_router_group, b_router_group,
           w_router_expert, b_router_expert, w_exp_gate, w_exp_up, w_exp_down, ln2_g, ln2_b):
    t = x.shape[0]
    mod = _mod_call(_pad_rows(jnp.stack([c, c_ctx])), w_ada, b_ada.reshape(1, -1))
    mod_lat = mod[0].reshape(N_MOD, D_MODEL)
    mod_ctx = mod[1].reshape(N_MOD, D_MODEL)

    n_a = 4 * GDN_WIDTH
    n_ba = 4 * HEADS
    w_pa = w_in[:, :n_a].astype(BF16)
    w_pb = w_in[:, n_a + n_ba:].astype(BF16)
    w_pc = jnp.pad(w_in[:, n_a:n_a + n_ba], ((0, 0), (0, LANES - n_ba))).astype(BF16)
    lane_row = lambda v: jnp.pad(v.reshape(1, -1), ((0, 0), (2 * HEADS, LANES - 4 * HEADS)))
    alog_row, dtb_row = lane_row(a_log), lane_row(dt_bias)

    h_ctx = _ln_mod_call(ctx, mod_ctx[0:1], mod_ctx[1:2])
    tc = ctx.shape[0]
    qkv_c = _inproj_qkv(h_ctx, w_pa, conv_qkv, tc, tc)
    bg_c = _inproj_ba(h_ctx, w_pc, alog_row, dtb_row, tc)
    s0 = jnp.zeros((2, HEADS, HEAD_DIM, HEAD_DIM), F32)
    _, s_ctx = _gdn(qkv_c, bg_c, s0)

    tm = 512
    h = _ln_mod_call(x, mod_lat[0:1], mod_lat[1:2])
    qkv = _inproj_qkv(h, w_pa, conv_qkv, GRID_W, tm)
    zs = _inproj_z(h, w_pa, gdn_norm_w.reshape(1, HEAD_DIM), tm)
    yconv = _inproj_xbc(h, w_pb, conv_b, GRID_W, tm)
    gates = _inproj_gates(h, w_pb, tm)
    bg = _inproj_ba(h, w_pc, alog_row, dtb_row, tm)
    o, _ = _gdn(qkv, bg, s_ctx)
    m = _merge_call(o, zs, yconv, gates, w_branch_a.astype(BF16), w_branch_b.astype(BF16))

    w_router = jnp.pad(jnp.concatenate([w_router_group, w_router_expert], axis=1),
                       ((0, 0), (0, LANES - N_GROUPS - N_EXPERTS)))
    mix = _outmm_call(m, w_out.astype(BF16))
    x1, tok, logits = _post_mix_call(mix, x, _pad_rows(mod_lat[2:5]),
                                     _pad_rows(jnp.stack([ln1_g, ln1_b])), w_router)

    return _moe(tok, logits, b_router_group, b_router_expert, w_exp_gate, w_exp_up, w_exp_down,
                x1, _pad_rows(mod_lat[5:6]), _pad_rows(jnp.stack([ln2_g, ln2_b])))


def kernel(x, c, ctx, c_ctx, w_ada, b_ada, w_in, conv_qkv, a_log, dt_bias, gdn_norm_w, conv_b,
           w_branch_a, w_branch_b, w_out, ln1_g, ln1_b, w_router_group, b_router_group,
           w_router_expert, b_router_expert, w_exp_gate, w_exp_up, w_exp_down, ln2_g, ln2_b):
    assert x.shape[0] == 1 and w_ada.shape[0] == 1, "single batch element, single layer"
    out = _layer(x[0], ctx[0], c[0], c_ctx, w_ada[0], b_ada[0], w_in[0], conv_qkv[0],
                 a_log[0].reshape(-1), dt_bias[0].reshape(-1), gdn_norm_w[0], conv_b[0],
                 w_branch_a[0], w_branch_b[0], w_out[0], ln1_g[0], ln1_b[0],
                 w_router_group[0], b_router_group[0], w_router_expert[0], b_router_expert[0],
                 w_exp_gate[0], w_exp_up[0], w_exp_down[0], ln2_g[0], ln2_b[0])
    return out[None]
```

```python
import functools

import numpy as np
import jax
import jax.numpy as jnp
from jax import lax
from jax.experimental import pallas as pl
from jax.experimental.pallas import tpu as pltpu

F32 = jnp.float32
BF16 = jnp.bfloat16
HIGHEST = lax.Precision.HIGHEST

D_MODEL = 4096
GRID_W = 64
CHUNK = 64
HEADS = 16
HEAD_DIM = 128
GDN_WIDTH = HEADS * HEAD_DIM
CONV_WIDTH = D_MODEL // 2
N_GROUPS = 8
EXPERTS_PER_GROUP = 8
N_EXPERTS = 64
EXPERT_HIDDEN = 768
N_MOD = 6
DEEPNORM_ALPHA = 2.0 ** 0.25
LN_EPS = 1e-6
RMS_EPS = 1e-6

LANES = 128
SUBLANES = 8
UNITS = 128
VMEM_LIMIT = 52 * 1024 * 1024


def _cparams(sem, vmem=VMEM_LIMIT):
    return pltpu.CompilerParams(dimension_semantics=sem, vmem_limit_bytes=vmem)


def _silu(x):
    return x * jax.nn.sigmoid(x)


def _layer_norm(x):
    mu = jnp.mean(x, axis=-1, keepdims=True)
    xc = x - mu
    var = jnp.mean(xc * xc, axis=-1, keepdims=True)
    return xc * lax.rsqrt(var + LN_EPS)


def _mod_kernel(c_ref, w_ref, b_ref, o_ref):
    s = _silu(c_ref[...])
    o_ref[...] = jnp.dot(s, w_ref[...], preferred_element_type=F32, precision=HIGHEST) + b_ref[...]


def _mod_call(cs, w_ada, b_ada):
    n = w_ada.shape[1]
    tn = 512
    return pl.pallas_call(
        _mod_kernel,
        out_shape=jax.ShapeDtypeStruct((SUBLANES, n), F32),
        grid=(n // tn,),
        in_specs=[pl.BlockSpec((SUBLANES, D_MODEL), lambda j: (0, 0)),
                  pl.BlockSpec((D_MODEL, tn), lambda j: (0, j)),
                  pl.BlockSpec((1, tn), lambda j: (0, j))],
        out_specs=pl.BlockSpec((SUBLANES, tn), lambda j: (0, j)),
        compiler_params=_cparams(("arbitrary",)),
        name="mod",
    )(cs, w_ada, b_ada)


def _ln_mod_kernel(x_ref, shift_ref, scale_ref, o_ref):
    y = _layer_norm(x_ref[...])
    o_ref[...] = (y * (1.0 + scale_ref[...]) + shift_ref[...]).astype(o_ref.dtype)


def _ln_mod_call(x, shift, scale):
    t = x.shape[0]
    tr = min(256, t)
    return pl.pallas_call(
        _ln_mod_kernel,
        out_shape=jax.ShapeDtypeStruct((t, D_MODEL), BF16),
        grid=(t // tr,),
        in_specs=[pl.BlockSpec((tr, D_MODEL), lambda i: (i, 0)),
                  pl.BlockSpec((1, D_MODEL), lambda i: (0, 0)),
                  pl.BlockSpec((1, D_MODEL), lambda i: (0, 0))],
        out_specs=pl.BlockSpec((tr, D_MODEL), lambda i: (i, 0)),
        compiler_params=_cparams(("arbitrary",)),
        name="ln_mod",
    )(x, shift, scale)


COL_QKV = 0
COL_Z = 3 * GDN_WIDTH
COL_XB = 0
COL_BG = CONV_WIDTH
COL_CG = 2 * CONV_WIDTH
COL_GATES = 3 * CONV_WIDTH


def _conv3_rows(x, taps_ref, group):
    rows = x.shape[0]
    pos = lax.broadcasted_iota(jnp.int32, x.shape, 0) % group
    prev = jnp.where(pos == 0, 0.0, pltpu.roll(x, 1, 0))
    nxt = jnp.where(pos == group - 1, 0.0, pltpu.roll(x, rows - 1, 0))
    return taps_ref[0:1, :] * prev + taps_ref[1:2, :] * x + taps_ref[2:3, :] * nxt


def _ip_qkv_kernel(group, x_ref, w_ref, taps_ref, o_ref):
    acc = jnp.dot(x_ref[...], w_ref[...], preferred_element_type=F32)
    y = _silu(_conv3_rows(acc, taps_ref, group))
    j = pl.program_id(0)
    tiles_per_part = GDN_WIDTH // acc.shape[1]
    for hh in range(acc.shape[1] // HEAD_DIM):
        ys = y[:, hh * HEAD_DIM:(hh + 1) * HEAD_DIM]
        r = lax.rsqrt(jnp.sum(ys * ys, axis=-1, keepdims=True) + 1e-6)
        scale = jnp.where(j < tiles_per_part, r * HEAD_DIM ** -0.5,
                          jnp.where(j < 2 * tiles_per_part, r, 1.0))
        o_ref[hh] = (ys * scale).astype(o_ref.dtype)


def _ip_z_kernel(x_ref, w_ref, nw_ref, o_ref):
    acc = jnp.dot(x_ref[...], w_ref[...], preferred_element_type=F32)
    for hh in range(acc.shape[1] // HEAD_DIM):
        zs = acc[:, hh * HEAD_DIM:(hh + 1) * HEAD_DIM]
        o_ref[hh] = _silu(zs) * nw_ref[...]


def _ip_xbc_kernel(group, x_ref, wxb_ref, wbg_ref, wcg_ref, taps_ref, o_ref):
    x = x_ref[...]
    xb = jnp.dot(x, wxb_ref[...], preferred_element_type=F32)
    bg = jnp.dot(x, wbg_ref[...], preferred_element_type=F32)
    cg = jnp.dot(x, wcg_ref[...], preferred_element_type=F32)
    o_ref[...] = (bg * _conv3_rows(cg * xb, taps_ref, group)).astype(o_ref.dtype)


def _ip_gates_kernel(x_ref, w_ref, o_ref):
    acc = jnp.dot(x_ref[...], w_ref[...], preferred_element_type=F32)
    o_ref[...] = jax.nn.sigmoid(acc).astype(o_ref.dtype)


def _ip_ba_kernel(x_ref, w_ref, alog_ref, dtb_ref, o_ref):
    acc = jnp.dot(x_ref[...], w_ref[...], preferred_element_type=F32)
    lane = lax.broadcasted_iota(jnp.int32, acc.shape, 1)
    a = acc + dtb_ref[...]
    softplus = jnp.maximum(a, 0.0) + jnp.log(1.0 + jnp.exp(-jnp.abs(a)))
    o_ref[...] = jnp.where(lane < 2 * HEADS, jax.nn.sigmoid(acc), -jnp.exp(alog_ref[...]) * softplus)


def _cast_weights_kernel(inner, n_w, x_ref, *refs):
    w_refs, rest, w_scr = refs[:n_w], refs[n_w:len(refs) - n_w], refs[len(refs) - n_w:]

    @pl.when(pl.program_id(1) == 0)
    def _():
        for w_ref, s_ref in zip(w_refs, w_scr):
            s_ref[...] = w_ref[...].astype(BF16)

    inner(x_ref, *w_scr, *rest)


def _inproj(kernel, h, w_cat, col_starts, tn, n_tiles, extra, extra_specs, out_shape, out_spec, tm):
    t = h.shape[0]
    n_w = len(col_starts)
    w_specs = [pl.BlockSpec((D_MODEL, tn), functools.partial(lambda j, i, o: (0, o + j), o=c // tn))
               for c in col_starts]
    scratch = []
    if w_cat.dtype != BF16:
        kernel = functools.partial(_cast_weights_kernel, kernel, n_w)
        scratch = [pltpu.VMEM((D_MODEL, tn), BF16)] * n_w
    return pl.pallas_call(
        kernel,
        out_shape=out_shape,
        grid=(n_tiles, t // tm),
        in_specs=[pl.BlockSpec((tm, D_MODEL), lambda j, i: (i, 0))] + w_specs + extra_specs,
        out_specs=out_spec,
        scratch_shapes=scratch,
        compiler_params=_cparams(("arbitrary", "arbitrary")),
        name="inproj",
    )(h, *([w_cat] * n_w), *extra)


def _inproj_qkv(h, w_cat, taps, group, tm):
    t = h.shape[0]
    tn = 512
    hpt = tn // HEAD_DIM
    return _inproj(
        functools.partial(_ip_qkv_kernel, group), h, w_cat, [COL_QKV], tn, 3 * GDN_WIDTH // tn,
        [taps], [pl.BlockSpec((3, tn), lambda j, i: (0, j))],
        jax.ShapeDtypeStruct((3 * HEADS, t, HEAD_DIM), BF16),
        pl.BlockSpec((hpt, tm, HEAD_DIM), lambda j, i: (j, i, 0)), tm)


def _inproj_z(h, w_cat, norm_w, tm):
    t = h.shape[0]
    tn = 512
    hpt = tn // HEAD_DIM
    return _inproj(
        _ip_z_kernel, h, w_cat, [COL_Z], tn, GDN_WIDTH // tn,
        [norm_w], [pl.BlockSpec((1, HEAD_DIM), lambda j, i: (0, 0))],
        jax.ShapeDtypeStruct((HEADS, t, HEAD_DIM), F32),
        pl.BlockSpec((hpt, tm, HEAD_DIM), lambda j, i: (j, i, 0)), tm)


def _inproj_xbc(h, w_cat, taps, group, tm):
    t = h.shape[0]
    tn = 512
    return _inproj(
        functools.partial(_ip_xbc_kernel, group), h, w_cat, [COL_XB, COL_BG, COL_CG], tn,
        CONV_WIDTH // tn, [taps], [pl.BlockSpec((3, tn), lambda j, i: (0, j))],
        jax.ShapeDtypeStruct((t, CONV_WIDTH), BF16),
        pl.BlockSpec((tm, tn), lambda j, i: (i, j)), tm)


def _inproj_gates(h, w_cat, tm):
    t = h.shape[0]
    tn = 1024
    return _inproj(
        _ip_gates_kernel, h, w_cat, [COL_GATES], tn, 2 * D_MODEL // tn, [], [],
        jax.ShapeDtypeStruct((t, 2 * D_MODEL), BF16),
        pl.BlockSpec((tm, tn), lambda j, i: (i, j)), tm)


def _inproj_ba(h, w_cat, alog_row, dtb_row, tm):
    t = h.shape[0]
    return _inproj(
        _ip_ba_kernel, h, w_cat, [4 * GDN_WIDTH], LANES, 1, [alog_row, dtb_row],
        [pl.BlockSpec((1, LANES), lambda j, i: (0, 0))] * 2,
        jax.ShapeDtypeStruct((t, LANES), F32),
        pl.BlockSpec((tm, LANES), lambda j, i: (i, 0)), tm)


def _gdn_scal_kernel(g_ref, gc_ref, ep_ref, egl_ref):
    d = pl.program_id(0)
    r = lax.broadcasted_iota(jnp.int32, (CHUNK, CHUNK), 0)
    c = lax.broadcasted_iota(jnp.int32, (CHUNK, CHUNK), 1)
    tri = jnp.where(d == 0, (c <= r).astype(F32), (c >= r).astype(F32))
    gc = jnp.dot(tri, g_ref[0], preferred_element_type=F32, precision=HIGHEST)
    gl = jnp.where(d == 0, gc[CHUNK - 1:CHUNK, :], gc[0:1, :])
    gc_ref[0] = gc
    ep_ref[0] = jnp.exp(gl - gc)
    egl_ref[0] = jnp.exp(gl)


def _gdn_scal_call(g_s):
    u = g_s.shape[2]
    tu = 512 if u % 512 == 0 else UNITS
    spec = pl.BlockSpec((1, CHUNK, tu), lambda d, i: (d, 0, i))
    return pl.pallas_call(
        _gdn_scal_kernel,
        out_shape=(jax.ShapeDtypeStruct(g_s.shape, F32), jax.ShapeDtypeStruct(g_s.shape, F32),
                   jax.ShapeDtypeStruct((2, 1, u), F32)),
        grid=(2, u // tu),
        in_specs=[spec],
        out_specs=(spec, spec, pl.BlockSpec((1, 1, tu), lambda d, i: (d, 0, i))),
        compiler_params=_cparams(("arbitrary", "arbitrary")),
        name="gdn_scal",
    )(g_s)


def _nt_dot(a, b):
    return lax.dot_general(a, b, (((1,), (1,)), ((), ())), preferred_element_type=F32)


def _gdn_gram_kernel(cb, q_ref, k_ref, g_ref, qk_ref, kt_ref):
    r = lax.broadcasted_iota(jnp.int32, (HEAD_DIM, HEAD_DIM), 0)
    c = lax.broadcasted_iota(jnp.int32, (HEAD_DIM, HEAD_DIM), 1)
    eye = (r == c).astype(BF16)
    for ci in range(cb):
        k = k_ref[0, ci * CHUNK:(ci + 1) * CHUNK, :]
        q = q_ref[0, ci * CHUNK:(ci + 1) * CHUNK, :]
        g_ref[ci] = _nt_dot(k, k)
        qk_ref[ci] = _nt_dot(q, k)
        kt_ref[ci] = _nt_dot(eye, k).astype(BF16)


def _gdn_gram_call(qkv, nc):
    cb = min(32, nc)
    ncb = nc // cb
    u = HEADS * nc
    return pl.pallas_call(
        functools.partial(_gdn_gram_kernel, cb),
        out_shape=(jax.ShapeDtypeStruct((u, CHUNK, CHUNK), F32),
                   jax.ShapeDtypeStruct((u, CHUNK, CHUNK), F32),
                   jax.ShapeDtypeStruct((u, HEAD_DIM, CHUNK), BF16)),
        grid=(HEADS, ncb),
        in_specs=[pl.BlockSpec((1, cb * CHUNK, HEAD_DIM), lambda h, c: (h, c, 0)),
                  pl.BlockSpec((1, cb * CHUNK, HEAD_DIM), lambda h, c: (HEADS + h, c, 0))],
        out_specs=(pl.BlockSpec((cb, CHUNK, CHUNK), lambda h, c: (h * ncb + c, 0, 0)),
                   pl.BlockSpec((cb, CHUNK, CHUNK), lambda h, c: (h * ncb + c, 0, 0)),
                   pl.BlockSpec((cb, HEAD_DIM, CHUNK), lambda h, c: (h * ncb + c, 0, 0))),
        compiler_params=_cparams(("arbitrary", "arbitrary")),
        name="gdn_gram",
    )(qkv, qkv)


NB = CHUNK // SUBLANES


def _row_bcast(ref, row):
    return jnp.broadcast_to(ref[pl.ds(row, 1), :], (SUBLANES, UNITS))


def _gdn_inv_kernel(g_ref, qk_ref, beta_ref, gc_ref, twtu_ref, atde_ref, *scratch):
    for bwd in (False, True):
        @pl.when(pl.program_id(0) == int(bwd))
        def _(bwd=bwd):
            _gdn_inv_body(bwd, g_ref, qk_ref, beta_ref, gc_ref, twtu_ref.at[0], atde_ref.at[0], *scratch)


def _gdn_inv_body(bwd, g_ref, qk_ref, beta_ref, gc_ref, twtu_ref, atde_ref,
                  g_s, qk_s, l_s, t_s, tw_s, tu_s, at_s, de_s, e_s):
    pos = (lambda a: CHUNK - 1 - a) if bwd else (lambda a: a)
    blk = (lambda b: NB - 1 - b) if bwd else (lambda b: b)

    for pp in range(CHUNK // 2):
        sl = slice(pp * LANES, (pp + 1) * LANES)
        g_s[sl, :] = g_ref[:, sl].T
        qk_s[sl, :] = qk_ref[:, sl].T
    e_s[...] = jnp.exp(gc_ref[0])

    @pl.when(pl.program_id(1) == 0)
    def _():
        for ref in (tw_s, tu_s, at_s, de_s):
            ref[...] = jnp.zeros(ref.shape, F32)

    sub = lax.broadcasted_iota(jnp.int32, (SUBLANES, UNITS), 0)
    zero = jnp.zeros((SUBLANES, UNITS), F32)
    rs = range(SUBLANES)
    own = [(SUBLANES - 1 - r) if bwd else r for r in rs]
    earlier = [(sub > own[r]) if bwd else (sub < own[r]) for r in rs]

    def tile_ds(p, b):
        return pl.ds(pl.multiple_of(p * CHUNK + b * SUBLANES, SUBLANES), SUBLANES)

    def cols_ds(b):
        return pl.ds(pl.multiple_of(b * SUBLANES, SUBLANES), SUBLANES)

    def row_block(ib, carry):
        b_own = blk(ib)
        ps = [pos(ib * SUBLANES + r) for r in rs]
        gc_p = [_row_bcast(gc_ref.at[0], p) for p in ps]
        beta_p = [_row_bcast(beta_ref.at[0], p) for p in ps]

        def weights_offdiag(bc, c):
            b = blk(bc)
            gc_c = gc_ref[0, cols_ds(b), :]
            for r in rs:
                dec = jnp.exp(gc_p[r] - gc_c)
                l_s[tile_ds(ps[r], b), :] = beta_p[r] * g_s[tile_ds(ps[r], b), :] * dec
                at_s[tile_ds(ps[r], b), :] = qk_s[tile_ds(ps[r], b), :] * dec
            return c

        lax.fori_loop(0, ib, weights_offdiag, 0)
        gc_c = gc_ref[0, cols_ds(b_own), :]
        for r in rs:
            t = tile_ds(ps[r], b_own)
            dec = jnp.exp(gc_p[r] - gc_c)
            l_s[t, :] = jnp.where(earlier[r], beta_p[r] * g_s[t, :] * dec, 0.0)
            at_s[t, :] = jnp.where(sub == own[r], qk_s[t, :], jnp.where(earlier[r], qk_s[t, :] * dec, 0.0))
            de_s[t, :] = jnp.where(sub == own[r], _row_bcast(e_s, ps[r]), 0.0)

        def finish(b, acc):
            done = []
            for r in rs:
                a_r = acc[r]
                for kk in range(r):
                    a_r = a_r - _row_bcast(l_s, ps[r] * CHUNK + ps[kk]) * done[kk]
                done.append(a_r)
            beta_c = beta_ref[0, cols_ds(b), :]
            be_c = beta_c * e_s[cols_ds(b), :]
            for r in rs:
                t_s[tile_ds(ps[r], b), :] = done[r]
                tu_s[tile_ds(ps[r], b), :] = done[r] * beta_c
                tw_s[tile_ds(ps[r], b), :] = done[r] * be_c

        def subst_offdiag(bc, c):
            b = blk(bc)

            def k_block(kb, acc):
                acc = list(acc)
                for kk in rs:
                    pk = pos(kb * SUBLANES + kk)
                    t_k = t_s[tile_ds(pk, b), :]
                    for r in rs:
                        acc[r] = acc[r] - _row_bcast(l_s, ps[r] * CHUNK + pk) * t_k
                return tuple(acc)

            finish(b, lax.fori_loop(bc, ib, k_block, (zero,) * SUBLANES))
            return c

        lax.fori_loop(0, ib, subst_offdiag, 0)
        finish(b_own, [jnp.where(sub == own[r], 1.0, 0.0) for r in rs])
        return carry

    lax.fori_loop(0, NB, row_block, 0)

    for p in range(CHUNK):
        rows = slice(p * CHUNK, (p + 1) * CHUNK)
        sl = slice(p * LANES, (p + 1) * LANES)
        twtu_ref[:, sl] = jnp.concatenate([tw_s[rows, :], tu_s[rows, :]], axis=0).T.astype(BF16)
        atde_ref[:, sl] = jnp.concatenate([at_s[rows, :], de_s[rows, :]], axis=0).T.astype(BF16)


def _gdn_inv_call(g_flat, qk_flat, beta_s, gc_s):
    u = g_flat.shape[0]
    mat = pl.BlockSpec((UNITS, CHUNK * CHUNK), lambda d, i: (i, 0))
    sc = pl.BlockSpec((1, CHUNK, UNITS), lambda d, i: (d, 0, i))
    out = pl.BlockSpec((1, UNITS, CHUNK * LANES), lambda d, i: (d, i, 0))
    soa = pltpu.VMEM((CHUNK * CHUNK, UNITS), F32)
    return pl.pallas_call(
        _gdn_inv_kernel,
        out_shape=(jax.ShapeDtypeStruct((2, u, CHUNK * LANES), BF16),
                   jax.ShapeDtypeStruct((2, u, CHUNK * LANES), BF16)),
        grid=(2, u // UNITS),
        in_specs=[mat, mat, sc, sc],
        out_specs=(out, out),
        scratch_shapes=[soa] * 8 + [pltpu.VMEM((CHUNK, UNITS), F32)],
        compiler_params=_cparams(("arbitrary", "arbitrary")),
        name="gdn_inv",
    )(g_flat, qk_flat, beta_s, gc_s)


def _gdn_scan_kernel(hb, cb, egl_ref, q_ref, k_ref, v_ref, kt_ref, twtu_ref, atde_ref, ep_ref, s0_ref,
                     o_ref, sfin_ref, s_scr, w_scr, u_scr):
    d = pl.program_id(0)
    hg = pl.program_id(1)
    ci = pl.program_id(2)
    ncb = pl.num_programs(2)

    @pl.when(ci == 0)
    def _():
        s_scr[...] = s0_ref[0]

    cblk = ci + d * (ncb - 1 - 2 * ci)
    zeros = jnp.zeros((CHUNK, HEAD_DIM), BF16)

    def prepare(c, carry):
        r0 = pl.multiple_of(c * CHUNK, CHUNK)
        for hh in range(hb):
            k = k_ref[hh, pl.ds(r0, CHUNK), :]
            v = v_ref[hh, pl.ds(r0, CHUNK), :]
            rhs = jnp.concatenate([jnp.concatenate([k, zeros], axis=1),
                                   jnp.concatenate([zeros, v], axis=1)], axis=0)
            wu = jnp.dot(twtu_ref[0, hh, c], rhs, preferred_element_type=F32)
            w_scr[c, hh] = wu[:, :HEAD_DIM].astype(BF16)
            u_scr[c, hh] = wu[:, HEAD_DIM:]
        return carry

    lax.fori_loop(0, cb, prepare, 0)

    def chunk_step(cc, carry):
        c = cc + d * (cb - 1 - 2 * cc)
        r0 = pl.multiple_of(c * CHUNK, CHUNK)
        heads = range(hb)
        s = [s_scr[hh] for hh in heads]
        x = [jnp.dot(jnp.concatenate([w_scr[c, hh], q_ref[hh, pl.ds(r0, CHUNK), :]], axis=0),
                     s[hh].astype(BF16), preferred_element_type=F32) for hh in heads]
        v_new = [u_scr[c, hh] - x[hh][:CHUNK] for hh in heads]
        for hh in heads:
            kdt = (kt_ref[hh, c].astype(F32) * ep_ref[0, hh, pl.ds(c, 1), :]).astype(BF16)
            egl = egl_ref[d, hg * hb + hh, cblk * cb + c]
            s_scr[hh] = egl * s[hh] + jnp.dot(kdt, v_new[hh].astype(BF16), preferred_element_type=F32)
        for hh in heads:
            rhs = jnp.concatenate([v_new[hh], x[hh][CHUNK:]], axis=0).astype(BF16)
            o_ref[0, hh, pl.ds(r0, CHUNK), :] = jnp.dot(atde_ref[0, hh, c], rhs,
                                                        preferred_element_type=F32).astype(o_ref.dtype)
        return carry

    lax.fori_loop(0, cb, chunk_step, 0)

    @pl.when(ci == ncb - 1)
    def _():
        sfin_ref[0] = s_scr[...]


def _gdn_scan_call(qkv, kt, twtu, atde, ep, egl, s0, nc):
    t = qkv.shape[1]
    hb = HEADS
    cb = min(8, nc)
    ncb = nc // cb
    hgs = HEADS // hb
    nat = lambda d, c: c + d * (ncb - 1 - 2 * c)
    tok = lambda part: pl.BlockSpec((hb, cb * CHUNK, HEAD_DIM),
                                    lambda d, hg, c, egl, part=part: (part * hgs + hg, nat(d, c), 0))
    per_dir = lambda last: pl.BlockSpec((1, hb, cb, CHUNK, last),
                                        lambda d, hg, c, egl: (d, hg, nat(d, c), 0, 0))
    grid_spec = pltpu.PrefetchScalarGridSpec(
        num_scalar_prefetch=1,
        grid=(2, hgs, ncb),
        in_specs=[tok(0), tok(1), tok(2),
                  pl.BlockSpec((hb, cb, HEAD_DIM, CHUNK), lambda d, hg, c, egl: (hg, nat(d, c), 0, 0)),
                  per_dir(LANES), per_dir(LANES),
                  pl.BlockSpec((1, hb, cb, CHUNK), lambda d, hg, c, egl: (d, hg, nat(d, c), 0)),
                  pl.BlockSpec((1, hb, HEAD_DIM, HEAD_DIM), lambda d, hg, c, egl: (d, hg, 0, 0))],
        out_specs=(pl.BlockSpec((1, hb, cb * CHUNK, HEAD_DIM), lambda d, hg, c, egl: (d, hg, nat(d, c), 0)),
                   pl.BlockSpec((1, hb, HEAD_DIM, HEAD_DIM), lambda d, hg, c, egl: (d, hg, 0, 0))),
        scratch_shapes=[pltpu.VMEM((hb, HEAD_DIM, HEAD_DIM), F32),
                        pltpu.VMEM((cb, hb, CHUNK, HEAD_DIM), BF16),
                        pltpu.VMEM((cb, hb, CHUNK, HEAD_DIM), F32)],
    )
    return pl.pallas_call(
        functools.partial(_gdn_scan_kernel, hb, cb),
        out_shape=(jax.ShapeDtypeStruct((2, HEADS, t, HEAD_DIM), BF16),
                   jax.ShapeDtypeStruct((2, HEADS, HEAD_DIM, HEAD_DIM), F32)),
        grid_spec=grid_spec,
        compiler_params=_cparams(("arbitrary", "arbitrary", "arbitrary")),
        name="gdn_scan",
    )(egl, qkv, qkv, qkv, kt.reshape(HEADS, nc, HEAD_DIM, CHUNK),
      twtu.reshape(2, HEADS, nc, CHUNK, LANES), atde.reshape(2, HEADS, nc, CHUNK, LANES), ep, s0)


def _gdn(qkv, bg, s0):
    t = qkv.shape[1]
    nc = t // CHUNK
    u = HEADS * nc
    up = -(-u // UNITS) * UNITS
    to_soa = lambda a: jnp.pad(jnp.transpose(a.reshape(nc, CHUNK, 2, HEADS), (2, 1, 3, 0)).reshape(2, CHUNK, u),
                               ((0, 0), (0, 0), (0, up - u)))
    beta_s = to_soa(bg[:, :2 * HEADS])
    g_s = to_soa(bg[:, 2 * HEADS:4 * HEADS])
    gc_s, ep_s, egl_s = _gdn_scal_call(g_s)
    gram, qk, kt = _gdn_gram_call(qkv, nc)
    pad_u = lambda a: jnp.pad(a.reshape(u, -1), ((0, up - u), (0, 0)))
    g_flat, qk_flat = pad_u(gram), pad_u(qk)
    twtu, atde = (a[:, :u] for a in _gdn_inv_call(g_flat, qk_flat, beta_s, gc_s))
    ep = jnp.transpose(ep_s[:, :, :u].reshape(2, CHUNK, HEADS, nc), (0, 2, 3, 1))
    egl = egl_s[:, 0, :u].reshape(2, HEADS, nc)
    return _gdn_scan_call(qkv, kt, twtu, atde, ep, egl, s0, nc)


def _merge_kernel(o_ref, zs_ref, yc_ref, ga_ref, gb_ref, wa_ref, wb_ref, m_ref, yg_s):
    @pl.when(pl.program_id(1) == 0)
    def _():
        for hh in range(HEADS):
            o = o_ref[0, hh].astype(F32) + o_ref[1, hh].astype(F32)
            y = o * lax.rsqrt(jnp.mean(o * o, axis=-1, keepdims=True) + RMS_EPS) * zs_ref[hh]
            yg_s[:, hh * HEAD_DIM:(hh + 1) * HEAD_DIM] = y.astype(BF16)

    pa = jnp.dot(yg_s[...], wa_ref[...], preferred_element_type=F32)
    pb = jnp.dot(yc_ref[...], wb_ref[...], preferred_element_type=F32)
    m_ref[...] = (ga_ref[...].astype(F32) * pa + gb_ref[...].astype(F32) * pb).astype(m_ref.dtype)


def _merge_call(o, zs, yconv, gates, wa, wb):
    t = yconv.shape[0]
    tm, tn = 512, 1024
    nj = D_MODEL // tn
    return pl.pallas_call(
        _merge_kernel,
        out_shape=jax.ShapeDtypeStruct((t, D_MODEL), BF16),
        grid=(t // tm, nj),
        in_specs=[pl.BlockSpec((2, HEADS, tm, HEAD_DIM), lambda i, j: (0, 0, i, 0)),
                  pl.BlockSpec((HEADS, tm, HEAD_DIM), lambda i, j: (0, i, 0)),
                  pl.BlockSpec((tm, CONV_WIDTH), lambda i, j: (i, 0)),
                  pl.BlockSpec((tm, tn), lambda i, j: (i, j)),
                  pl.BlockSpec((tm, tn), lambda i, j: (i, nj + j)),
                  pl.BlockSpec((GDN_WIDTH, tn), lambda i, j: (0, j)),
                  pl.BlockSpec((CONV_WIDTH, tn), lambda i, j: (0, j))],
        out_specs=pl.BlockSpec((tm, tn), lambda i, j: (i, j)),
        scratch_shapes=[pltpu.VMEM((tm, GDN_WIDTH), BF16)],
        compiler_params=_cparams(("arbitrary", "arbitrary")),
        name="merge",
    )(o, zs, yconv, gates, gates, wa, wb)


LN_ROWS = 64


def _outmm_kernel(m_ref, w_ref, o_ref):
    o_ref[...] = jnp.dot(m_ref[...], w_ref[...], preferred_element_type=F32)


def _outmm_call(m, w_out):
    t = m.shape[0]
    tm, tn = 512, 1024
    return pl.pallas_call(
        _outmm_kernel,
        out_shape=jax.ShapeDtypeStruct((t, D_MODEL), F32),
        grid=(D_MODEL // tn, t // tm),
        in_specs=[pl.BlockSpec((tm, D_MODEL), lambda j, i: (i, 0)),
                  pl.BlockSpec((D_MODEL, tn), lambda j, i: (0, j))],
        out_specs=pl.BlockSpec((tm, tn), lambda j, i: (i, j)),
        compiler_params=_cparams(("arbitrary", "arbitrary")),
        name="out_proj",
    )(m, w_out)


HALF = D_MODEL // 2


def _bf16_bits(v):
    return pltpu.bitcast(v.astype(BF16).astype(F32), jnp.uint32)


def _post_mix_kernel(mix_ref, x_ref, mod_ref, ln_ref, wrh_ref, wrl_ref, x1_ref, tokp_ref, logit_ref, tok_s):
    def rows_step(ci, c):
        sl = pl.ds(pl.multiple_of(ci * LN_ROWS, LN_ROWS), LN_ROWS)
        x1 = _layer_norm(DEEPNORM_ALPHA * x_ref[sl, :] + mod_ref[0:1, :] * mix_ref[sl, :])
        x1 = x1 * ln_ref[0:1, :] + ln_ref[1:2, :]
        x1_ref[sl, :] = x1
        tok = _layer_norm(x1) * (1.0 + mod_ref[2:3, :]) + mod_ref[1:2, :]
        tok_s[sl, :] = tok
        tokp_ref[sl, :] = _bf16_bits(tok[:, :HALF]) | (_bf16_bits(tok[:, HALF:]) >> 16)
        return c

    lax.fori_loop(0, x_ref.shape[0] // LN_ROWS, rows_step, 0)
    tok = tok_s[...]
    t_hi = tok.astype(BF16)
    t_lo = (tok - t_hi.astype(F32)).astype(BF16)
    logit_ref[...] = (jnp.dot(t_hi, wrh_ref[...], preferred_element_type=F32)
                      + jnp.dot(t_lo, wrh_ref[...], preferred_element_type=F32)
                      + jnp.dot(t_hi, wrl_ref[...], preferred_element_type=F32))


def _post_mix_call(mix, x, mod3, ln1, w_router):
    wr_hi = w_router.astype(BF16)
    wr_lo = (w_router - wr_hi.astype(F32)).astype(BF16)
    t = x.shape[0]
    tm = 256
    row = pl.BlockSpec((tm, D_MODEL), lambda i: (i, 0))
    vec = pl.BlockSpec((SUBLANES, D_MODEL), lambda i: (0, 0))
    wr = pl.BlockSpec((D_MODEL, LANES), lambda i: (0, 0))
    return pl.pallas_call(
        _post_mix_kernel,
        out_shape=(jax.ShapeDtypeStruct((t, D_MODEL), F32),
                   jax.ShapeDtypeStruct((t, HALF), jnp.uint32),
                   jax.ShapeDtypeStruct((t, LANES), F32)),
        grid=(t // tm,),
        in_specs=[row, row, vec, vec, wr, wr],
        out_specs=(row, pl.BlockSpec((tm, HALF), lambda i: (i, 0)), pl.BlockSpec((tm, LANES), lambda i: (i, 0))),
        scratch_shapes=[pltpu.VMEM((tm, D_MODEL), F32)],
        compiler_params=_cparams(("arbitrary",)),
        name="post_mix",
    )(mix, x, mod3, ln1, wr_hi, wr_lo)


MOE_BM = 256
MOE_HC = 256


def _gather_kernel(rows, nu_ref, idx_ref, nxt_ref, src_ref, dst_ref, buf, sem):
    i = pl.program_id(0)
    nu = nu_ref[0]
    slot = i % 2

    def copy(iref, s, r):
        return pltpu.make_async_copy(src_ref.at[pl.ds(iref[0, 0, r], 1), :],
                                     buf.at[s, pl.ds(r, 1), :], sem.at[s])

    def issue(iref, s):
        def body(r, c):
            copy(iref, s, r).start()
            return c
        lax.fori_loop(0, rows, body, 0, unroll=8)

    @pl.when(i == 0)
    def _():
        issue(idx_ref, 0)

    @pl.when(i + 1 < nu)
    def _():
        issue(nxt_ref, 1 - slot)

    @pl.when(i < nu)
    def _():
        def body(r, c):
            copy(idx_ref, slot, r).wait()
            return c
        lax.fori_loop(0, rows, body, 0, unroll=8)
        packed = buf[slot]
        dst_ref[:, :HALF] = pltpu.bitcast(packed & jnp.uint32(0xFFFF0000), F32).astype(BF16)
        dst_ref[:, HALF:] = pltpu.bitcast(packed << 16, F32).astype(BF16)

    @pl.when(i >= nu)
    def _():
        dst_ref[...] = jnp.zeros(dst_ref.shape, dst_ref.dtype)


def _gather_call(src, idx, n_used):
    n = idx.shape[0]
    rows = MOE_BM
    nblk = n // rows
    idx3 = idx.reshape(nblk, 1, rows)
    grid_spec = pltpu.PrefetchScalarGridSpec(
        num_scalar_prefetch=1,
        grid=(nblk,),
        in_specs=[pl.BlockSpec((1, 1, rows), lambda i, nu: (i, 0, 0), memory_space=pltpu.SMEM),
                  pl.BlockSpec((1, 1, rows), lambda i, nu: (jnp.minimum(i + 1, nblk - 1), 0, 0),
                               memory_space=pltpu.SMEM),
                  pl.BlockSpec(memory_space=pl.ANY)],
        out_specs=pl.BlockSpec((rows, D_MODEL), lambda i, nu: (i, 0)),
        scratch_shapes=[pltpu.VMEM((2, rows, src.shape[1]), src.dtype), pltpu.SemaphoreType.DMA((2,))],
    )
    return pl.pallas_call(
        functools.partial(_gather_kernel, rows),
        out_shape=jax.ShapeDtypeStruct((n, D_MODEL), BF16),
        grid_spec=grid_spec,
        compiler_params=_cparams(("arbitrary",)),
        name="moe_gather",
    )(n_used, idx3, idx3, src)


def _expert_changed(be_ref, nu_ref, b):
    bb = jnp.minimum(b, nu_ref[0] - 1)
    return (b == 0) | (be_ref[bb] != be_ref[jnp.maximum(bb - 1, 0)])


def _moe_hidden_kernel(be_ref, nu_ref, x_ref, wg_ref, wu_ref, h_ref, w_s):
    b = pl.program_id(0)
    hc = pl.program_id(1)
    live = b < nu_ref[0]

    @pl.when(live & _expert_changed(be_ref, nu_ref, b))
    def _():
        w_s[hc, :, :MOE_HC] = wg_ref[0].astype(BF16)
        w_s[hc, :, MOE_HC:] = wu_ref[0].astype(BF16)

    @pl.when(live)
    def _():
        gu = jnp.dot(x_ref[...], w_s[hc], preferred_element_type=F32)
        h_ref[...] = (_silu(gu[:, :MOE_HC]) * gu[:, MOE_HC:]).astype(h_ref.dtype)

    @pl.when(jnp.logical_not(live))
    def _():
        h_ref[...] = jnp.zeros(h_ref.shape, h_ref.dtype)


def _moe_down_kernel(be_ref, nu_ref, h_ref, wd_ref, y_ref, wd_s):
    b = pl.program_id(0)
    live = b < nu_ref[0]

    @pl.when(live & _expert_changed(be_ref, nu_ref, b))
    def _():
        wd_s[...] = wd_ref[0].astype(BF16)

    @pl.when(live)
    def _():
        y_ref[...] = jnp.dot(h_ref[...], wd_s[...], preferred_element_type=F32)

    @pl.when(jnp.logical_not(live))
    def _():
        y_ref[...] = jnp.zeros(y_ref.shape, F32)


def _expert_call(xs, wg, wu, wd, block_e, n_used):
    n = xs.shape[0]
    nb = n // MOE_BM
    nh = EXPERT_HIDDEN // MOE_HC
    blk = lambda b, nu: jnp.minimum(b, nu[0] - 1)

    def w_chunk(b, h, be, nu):
        bb = blk(b, nu)
        first = (b < nu[0]) & ((bb == 0) | (be[bb] != be[jnp.maximum(bb - 1, 0)]))
        return be[bb], 0, jnp.where(first, h, nh - 1)

    hid = pl.pallas_call(
        _moe_hidden_kernel,
        out_shape=jax.ShapeDtypeStruct((n, EXPERT_HIDDEN), BF16),
        grid_spec=pltpu.PrefetchScalarGridSpec(
            num_scalar_prefetch=2,
            grid=(nb, nh),
            in_specs=[pl.BlockSpec((MOE_BM, D_MODEL), lambda b, h, be, nu: (blk(b, nu), 0)),
                      pl.BlockSpec((1, D_MODEL, MOE_HC), w_chunk),
                      pl.BlockSpec((1, D_MODEL, MOE_HC), w_chunk)],
            out_specs=pl.BlockSpec((MOE_BM, MOE_HC), lambda b, h, be, nu: (b, h)),
            scratch_shapes=[pltpu.VMEM((nh, D_MODEL, 2 * MOE_HC), BF16)]),
        compiler_params=_cparams(("arbitrary", "arbitrary")),
        name="moe_hidden",
    )(block_e, n_used, xs, wg, wu)
    return pl.pallas_call(
        _moe_down_kernel,
        out_shape=jax.ShapeDtypeStruct((n, D_MODEL), F32),
        grid_spec=pltpu.PrefetchScalarGridSpec(
            num_scalar_prefetch=2,
            grid=(nb,),
            in_specs=[pl.BlockSpec((MOE_BM, EXPERT_HIDDEN), lambda b, be, nu: (blk(b, nu), 0)),
                      pl.BlockSpec((1, EXPERT_HIDDEN, D_MODEL), lambda b, be, nu: (be[blk(b, nu)], 0, 0))],
            out_specs=pl.BlockSpec((MOE_BM, D_MODEL), lambda b, be, nu: (b, 0)),
            scratch_shapes=[pltpu.VMEM((EXPERT_HIDDEN, D_MODEL), BF16)]),
        compiler_params=_cparams(("arbitrary",)),
        name="moe_down",
    )(block_e, n_used, hid, wd)


def _combine_kernel(rows, d_ref, nxt_ref, y_ref, x1_ref, w_ref, mod_ref, ln_ref, o_ref, buf, sem):
    i = pl.program_id(0)
    slot = i % 2

    def copy(iref, s, k, r):
        return pltpu.make_async_copy(y_ref.at[pl.ds(iref[0, k, r], 1), :],
                                     buf.at[s, k, pl.ds(r, 1), :], sem.at[s])

    def issue(iref, s):
        def body(r, c):
            copy(iref, s, 0, r).start()
            copy(iref, s, 1, r).start()
            return c
        lax.fori_loop(0, rows, body, 0, unroll=8)

    @pl.when(i == 0)
    def _():
        issue(d_ref, 0)

    @pl.when(i + 1 < pl.num_programs(0))
    def _():
        issue(nxt_ref, 1 - slot)

    def wait_body(r, c):
        copy(d_ref, slot, 0, r).wait()
        copy(d_ref, slot, 1, r).wait()
        return c

    lax.fori_loop(0, rows, wait_body, 0, unroll=8)

    def rows_step(ci, c):
        sl = pl.ds(pl.multiple_of(ci * LN_ROWS, LN_ROWS), LN_ROWS)
        f = w_ref[sl, 0:1] * buf[slot, 0, sl, :] + w_ref[sl, 1:2] * buf[slot, 1, sl, :]
        y = _layer_norm(DEEPNORM_ALPHA * x1_ref[sl, :] + mod_ref[0:1, :] * f)
        o_ref[sl, :] = y * ln_ref[0:1, :] + ln_ref[1:2, :]
        return c

    lax.fori_loop(0, rows // LN_ROWS, rows_step, 0)


def _combine_call(ys, dest, weights, x1, mod_row, ln2):
    t = x1.shape[0]
    rows = 128
    nblk = t // rows
    dest3 = jnp.transpose(dest.reshape(nblk, rows, 2), (0, 2, 1))
    return pl.pallas_call(
        functools.partial(_combine_kernel, rows),
        out_shape=jax.ShapeDtypeStruct((t, D_MODEL), F32),
        grid=(nblk,),
        in_specs=[pl.BlockSpec((1, 2, rows), lambda i: (i, 0, 0), memory_space=pltpu.SMEM),
                  pl.BlockSpec((1, 2, rows), lambda i: (jnp.minimum(i + 1, nblk - 1), 0, 0),
                               memory_space=pltpu.SMEM),
                  pl.BlockSpec(memory_space=pl.ANY),
                  pl.BlockSpec((rows, D_MODEL), lambda i: (i, 0)),
                  pl.BlockSpec((rows, 2), lambda i: (i, 0)),
                  pl.BlockSpec((SUBLANES, D_MODEL), lambda i: (0, 0)),
                  pl.BlockSpec((SUBLANES, D_MODEL), lambda i: (0, 0))],
        out_specs=pl.BlockSpec((rows, D_MODEL), lambda i: (i, 0)),
        scratch_shapes=[pltpu.VMEM((2, 2, rows, D_MODEL), F32), pltpu.SemaphoreType.DMA((2,))],
        compiler_params=_cparams(("arbitrary",)),
        name="moe_combine",
    )(dest3, dest3, ys, x1, weights, mod_row, ln2)


def _route(logits, b_group, b_expert):
    t = logits.shape[0]
    p_group = jax.nn.softmax(logits[:, :N_GROUPS] + b_group, axis=-1)
    group = jnp.argmax(p_group, axis=-1)
    gate_group = jnp.take_along_axis(p_group, group[:, None], axis=-1)
    le = (logits[:, N_GROUPS:N_GROUPS + N_EXPERTS] + b_expert).reshape(t, N_GROUPS, EXPERTS_PER_GROUP)
    le = jnp.take_along_axis(le, group[:, None, None], axis=1)[:, 0]
    top_p, top_i = lax.top_k(jax.nn.softmax(le, axis=-1), 2)
    weights = gate_group * top_p / jnp.sum(top_p, axis=-1, keepdims=True)
    expert_id = group[:, None] * EXPERTS_PER_GROUP + top_i
    return expert_id.astype(jnp.int32), weights


def _moe(tok, logits, b_group, b_expert, wg, wu, wd, x1, mod_row, ln2):
    t = tok.shape[0]
    expert_id, weights = _route(logits, b_group, b_expert)
    e_flat = expert_id.reshape(-1)
    n_assign = e_flat.shape[0]
    onehot = (e_flat[:, None] == jnp.arange(N_EXPERTS)[None, :]).astype(jnp.int32)
    rank = jnp.take_along_axis(jnp.cumsum(onehot, axis=0), e_flat[:, None], axis=1)[:, 0] - 1
    counts = jnp.sum(onehot, axis=0)
    padded = (counts + MOE_BM - 1) // MOE_BM * MOE_BM
    pad_end = jnp.cumsum(padded)
    dest = (pad_end - padded)[e_flat] + rank
    n_blocks = -(-n_assign // MOE_BM) + N_EXPERTS
    n_rows = n_blocks * MOE_BM
    src_tok = jnp.zeros((n_rows,), jnp.int32).at[dest].set(jnp.arange(n_assign, dtype=jnp.int32) // 2)
    block_e = jnp.minimum(jnp.searchsorted(pad_end, jnp.arange(n_blocks) * MOE_BM, side="right"),
                          N_EXPERTS - 1).astype(jnp.int32)
    n_used = (pad_end[-1] // MOE_BM).astype(jnp.int32).reshape(1)
    xs = _gather_call(tok, src_tok, n_used)
    ys = _expert_call(xs, wg, wu, wd, block_e, n_used)
    return _combine_call(ys, dest.reshape(t, 2).astype(jnp.int32), weights, x1, mod_row, ln2)


def _pad_rows(v, rows=SUBLANES):
    return jnp.pad(v, ((0, rows - v.shape[0]), (0, 0)))


def _layer(x, ctx, c, c_ctx, w_ada, b_ada, w_in, conv_qkv, a_log, dt_bias, gdn_norm_w, conv_b,
           w_branch_a, w_branch_b, w_out, ln1_g, ln1_b, w_router_group, b_router_group,
           w_router_expert, b_router_expert, w_exp_gate, w_exp_up, w_exp_down, ln2_g, ln2_b):
    t = x.shape[0]
    mod = _mod_call(_pad_rows(jnp.stack([c, c_ctx])), w_ada, b_ada.reshape(1, -1))
    mod_lat = mod[0].reshape(N_MOD, D_MODEL)
    mod_ctx = mod[1].reshape(N_MOD, D_MODEL)

    w_pa = w_pc = w_in
    w_pb = w_in[:, 4 * GDN_WIDTH + 4 * HEADS:].astype(BF16)
    lane_row = lambda v: jnp.pad(v.reshape(1, -1), ((0, 0), (2 * HEADS, LANES - 4 * HEADS)))
    alog_row, dtb_row = lane_row(a_log), lane_row(dt_bias)

    h_ctx = _ln_mod_call(ctx, mod_ctx[0:1], mod_ctx[1:2])
    tc = ctx.shape[0]
    qkv_c = _inproj_qkv(h_ctx, w_pa, conv_qkv, tc, tc)
    bg_c = _inproj_ba(h_ctx, w_pc, alog_row, dtb_row, tc)
    s0 = jnp.zeros((2, HEADS, HEAD_DIM, HEAD_DIM), F32)
    _, s_ctx = _gdn(qkv_c, bg_c, s0)

    tm = 512
    h = _ln_mod_call(x, mod_lat[0:1], mod_lat[1:2])
    qkv = _inproj_qkv(h, w_pa, conv_qkv, GRID_W, tm)
    zs = _inproj_z(h, w_pa, gdn_norm_w.reshape(1, HEAD_DIM), tm)
    yconv = _inproj_xbc(h, w_pb, conv_b, GRID_W, tm)
    gates = _inproj_gates(h, w_pb, tm)
    bg = _inproj_ba(h, w_pc, alog_row, dtb_row, tm)
    o, _ = _gdn(qkv, bg, s_ctx)
    m = _merge_call(o, zs, yconv, gates, w_branch_a.astype(BF16), w_branch_b.astype(BF16))

    w_router = jnp.pad(jnp.concatenate([w_router_group, w_router_expert], axis=1),
                       ((0, 0), (0, LANES - N_GROUPS - N_EXPERTS)))
    mix = _outmm_call(m, w_out.astype(BF16))
    x1, tok, logits = _post_mix_call(mix, x, _pad_rows(mod_lat[2:5]),
                                     _pad_rows(jnp.stack([ln1_g, ln1_b])), w_router)

    return _moe(tok, logits, b_router_group, b_router_expert, w_exp_gate, w_exp_up, w_exp_down,
                x1, _pad_rows(mod_lat[5:6]), _pad_rows(jnp.stack([ln2_g, ln2_b])))


def kernel(x, c, ctx, c_ctx, w_ada, b_ada, w_in, conv_qkv, a_log, dt_bias, gdn_norm_w, conv_b,
           w_branch_a, w_branch_b, w_out, ln1_g, ln1_b, w_router_group, b_router_group,
           w_router_expert, b_router_expert, w_exp_gate, w_exp_up, w_exp_down, ln2_g, ln2_b):
    assert x.shape[0] == 1 and w_ada.shape[0] == 1, "single batch element, single layer"
    out = _layer(x[0], ctx[0], c[0], c_ctx, w_ada[0], b_ada[0], w_in[0], conv_qkv[0],
                 a_log[0].reshape(-1), dt_bias[0].reshape(-1), gdn_norm_w[0], conv_b[0],
                 w_branch_a[0], w_branch_b[0], w_out[0], ln1_g[0], ln1_b[0],
                 w_router_group[0], b_router_group[0], w_router_expert[0], b_router_expert[0],
                 w_exp_gate[0], w_exp_up[0], w_exp_down[0], ln2_g[0], ln2_b[0])
    return out[None]
```

```python
import functools

import numpy as np
import jax
import jax.numpy as jnp
from jax import lax
from jax.experimental import pallas as pl
from jax.experimental.pallas import tpu as pltpu

F32 = jnp.float32
BF16 = jnp.bfloat16
HIGHEST = lax.Precision.HIGHEST

D_MODEL = 4096
GRID_W = 64
CHUNK = 64
HEADS = 16
HEAD_DIM = 128
GDN_WIDTH = HEADS * HEAD_DIM
CONV_WIDTH = D_MODEL // 2
N_GROUPS = 8
EXPERTS_PER_GROUP = 8
N_EXPERTS = 64
EXPERT_HIDDEN = 768
N_MOD = 6
DEEPNORM_ALPHA = 2.0 ** 0.25
LN_EPS = 1e-6
RMS_EPS = 1e-6

LANES = 128
SUBLANES = 8
UNITS = 128
VMEM_LIMIT = 52 * 1024 * 1024


def _cparams(sem, vmem=VMEM_LIMIT):
    return pltpu.CompilerParams(dimension_semantics=sem, vmem_limit_bytes=vmem)


def _silu(x):
    return x * jax.nn.sigmoid(x)


def _layer_norm(x):
    mu = jnp.mean(x, axis=-1, keepdims=True)
    xc = x - mu
    var = jnp.mean(xc * xc, axis=-1, keepdims=True)
    return xc * lax.rsqrt(var + LN_EPS)


def _mod_kernel(c_ref, w_ref, b_ref, o_ref):
    s = _silu(c_ref[...])
    o_ref[...] = jnp.dot(s, w_ref[...], preferred_element_type=F32, precision=HIGHEST) + b_ref[...]


def _mod_call(cs, w_ada, b_ada):
    n = w_ada.shape[1]
    tn = 512
    return pl.pallas_call(
        _mod_kernel,
        out_shape=jax.ShapeDtypeStruct((SUBLANES, n), F32),
        grid=(n // tn,),
        in_specs=[pl.BlockSpec((SUBLANES, D_MODEL), lambda j: (0, 0)),
                  pl.BlockSpec((D_MODEL, tn), lambda j: (0, j)),
                  pl.BlockSpec((1, tn), lambda j: (0, j))],
        out_specs=pl.BlockSpec((SUBLANES, tn), lambda j: (0, j)),
        compiler_params=_cparams(("arbitrary",)),
        name="mod",
    )(cs, w_ada, b_ada)


def _ln_mod_kernel(x_ref, shift_ref, scale_ref, o_ref):
    y = _layer_norm(x_ref[...])
    o_ref[...] = (y * (1.0 + scale_ref[...]) + shift_ref[...]).astype(o_ref.dtype)


def _ln_mod_call(x, shift, scale):
    t = x.shape[0]
    tr = min(256, t)
    return pl.pallas_call(
        _ln_mod_kernel,
        out_shape=jax.ShapeDtypeStruct((t, D_MODEL), BF16),
        grid=(t // tr,),
        in_specs=[pl.BlockSpec((tr, D_MODEL), lambda i: (i, 0)),
                  pl.BlockSpec((1, D_MODEL), lambda i: (0, 0)),
                  pl.BlockSpec((1, D_MODEL), lambda i: (0, 0))],
        out_specs=pl.BlockSpec((tr, D_MODEL), lambda i: (i, 0)),
        compiler_params=_cparams(("arbitrary",)),
        name="ln_mod",
    )(x, shift, scale)


COL_QKV = 0
COL_Z = 3 * GDN_WIDTH
COL_XB = 0
COL_BG = CONV_WIDTH
COL_CG = 2 * CONV_WIDTH
COL_GATES = 3 * CONV_WIDTH


def _conv3_rows(x, taps_ref, group):
    rows = x.shape[0]
    pos = lax.broadcasted_iota(jnp.int32, x.shape, 0) % group
    prev = jnp.where(pos == 0, 0.0, pltpu.roll(x, 1, 0))
    nxt = jnp.where(pos == group - 1, 0.0, pltpu.roll(x, rows - 1, 0))
    return taps_ref[0:1, :] * prev + taps_ref[1:2, :] * x + taps_ref[2:3, :] * nxt


QKV_SUB = 2 * HEAD_DIM


def _ip_qkv_kernel(group, x_ref, w_ref, taps_ref, o_ref):
    x = x_ref[...]
    j = pl.program_id(0)
    tn = w_ref.shape[1]
    tiles_per_part = GDN_WIDTH // tn
    for cb in range(tn // QKV_SUB):
        cols = slice(cb * QKV_SUB, (cb + 1) * QKV_SUB)
        acc = jnp.dot(x, w_ref[:, cols], preferred_element_type=F32)
        y = _silu(_conv3_rows(acc, taps_ref.at[:, cols], group))
        for hh in range(QKV_SUB // HEAD_DIM):
            ys = y[:, hh * HEAD_DIM:(hh + 1) * HEAD_DIM]
            r = lax.rsqrt(jnp.sum(ys * ys, axis=-1, keepdims=True) + 1e-6)
            scale = jnp.where(j < tiles_per_part, r * HEAD_DIM ** -0.5,
                              jnp.where(j < 2 * tiles_per_part, r, 1.0))
            o_ref[cb * (QKV_SUB // HEAD_DIM) + hh] = (ys * scale).astype(o_ref.dtype)


def _ip_z_kernel(x_ref, w_ref, nw_ref, o_ref):
    acc = jnp.dot(x_ref[...], w_ref[...], preferred_element_type=F32)
    for hh in range(acc.shape[1] // HEAD_DIM):
        zs = acc[:, hh * HEAD_DIM:(hh + 1) * HEAD_DIM]
        o_ref[hh] = _silu(zs) * nw_ref[...]


def _ip_xbc_kernel(group, x_ref, wxb_ref, wbg_ref, wcg_ref, taps_ref, o_ref):
    x = x_ref[...]
    xb = jnp.dot(x, wxb_ref[...], preferred_element_type=F32)
    bg = jnp.dot(x, wbg_ref[...], preferred_element_type=F32)
    cg = jnp.dot(x, wcg_ref[...], preferred_element_type=F32)
    o_ref[...] = (bg * _conv3_rows(cg * xb, taps_ref, group)).astype(o_ref.dtype)


def _ip_gates_kernel(x_ref, w_ref, o_ref):
    acc = jnp.dot(x_ref[...], w_ref[...], preferred_element_type=F32)
    o_ref[...] = jax.nn.sigmoid(acc).astype(o_ref.dtype)


def _ip_ba_kernel(x_ref, w_ref, alog_ref, dtb_ref, o_ref):
    acc = jnp.dot(x_ref[...], w_ref[...], preferred_element_type=F32)
    lane = lax.broadcasted_iota(jnp.int32, acc.shape, 1)
    a = acc + dtb_ref[...]
    softplus = jnp.maximum(a, 0.0) + jnp.log(1.0 + jnp.exp(-jnp.abs(a)))
    o_ref[...] = jnp.where(lane < 2 * HEADS, jax.nn.sigmoid(acc), -jnp.exp(alog_ref[...]) * softplus)


def _wcopy_kernel(shift, a_ref, b_ref, o_ref):
    if shift:
        o_ref[...] = jnp.concatenate([a_ref[:, shift:], b_ref[:, :shift]], axis=1).astype(o_ref.dtype)
    else:
        o_ref[...] = a_ref[...].astype(o_ref.dtype)


def _wcopy_call(w, col0, ncols):
    rows = w.shape[0]
    tr, tn = 512, 1024
    shift = col0 % LANES
    base = (col0 - shift) // tn
    nxt = tn // LANES
    assert (col0 - shift) % tn == 0 and ncols % tn == 0
    return pl.pallas_call(
        functools.partial(_wcopy_kernel, shift),
        out_shape=jax.ShapeDtypeStruct((rows, ncols), BF16),
        grid=(rows // tr, ncols // tn),
        in_specs=[pl.BlockSpec((tr, tn), lambda i, j: (i, base + j)),
                  pl.BlockSpec((tr, LANES), lambda i, j: (i, (base + j + 1) * nxt))],
        out_specs=pl.BlockSpec((tr, tn), lambda i, j: (i, j)),
        compiler_params=_cparams(("arbitrary", "arbitrary")),
        name="w_copy",
    )(w, w)


def _cast_weights_kernel(inner, n_w, x_ref, *refs):
    w_refs, rest, w_scr = refs[:n_w], refs[n_w:len(refs) - n_w], refs[len(refs) - n_w:]

    @pl.when(pl.program_id(1) == 0)
    def _():
        for w_ref, s_ref in zip(w_refs, w_scr):
            s_ref[...] = w_ref[...].astype(BF16)

    inner(x_ref, *w_scr, *rest)


def _inproj(kernel, h, w_cat, col_starts, tn, n_tiles, extra, extra_specs, out_shape, out_spec, tm):
    t = h.shape[0]
    n_w = len(col_starts)
    w_specs = [pl.BlockSpec((D_MODEL, tn), functools.partial(lambda j, i, o: (0, o + j), o=c // tn))
               for c in col_starts]
    scratch = []
    if w_cat.dtype != BF16:
        kernel = functools.partial(_cast_weights_kernel, kernel, n_w)
        scratch = [pltpu.VMEM((D_MODEL, tn), BF16)] * n_w
    return pl.pallas_call(
        kernel,
        out_shape=out_shape,
        grid=(n_tiles, t // tm),
        in_specs=[pl.BlockSpec((tm, D_MODEL), lambda j, i: (i, 0))] + w_specs + extra_specs,
        out_specs=out_spec,
        scratch_shapes=scratch,
        compiler_params=_cparams(("arbitrary", "arbitrary")),
        name="inproj",
    )(h, *([w_cat] * n_w), *extra)


def _inproj_qkv(h, w_cat, taps, group, tm):
    t = h.shape[0]
    tn = 1024
    hpt = tn // HEAD_DIM
    return _inproj(
        functools.partial(_ip_qkv_kernel, group), h, w_cat, [COL_QKV], tn, 3 * GDN_WIDTH // tn,
        [taps], [pl.BlockSpec((3, tn), lambda j, i: (0, j))],
        jax.ShapeDtypeStruct((3 * HEADS, t, HEAD_DIM), BF16),
        pl.BlockSpec((hpt, tm, HEAD_DIM), lambda j, i: (j, i, 0)), tm)


def _inproj_z(h, w_cat, norm_w, tm):
    t = h.shape[0]
    tn = 1024
    hpt = tn // HEAD_DIM
    return _inproj(
        _ip_z_kernel, h, w_cat, [COL_Z], tn, GDN_WIDTH // tn,
        [norm_w], [pl.BlockSpec((1, HEAD_DIM), lambda j, i: (0, 0))],
        jax.ShapeDtypeStruct((HEADS, t, HEAD_DIM), F32),
        pl.BlockSpec((hpt, tm, HEAD_DIM), lambda j, i: (j, i, 0)), tm)


def _inproj_xbc(h, w_cat, taps, group, tm):
    t = h.shape[0]
    tn = 512
    return _inproj(
        functools.partial(_ip_xbc_kernel, group), h, w_cat, [COL_XB, COL_BG, COL_CG], tn,
        CONV_WIDTH // tn, [taps], [pl.BlockSpec((3, tn), lambda j, i: (0, j))],
        jax.ShapeDtypeStruct((t, CONV_WIDTH), BF16),
        pl.BlockSpec((tm, tn), lambda j, i: (i, j)), tm)


def _inproj_gates(h, w_cat, tm):
    t = h.shape[0]
    tn = 1024
    return _inproj(
        _ip_gates_kernel, h, w_cat, [COL_GATES], tn, 2 * D_MODEL // tn, [], [],
        jax.ShapeDtypeStruct((t, 2 * D_MODEL), BF16),
        pl.BlockSpec((tm, tn), lambda j, i: (i, j)), tm)


def _inproj_ba(h, w_cat, alog_row, dtb_row, tm):
    t = h.shape[0]
    return _inproj(
        _ip_ba_kernel, h, w_cat, [4 * GDN_WIDTH], LANES, 1, [alog_row, dtb_row],
        [pl.BlockSpec((1, LANES), lambda j, i: (0, 0))] * 2,
        jax.ShapeDtypeStruct((t, LANES), F32),
        pl.BlockSpec((tm, LANES), lambda j, i: (i, 0)), tm)


def _gdn_scal_kernel(g_ref, gc_ref, ep_ref, egl_ref):
    d = pl.program_id(0)
    r = lax.broadcasted_iota(jnp.int32, (CHUNK, CHUNK), 0)
    c = lax.broadcasted_iota(jnp.int32, (CHUNK, CHUNK), 1)
    tri = jnp.where(d == 0, (c <= r).astype(F32), (c >= r).astype(F32))
    gc = jnp.dot(tri, g_ref[0], preferred_element_type=F32, precision=HIGHEST)
    gl = jnp.where(d == 0, gc[CHUNK - 1:CHUNK, :], gc[0:1, :])
    gc_ref[0] = gc
    ep_ref[0] = jnp.exp(gl - gc)
    egl_ref[0] = jnp.exp(gl)


def _gdn_scal_call(g_s):
    u = g_s.shape[2]
    tu = 512 if u % 512 == 0 else UNITS
    spec = pl.BlockSpec((1, CHUNK, tu), lambda d, i: (d, 0, i))
    return pl.pallas_call(
        _gdn_scal_kernel,
        out_shape=(jax.ShapeDtypeStruct(g_s.shape, F32), jax.ShapeDtypeStruct(g_s.shape, F32),
                   jax.ShapeDtypeStruct((2, 1, u), F32)),
        grid=(2, u // tu),
        in_specs=[spec],
        out_specs=(spec, spec, pl.BlockSpec((1, 1, tu), lambda d, i: (d, 0, i))),
        compiler_params=_cparams(("arbitrary", "arbitrary")),
        name="gdn_scal",
    )(g_s)


def _nt_dot(a, b):
    return lax.dot_general(a, b, (((1,), (1,)), ((), ())), preferred_element_type=F32)


def _gdn_gram_kernel(cb, q_ref, k_ref, g_ref, qk_ref, kt_ref):
    r = lax.broadcasted_iota(jnp.int32, (HEAD_DIM, HEAD_DIM), 0)
    c = lax.broadcasted_iota(jnp.int32, (HEAD_DIM, HEAD_DIM), 1)
    eye = (r == c).astype(BF16)
    for ci in range(cb):
        k = k_ref[0, ci * CHUNK:(ci + 1) * CHUNK, :]
        q = q_ref[0, ci * CHUNK:(ci + 1) * CHUNK, :]
        g_ref[ci] = _nt_dot(k, k)
        qk_ref[ci] = _nt_dot(q, k)
        kt_ref[ci] = _nt_dot(eye, k).astype(BF16)


def _gdn_gram_call(qkv, nc):
    cb = min(32, nc)
    ncb = nc // cb
    u = HEADS * nc
    return pl.pallas_call(
        functools.partial(_gdn_gram_kernel, cb),
        out_shape=(jax.ShapeDtypeStruct((u, CHUNK, CHUNK), F32),
                   jax.ShapeDtypeStruct((u, CHUNK, CHUNK), F32),
                   jax.ShapeDtypeStruct((u, HEAD_DIM, CHUNK), BF16)),
        grid=(HEADS, ncb),
        in_specs=[pl.BlockSpec((1, cb * CHUNK, HEAD_DIM), lambda h, c: (h, c, 0)),
                  pl.BlockSpec((1, cb * CHUNK, HEAD_DIM), lambda h, c: (HEADS + h, c, 0))],
        out_specs=(pl.BlockSpec((cb, CHUNK, CHUNK), lambda h, c: (h * ncb + c, 0, 0)),
                   pl.BlockSpec((cb, CHUNK, CHUNK), lambda h, c: (h * ncb + c, 0, 0)),
                   pl.BlockSpec((cb, HEAD_DIM, CHUNK), lambda h, c: (h * ncb + c, 0, 0))),
        compiler_params=_cparams(("arbitrary", "arbitrary")),
        name="gdn_gram",
    )(qkv, qkv)


NB = CHUNK // SUBLANES


def _row_bcast(ref, row):
    return jnp.broadcast_to(ref[pl.ds(row, 1), :], (SUBLANES, UNITS))


def _gdn_inv_kernel(g_ref, qk_ref, beta_ref, gc_ref, twtu_ref, atde_ref, *scratch):
    for bwd in (False, True):
        @pl.when(pl.program_id(0) == int(bwd))
        def _(bwd=bwd):
            _gdn_inv_body(bwd, g_ref, qk_ref, beta_ref, gc_ref, twtu_ref.at[0], atde_ref.at[0], *scratch)


def _gdn_inv_body(bwd, g_ref, qk_ref, beta_ref, gc_ref, twtu_ref, atde_ref,
                  g_s, qk_s, l_s, t_s, tw_s, tu_s, at_s, de_s, e_s):
    pos = (lambda a: CHUNK - 1 - a) if bwd else (lambda a: a)
    blk = (lambda b: NB - 1 - b) if bwd else (lambda b: b)

    for pp in range(CHUNK // 2):
        sl = slice(pp * LANES, (pp + 1) * LANES)
        g_s[sl, :] = g_ref[:, sl].T
        qk_s[sl, :] = qk_ref[:, sl].T
    e_s[...] = jnp.exp(gc_ref[0])

    @pl.when(pl.program_id(1) == 0)
    def _():
        for ref in (tw_s, tu_s, at_s, de_s):
            ref[...] = jnp.zeros(ref.shape, F32)

    sub = lax.broadcasted_iota(jnp.int32, (SUBLANES, UNITS), 0)
    zero = jnp.zeros((SUBLANES, UNITS), F32)
    rs = range(SUBLANES)
    own = [(SUBLANES - 1 - r) if bwd else r for r in rs]
    earlier = [(sub > own[r]) if bwd else (sub < own[r]) for r in rs]

    def tile_ds(p, b):
        return pl.ds(pl.multiple_of(p * CHUNK + b * SUBLANES, SUBLANES), SUBLANES)

    def cols_ds(b):
        return pl.ds(pl.multiple_of(b * SUBLANES, SUBLANES), SUBLANES)

    def row_block(ib, carry):
        b_own = blk(ib)
        ps = [pos(ib * SUBLANES + r) for r in rs]
        gc_p = [_row_bcast(gc_ref.at[0], p) for p in ps]
        beta_p = [_row_bcast(beta_ref.at[0], p) for p in ps]

        def weights_offdiag(bc, c):
            b = blk(bc)
            gc_c = gc_ref[0, cols_ds(b), :]
            for r in rs:
                dec = jnp.exp(gc_p[r] - gc_c)
                l_s[tile_ds(ps[r], b), :] = beta_p[r] * g_s[tile_ds(ps[r], b), :] * dec
                at_s[tile_ds(ps[r], b), :] = qk_s[tile_ds(ps[r], b), :] * dec
            return c

        lax.fori_loop(0, ib, weights_offdiag, 0)
        gc_c = gc_ref[0, cols_ds(b_own), :]
        for r in rs:
            t = tile_ds(ps[r], b_own)
            dec = jnp.exp(gc_p[r] - gc_c)
            l_s[t, :] = jnp.where(earlier[r], beta_p[r] * g_s[t, :] * dec, 0.0)
            at_s[t, :] = jnp.where(sub == own[r], qk_s[t, :], jnp.where(earlier[r], qk_s[t, :] * dec, 0.0))
            de_s[t, :] = jnp.where(sub == own[r], _row_bcast(e_s, ps[r]), 0.0)

        def finish(b, acc):
            done = []
            for r in rs:
                a_r = acc[r]
                for kk in range(r):
                    a_r = a_r - _row_bcast(l_s, ps[r] * CHUNK + ps[kk]) * done[kk]
                done.append(a_r)
            beta_c = beta_ref[0, cols_ds(b), :]
            be_c = beta_c * e_s[cols_ds(b), :]
            for r in rs:
                t_s[tile_ds(ps[r], b), :] = done[r]
                tu_s[tile_ds(ps[r], b), :] = done[r] * beta_c
                tw_s[tile_ds(ps[r], b), :] = done[r] * be_c

        def subst_offdiag(bc, c):
            b = blk(bc)

            def k_block(kb, acc):
                acc = list(acc)
                for kk in rs:
                    pk = pos(kb * SUBLANES + kk)
                    t_k = t_s[tile_ds(pk, b), :]
                    for r in rs:
                        acc[r] = acc[r] - _row_bcast(l_s, ps[r] * CHUNK + pk) * t_k
                return tuple(acc)

            finish(b, lax.fori_loop(bc, ib, k_block, (zero,) * SUBLANES))
            return c

        lax.fori_loop(0, ib, subst_offdiag, 0)
        finish(b_own, [jnp.where(sub == own[r], 1.0, 0.0) for r in rs])
        return carry

    lax.fori_loop(0, NB, row_block, 0)

    for p in range(CHUNK):
        rows = slice(p * CHUNK, (p + 1) * CHUNK)
        sl = slice(p * LANES, (p + 1) * LANES)
        twtu_ref[:, sl] = jnp.concatenate([tw_s[rows, :], tu_s[rows, :]], axis=0).T.astype(BF16)
        atde_ref[:, sl] = jnp.concatenate([at_s[rows, :], de_s[rows, :]], axis=0).T.astype(BF16)


def _gdn_inv_call(g_flat, qk_flat, beta_s, gc_s):
    u = g_flat.shape[0]
    mat = pl.BlockSpec((UNITS, CHUNK * CHUNK), lambda d, i: (i, 0))
    sc = pl.BlockSpec((1, CHUNK, UNITS), lambda d, i: (d, 0, i))
    out = pl.BlockSpec((1, UNITS, CHUNK * LANES), lambda d, i: (d, i, 0))
    soa = pltpu.VMEM((CHUNK * CHUNK, UNITS), F32)
    return pl.pallas_call(
        _gdn_inv_kernel,
        out_shape=(jax.ShapeDtypeStruct((2, u, CHUNK * LANES), BF16),
                   jax.ShapeDtypeStruct((2, u, CHUNK * LANES), BF16)),
        grid=(2, u // UNITS),
        in_specs=[mat, mat, sc, sc],
        out_specs=(out, out),
        scratch_shapes=[soa] * 8 + [pltpu.VMEM((CHUNK, UNITS), F32)],
        compiler_params=_cparams(("arbitrary", "arbitrary")),
        name="gdn_inv",
    )(g_flat, qk_flat, beta_s, gc_s)


def _gdn_scan_kernel(hb, cb, egl_ref, q_ref, k_ref, v_ref, kt_ref, twtu_ref, atde_ref, ep_ref, s0_ref,
                     o_ref, sfin_ref, s_scr, w_scr, u_scr):
    d = pl.program_id(0)
    hg = pl.program_id(1)
    ci = pl.program_id(2)
    ncb = pl.num_programs(2)

    @pl.when(ci == 0)
    def _():
        s_scr[...] = s0_ref[0]

    cblk = ci + d * (ncb - 1 - 2 * ci)
    zeros = jnp.zeros((CHUNK, HEAD_DIM), BF16)

    def prepare(c, carry):
        r0 = pl.multiple_of(c * CHUNK, CHUNK)
        for hh in range(hb):
            k = k_ref[hh, pl.ds(r0, CHUNK), :]
            v = v_ref[hh, pl.ds(r0, CHUNK), :]
            rhs = jnp.concatenate([jnp.concatenate([k, zeros], axis=1),
                                   jnp.concatenate([zeros, v], axis=1)], axis=0)
            wu = jnp.dot(twtu_ref[0, hh, c], rhs, preferred_element_type=F32)
            w_scr[c, hh] = wu[:, :HEAD_DIM].astype(BF16)
            u_scr[c, hh] = wu[:, HEAD_DIM:]
        return carry

    lax.fori_loop(0, cb, prepare, 0)

    def chunk_step(cc, carry):
        c = cc + d * (cb - 1 - 2 * cc)
        r0 = pl.multiple_of(c * CHUNK, CHUNK)
        heads = range(hb)
        s = [s_scr[hh] for hh in heads]
        x = [jnp.dot(jnp.concatenate([w_scr[c, hh], q_ref[hh, pl.ds(r0, CHUNK), :]], axis=0),
                     s[hh].astype(BF16), preferred_element_type=F32) for hh in heads]
        v_new = [u_scr[c, hh] - x[hh][:CHUNK] for hh in heads]
        for hh in heads:
            kdt = (kt_ref[hh, c].astype(F32) * ep_ref[0, hh, pl.ds(c, 1), :]).astype(BF16)
            egl = egl_ref[d, hg * hb + hh, cblk * cb + c]
            s_scr[hh] = egl * s[hh] + jnp.dot(kdt, v_new[hh].astype(BF16), preferred_element_type=F32)
        for hh in heads:
            rhs = jnp.concatenate([v_new[hh], x[hh][CHUNK:]], axis=0).astype(BF16)
            o_ref[0, hh, pl.ds(r0, CHUNK), :] = jnp.dot(atde_ref[0, hh, c], rhs,
                                                        preferred_element_type=F32).astype(o_ref.dtype)
        return carry

    lax.fori_loop(0, cb, chunk_step, 0)

    @pl.when(ci == ncb - 1)
    def _():
        sfin_ref[0] = s_scr[...]


def _gdn_scan_call(qkv, kt, twtu, atde, ep, egl, s0, nc):
    t = qkv.shape[1]
    hb = HEADS
    cb = min(8, nc)
    ncb = nc // cb
    hgs = HEADS // hb
    nat = lambda d, c: c + d * (ncb - 1 - 2 * c)
    tok = lambda part: pl.BlockSpec((hb, cb * CHUNK, HEAD_DIM),
                                    lambda d, hg, c, egl, part=part: (part * hgs + hg, nat(d, c), 0))
    per_dir = lambda last: pl.BlockSpec((1, hb, cb, CHUNK, last),
                                        lambda d, hg, c, egl: (d, hg, nat(d, c), 0, 0))
    grid_spec = pltpu.PrefetchScalarGridSpec(
        num_scalar_prefetch=1,
        grid=(2, hgs, ncb),
        in_specs=[tok(0), tok(1), tok(2),
                  pl.BlockSpec((hb, cb, HEAD_DIM, CHUNK), lambda d, hg, c, egl: (hg, nat(d, c), 0, 0)),
                  per_dir(LANES), per_dir(LANES),
                  pl.BlockSpec((1, hb, cb, CHUNK), lambda d, hg, c, egl: (d, hg, nat(d, c), 0)),
                  pl.BlockSpec((1, hb, HEAD_DIM, HEAD_DIM), lambda d, hg, c, egl: (d, hg, 0, 0))],
        out_specs=(pl.BlockSpec((1, hb, cb * CHUNK, HEAD_DIM), lambda d, hg, c, egl: (d, hg, nat(d, c), 0)),
                   pl.BlockSpec((1, hb, HEAD_DIM, HEAD_DIM), lambda d, hg, c, egl: (d, hg, 0, 0))),
        scratch_shapes=[pltpu.VMEM((hb, HEAD_DIM, HEAD_DIM), F32),
                        pltpu.VMEM((cb, hb, CHUNK, HEAD_DIM), BF16),
                        pltpu.VMEM((cb, hb, CHUNK, HEAD_DIM), F32)],
    )
    return pl.pallas_call(
        functools.partial(_gdn_scan_kernel, hb, cb),
        out_shape=(jax.ShapeDtypeStruct((2, HEADS, t, HEAD_DIM), BF16),
                   jax.ShapeDtypeStruct((2, HEADS, HEAD_DIM, HEAD_DIM), F32)),
        grid_spec=grid_spec,
        compiler_params=_cparams(("arbitrary", "arbitrary", "arbitrary")),
        name="gdn_scan",
    )(egl, qkv, qkv, qkv, kt.reshape(HEADS, nc, HEAD_DIM, CHUNK),
      twtu.reshape(2, HEADS, nc, CHUNK, LANES), atde.reshape(2, HEADS, nc, CHUNK, LANES), ep, s0)


def _gdn(qkv, bg, s0):
    t = qkv.shape[1]
    nc = t // CHUNK
    u = HEADS * nc
    up = -(-u // UNITS) * UNITS
    to_soa = lambda a: jnp.pad(jnp.transpose(a.reshape(nc, CHUNK, 2, HEADS), (2, 1, 3, 0)).reshape(2, CHUNK, u),
                               ((0, 0), (0, 0), (0, up - u)))
    beta_s = to_soa(bg[:, :2 * HEADS])
    g_s = to_soa(bg[:, 2 * HEADS:4 * HEADS])
    gc_s, ep_s, egl_s = _gdn_scal_call(g_s)
    gram, qk, kt = _gdn_gram_call(qkv, nc)
    pad_u = lambda a: jnp.pad(a.reshape(u, -1), ((0, up - u), (0, 0)))
    g_flat, qk_flat = pad_u(gram), pad_u(qk)
    twtu, atde = (a[:, :u] for a in _gdn_inv_call(g_flat, qk_flat, beta_s, gc_s))
    ep = jnp.transpose(ep_s[:, :, :u].reshape(2, CHUNK, HEADS, nc), (0, 2, 3, 1))
    egl = egl_s[:, 0, :u].reshape(2, HEADS, nc)
    return _gdn_scan_call(qkv, kt, twtu, atde, ep, egl, s0, nc)


def _merge_kernel(o_ref, zs_ref, yc_ref, ga_ref, gb_ref, wa_ref, wb_ref, m_ref, yg_s):
    @pl.when(pl.program_id(1) == 0)
    def _():
        for hh in range(HEADS):
            o = o_ref[0, hh].astype(F32) + o_ref[1, hh].astype(F32)
            y = o * lax.rsqrt(jnp.mean(o * o, axis=-1, keepdims=True) + RMS_EPS) * zs_ref[hh]
            yg_s[:, hh * HEAD_DIM:(hh + 1) * HEAD_DIM] = y.astype(BF16)

    pa = jnp.dot(yg_s[...], wa_ref[...], preferred_element_type=F32)
    pb = jnp.dot(yc_ref[...], wb_ref[...], preferred_element_type=F32)
    m_ref[...] = (ga_ref[...].astype(F32) * pa + gb_ref[...].astype(F32) * pb).astype(m_ref.dtype)


def _merge_call(o, zs, yconv, gates, wa, wb):
    t = yconv.shape[0]
    tm, tn = 512, 1024
    nj = D_MODEL // tn
    return pl.pallas_call(
        _merge_kernel,
        out_shape=jax.ShapeDtypeStruct((t, D_MODEL), BF16),
        grid=(t // tm, nj),
        in_specs=[pl.BlockSpec((2, HEADS, tm, HEAD_DIM), lambda i, j: (0, 0, i, 0)),
                  pl.BlockSpec((HEADS, tm, HEAD_DIM), lambda i, j: (0, i, 0)),
                  pl.BlockSpec((tm, CONV_WIDTH), lambda i, j: (i, 0)),
                  pl.BlockSpec((tm, tn), lambda i, j: (i, j)),
                  pl.BlockSpec((tm, tn), lambda i, j: (i, nj + j)),
                  pl.BlockSpec((GDN_WIDTH, tn), lambda i, j: (0, j)),
                  pl.BlockSpec((CONV_WIDTH, tn), lambda i, j: (0, j))],
        out_specs=pl.BlockSpec((tm, tn), lambda i, j: (i, j)),
        scratch_shapes=[pltpu.VMEM((tm, GDN_WIDTH), BF16)],
        compiler_params=_cparams(("arbitrary", "arbitrary")),
        name="merge",
    )(o, zs, yconv, gates, gates, wa, wb)


LN_ROWS = 64


def _outmm_kernel(m_ref, w_ref, o_ref):
    o_ref[...] = jnp.dot(m_ref[...], w_ref[...], preferred_element_type=F32)


def _outmm_call(m, w_out):
    t = m.shape[0]
    tm, tn = 512, 1024
    return pl.pallas_call(
        _outmm_kernel,
        out_shape=jax.ShapeDtypeStruct((t, D_MODEL), F32),
        grid=(D_MODEL // tn, t // tm),
        in_specs=[pl.BlockSpec((tm, D_MODEL), lambda j, i: (i, 0)),
                  pl.BlockSpec((D_MODEL, tn), lambda j, i: (0, j))],
        out_specs=pl.BlockSpec((tm, tn), lambda j, i: (i, j)),
        compiler_params=_cparams(("arbitrary", "arbitrary")),
        name="out_proj",
    )(m, w_out)


HALF = D_MODEL // 2


def _bf16_bits(v):
    return pltpu.bitcast(v.astype(BF16).astype(F32), jnp.uint32)


def _post_mix_kernel(mix_ref, x_ref, mod_ref, ln_ref, wrh_ref, wrl_ref, x1_ref, tokp_ref, logit_ref, tok_s):
    def rows_step(ci, c):
        sl = pl.ds(pl.multiple_of(ci * LN_ROWS, LN_ROWS), LN_ROWS)
        x1 = _layer_norm(DEEPNORM_ALPHA * x_ref[sl, :] + mod_ref[0:1, :] * mix_ref[sl, :])
        x1 = x1 * ln_ref[0:1, :] + ln_ref[1:2, :]
        x1_ref[sl, :] = x1
        tok = _layer_norm(x1) * (1.0 + mod_ref[2:3, :]) + mod_ref[1:2, :]
        tok_s[sl, :] = tok
        tokp_ref[sl, :] = _bf16_bits(tok[:, :HALF]) | (_bf16_bits(tok[:, HALF:]) >> 16)
        return c

    lax.fori_loop(0, x_ref.shape[0] // LN_ROWS, rows_step, 0)
    tok = tok_s[...]
    t_hi = tok.astype(BF16)
    t_lo = (tok - t_hi.astype(F32)).astype(BF16)
    logit_ref[...] = (jnp.dot(t_hi, wrh_ref[...], preferred_element_type=F32)
                      + jnp.dot(t_lo, wrh_ref[...], preferred_element_type=F32)
                      + jnp.dot(t_hi, wrl_ref[...], preferred_element_type=F32))


def _post_mix_call(mix, x, mod3, ln1, w_router):
    wr_hi = w_router.astype(BF16)
    wr_lo = (w_router - wr_hi.astype(F32)).astype(BF16)
    t = x.shape[0]
    tm = 256
    row = pl.BlockSpec((tm, D_MODEL), lambda i: (i, 0))
    vec = pl.BlockSpec((SUBLANES, D_MODEL), lambda i: (0, 0))
    wr = pl.BlockSpec((D_MODEL, LANES), lambda i: (0, 0))
    return pl.pallas_call(
        _post_mix_kernel,
        out_shape=(jax.ShapeDtypeStruct((t, D_MODEL), F32),
                   jax.ShapeDtypeStruct((t, HALF), jnp.uint32),
                   jax.ShapeDtypeStruct((t, LANES), F32)),
        grid=(t // tm,),
        in_specs=[row, row, vec, vec, wr, wr],
        out_specs=(row, pl.BlockSpec((tm, HALF), lambda i: (i, 0)), pl.BlockSpec((tm, LANES), lambda i: (i, 0))),
        scratch_shapes=[pltpu.VMEM((tm, D_MODEL), F32)],
        compiler_params=_cparams(("arbitrary",)),
        name="post_mix",
    )(mix, x, mod3, ln1, wr_hi, wr_lo)


MOE_BM = 256
MOE_HC = 256


def _gather_kernel(rows, nu_ref, idx_ref, nxt_ref, src_ref, dst_ref, buf, sem):
    i = pl.program_id(0)
    nu = nu_ref[0]
    slot = i % 2

    def copy(iref, s, r):
        return pltpu.make_async_copy(src_ref.at[pl.ds(iref[0, 0, r], 1), :],
                                     buf.at[s, pl.ds(r, 1), :], sem.at[s])

    def issue(iref, s):
        def body(r, c):
            copy(iref, s, r).start()
            return c
        lax.fori_loop(0, rows, body, 0, unroll=8)

    @pl.when(i == 0)
    def _():
        issue(idx_ref, 0)

    @pl.when(i + 1 < nu)
    def _():
        issue(nxt_ref, 1 - slot)

    @pl.when(i < nu)
    def _():
        def body(r, c):
            copy(idx_ref, slot, r).wait()
            return c
        lax.fori_loop(0, rows, body, 0, unroll=8)
        packed = buf[slot]
        dst_ref[:, :HALF] = pltpu.bitcast(packed & jnp.uint32(0xFFFF0000), F32).astype(BF16)
        dst_ref[:, HALF:] = pltpu.bitcast(packed << 16, F32).astype(BF16)

    @pl.when(i >= nu)
    def _():
        dst_ref[...] = jnp.zeros(dst_ref.shape, dst_ref.dtype)


def _gather_call(src, idx, n_used):
    n = idx.shape[0]
    rows = MOE_BM
    nblk = n // rows
    idx3 = idx.reshape(nblk, 1, rows)
    grid_spec = pltpu.PrefetchScalarGridSpec(
        num_scalar_prefetch=1,
        grid=(nblk,),
        in_specs=[pl.BlockSpec((1, 1, rows), lambda i, nu: (i, 0, 0), memory_space=pltpu.SMEM),
                  pl.BlockSpec((1, 1, rows), lambda i, nu: (jnp.minimum(i + 1, nblk - 1), 0, 0),
                               memory_space=pltpu.SMEM),
                  pl.BlockSpec(memory_space=pl.ANY)],
        out_specs=pl.BlockSpec((rows, D_MODEL), lambda i, nu: (i, 0)),
        scratch_shapes=[pltpu.VMEM((2, rows, src.shape[1]), src.dtype), pltpu.SemaphoreType.DMA((2,))],
    )
    return pl.pallas_call(
        functools.partial(_gather_kernel, rows),
        out_shape=jax.ShapeDtypeStruct((n, D_MODEL), BF16),
        grid_spec=grid_spec,
        compiler_params=_cparams(("arbitrary",)),
        name="moe_gather",
    )(n_used, idx3, idx3, src)


def _expert_changed(be_ref, nu_ref, b):
    bb = jnp.minimum(b, nu_ref[0] - 1)
    return (b == 0) | (be_ref[bb] != be_ref[jnp.maximum(bb - 1, 0)])


def _moe_hidden_kernel(be_ref, nu_ref, x_ref, wg_ref, wu_ref, h_ref, w_s):
    b = pl.program_id(0)
    hc = pl.program_id(1)
    live = b < nu_ref[0]

    @pl.when(live & _expert_changed(be_ref, nu_ref, b))
    def _():
        w_s[hc, :, :MOE_HC] = wg_ref[0].astype(BF16)
        w_s[hc, :, MOE_HC:] = wu_ref[0].astype(BF16)

    @pl.when(live)
    def _():
        gu = jnp.dot(x_ref[...], w_s[hc], preferred_element_type=F32)
        h_ref[...] = (_silu(gu[:, :MOE_HC]) * gu[:, MOE_HC:]).astype(h_ref.dtype)

    @pl.when(jnp.logical_not(live))
    def _():
        h_ref[...] = jnp.zeros(h_ref.shape, h_ref.dtype)


def _moe_down_kernel(be_ref, nu_ref, h_ref, wd_ref, y_ref, wd_s):
    b = pl.program_id(0)
    live = b < nu_ref[0]

    @pl.when(live & _expert_changed(be_ref, nu_ref, b))
    def _():
        wd_s[...] = wd_ref[0].astype(BF16)

    @pl.when(live)
    def _():
        y_ref[...] = jnp.dot(h_ref[...], wd_s[...], preferred_element_type=F32)

    @pl.when(jnp.logical_not(live))
    def _():
        y_ref[...] = jnp.zeros(y_ref.shape, F32)


def _expert_call(xs, wg, wu, wd, block_e, n_used):
    n = xs.shape[0]
    nb = n // MOE_BM
    nh = EXPERT_HIDDEN // MOE_HC
    blk = lambda b, nu: jnp.minimum(b, nu[0] - 1)

    def w_chunk(b, h, be, nu):
        bb = blk(b, nu)
        first = (b < nu[0]) & ((bb == 0) | (be[bb] != be[jnp.maximum(bb - 1, 0)]))
        return be[bb], 0, jnp.where(first, h, nh - 1)

    hid = pl.pallas_call(
        _moe_hidden_kernel,
        out_shape=jax.ShapeDtypeStruct((n, EXPERT_HIDDEN), BF16),
        grid_spec=pltpu.PrefetchScalarGridSpec(
            num_scalar_prefetch=2,
            grid=(nb, nh),
            in_specs=[pl.BlockSpec((MOE_BM, D_MODEL), lambda b, h, be, nu: (blk(b, nu), 0)),
                      pl.BlockSpec((1, D_MODEL, MOE_HC), w_chunk),
                      pl.BlockSpec((1, D_MODEL, MOE_HC), w_chunk)],
            out_specs=pl.BlockSpec((MOE_BM, MOE_HC), lambda b, h, be, nu: (b, h)),
            scratch_shapes=[pltpu.VMEM((nh, D_MODEL, 2 * MOE_HC), BF16)]),
        compiler_params=_cparams(("arbitrary", "arbitrary")),
        name="moe_hidden",
    )(block_e, n_used, xs, wg, wu)
    return pl.pallas_call(
        _moe_down_kernel,
        out_shape=jax.ShapeDtypeStruct((n, D_MODEL), F32),
        grid_spec=pltpu.PrefetchScalarGridSpec(
            num_scalar_prefetch=2,
            grid=(nb,),
            in_specs=[pl.BlockSpec((MOE_BM, EXPERT_HIDDEN), lambda b, be, nu: (blk(b, nu), 0)),
                      pl.BlockSpec((1, EXPERT_HIDDEN, D_MODEL), lambda b, be, nu: (be[blk(b, nu)], 0, 0))],
            out_specs=pl.BlockSpec((MOE_BM, D_MODEL), lambda b, be, nu: (b, 0)),
            scratch_shapes=[pltpu.VMEM((EXPERT_HIDDEN, D_MODEL), BF16)]),
        compiler_params=_cparams(("arbitrary",)),
        name="moe_down",
    )(block_e, n_used, hid, wd)


def _combine_kernel(rows, d_ref, nxt_ref, y_ref, x1_ref, w_ref, mod_ref, ln_ref, o_ref, buf, sem):
    i = pl.program_id(0)
    slot = i % 2

    def copy(iref, s, k, r):
        return pltpu.make_async_copy(y_ref.at[pl.ds(iref[0, k, r], 1), :],
                                     buf.at[s, k, pl.ds(r, 1), :], sem.at[s])

    def issue(iref, s):
        def body(r, c):
            copy(iref, s, 0, r).start()
            copy(iref, s, 1, r).start()
            return c
        lax.fori_loop(0, rows, body, 0, unroll=8)

    @pl.when(i == 0)
    def _():
        issue(d_ref, 0)

    @pl.when(i + 1 < pl.num_programs(0))
    def _():
        issue(nxt_ref, 1 - slot)

    def wait_body(r, c):
        copy(d_ref, slot, 0, r).wait()
        copy(d_ref, slot, 1, r).wait()
        return c

    lax.fori_loop(0, rows, wait_body, 0, unroll=8)

    def rows_step(ci, c):
        sl = pl.ds(pl.multiple_of(ci * LN_ROWS, LN_ROWS), LN_ROWS)
        f = w_ref[sl, 0:1] * buf[slot, 0, sl, :] + w_ref[sl, 1:2] * buf[slot, 1, sl, :]
        y = _layer_norm(DEEPNORM_ALPHA * x1_ref[sl, :] + mod_ref[0:1, :] * f)
        o_ref[sl, :] = y * ln_ref[0:1, :] + ln_ref[1:2, :]
        return c

    lax.fori_loop(0, rows // LN_ROWS, rows_step, 0)


def _combine_call(ys, dest, weights, x1, mod_row, ln2):
    t = x1.shape[0]
    rows = 128
    nblk = t // rows
    dest3 = jnp.transpose(dest.reshape(nblk, rows, 2), (0, 2, 1))
    return pl.pallas_call(
        functools.partial(_combine_kernel, rows),
        out_shape=jax.ShapeDtypeStruct((t, D_MODEL), F32),
        grid=(nblk,),
        in_specs=[pl.BlockSpec((1, 2, rows), lambda i: (i, 0, 0), memory_space=pltpu.SMEM),
                  pl.BlockSpec((1, 2, rows), lambda i: (jnp.minimum(i + 1, nblk - 1), 0, 0),
                               memory_space=pltpu.SMEM),
                  pl.BlockSpec(memory_space=pl.ANY),
                  pl.BlockSpec((rows, D_MODEL), lambda i: (i, 0)),
                  pl.BlockSpec((rows, 2), lambda i: (i, 0)),
                  pl.BlockSpec((SUBLANES, D_MODEL), lambda i: (0, 0)),
                  pl.BlockSpec((SUBLANES, D_MODEL), lambda i: (0, 0))],
        out_specs=pl.BlockSpec((rows, D_MODEL), lambda i: (i, 0)),
        scratch_shapes=[pltpu.VMEM((2, 2, rows, D_MODEL), F32), pltpu.SemaphoreType.DMA((2,))],
        compiler_params=_cparams(("arbitrary",)),
        name="moe_combine",
    )(dest3, dest3, ys, x1, weights, mod_row, ln2)


def _route(logits, b_group, b_expert):
    t = logits.shape[0]
    p_group = jax.nn.softmax(logits[:, :N_GROUPS] + b_group, axis=-1)
    group = jnp.argmax(p_group, axis=-1)
    gate_group = jnp.take_along_axis(p_group, group[:, None], axis=-1)
    le = (logits[:, N_GROUPS:N_GROUPS + N_EXPERTS] + b_expert).reshape(t, N_GROUPS, EXPERTS_PER_GROUP)
    le = jnp.take_along_axis(le, group[:, None, None], axis=1)[:, 0]
    top_p, top_i = lax.top_k(jax.nn.softmax(le, axis=-1), 2)
    weights = gate_group * top_p / jnp.sum(top_p, axis=-1, keepdims=True)
    expert_id = group[:, None] * EXPERTS_PER_GROUP + top_i
    return expert_id.astype(jnp.int32), weights


def _moe(tok, logits, b_group, b_expert, wg, wu, wd, x1, mod_row, ln2):
    t = tok.shape[0]
    expert_id, weights = _route(logits, b_group, b_expert)
    e_flat = expert_id.reshape(-1)
    n_assign = e_flat.shape[0]
    onehot = (e_flat[:, None] == jnp.arange(N_EXPERTS)[None, :]).astype(jnp.int32)
    rank = jnp.take_along_axis(jnp.cumsum(onehot, axis=0), e_flat[:, None], axis=1)[:, 0] - 1
    counts = jnp.sum(onehot, axis=0)
    padded = (counts + MOE_BM - 1) // MOE_BM * MOE_BM
    pad_end = jnp.cumsum(padded)
    dest = (pad_end - padded)[e_flat] + rank
    n_blocks = -(-n_assign // MOE_BM) + N_EXPERTS
    n_rows = n_blocks * MOE_BM
    src_tok = jnp.zeros((n_rows,), jnp.int32).at[dest].set(jnp.arange(n_assign, dtype=jnp.int32) // 2)
    block_e = jnp.minimum(jnp.searchsorted(pad_end, jnp.arange(n_blocks) * MOE_BM, side="right"),
                          N_EXPERTS - 1).astype(jnp.int32)
    n_used = (pad_end[-1] // MOE_BM).astype(jnp.int32).reshape(1)
    xs = _gather_call(tok, src_tok, n_used)
    ys = _expert_call(xs, wg, wu, wd, block_e, n_used)
    return _combine_call(ys, dest.reshape(t, 2).astype(jnp.int32), weights, x1, mod_row, ln2)


def _pad_rows(v, rows=SUBLANES):
    return jnp.pad(v, ((0, rows - v.shape[0]), (0, 0)))


def _layer(x, ctx, c, c_ctx, w_ada, b_ada, w_in, conv_qkv, a_log, dt_bias, gdn_norm_w, conv_b,
           w_branch_a, w_branch_b, w_out, ln1_g, ln1_b, w_router_group, b_router_group,
           w_router_expert, b_router_expert, w_exp_gate, w_exp_up, w_exp_down, ln2_g, ln2_b):
    t = x.shape[0]
    mod = _mod_call(_pad_rows(jnp.stack([c, c_ctx])), w_ada, b_ada.reshape(1, -1))
    mod_lat = mod[0].reshape(N_MOD, D_MODEL)
    mod_ctx = mod[1].reshape(N_MOD, D_MODEL)

    w_pa = _wcopy_call(w_in, 0, 4 * GDN_WIDTH)
    w_pb = _wcopy_call(w_in, 4 * GDN_WIDTH + 4 * HEADS, 3 * CONV_WIDTH + 2 * D_MODEL)
    w_pc = w_in
    lane_row = lambda v: jnp.pad(v.reshape(1, -1), ((0, 0), (2 * HEADS, LANES - 4 * HEADS)))
    alog_row, dtb_row = lane_row(a_log), lane_row(dt_bias)

    h_ctx = _ln_mod_call(ctx, mod_ctx[0:1], mod_ctx[1:2])
    tc = ctx.shape[0]
    qkv_c = _inproj_qkv(h_ctx, w_pa, conv_qkv, tc, tc)
    bg_c = _inproj_ba(h_ctx, w_pc, alog_row, dtb_row, tc)
    s0 = jnp.zeros((2, HEADS, HEAD_DIM, HEAD_DIM), F32)
    _, s_ctx = _gdn(qkv_c, bg_c, s0)

    tm = 512
    h = _ln_mod_call(x, mod_lat[0:1], mod_lat[1:2])
    qkv = _inproj_qkv(h, w_pa, conv_qkv, GRID_W, tm)
    zs = _inproj_z(h, w_pa, gdn_norm_w.reshape(1, HEAD_DIM), tm)
    yconv = _inproj_xbc(h, w_pb, conv_b, GRID_W, tm)
    gates = _inproj_gates(h, w_pb, tm)
    bg = _inproj_ba(h, w_pc, alog_row, dtb_row, tm)
    o, _ = _gdn(qkv, bg, s_ctx)
    m = _merge_call(o, zs, yconv, gates, w_branch_a.astype(BF16), w_branch_b.astype(BF16))

    w_router = jnp.pad(jnp.concatenate([w_router_group, w_router_expert], axis=1),
                       ((0, 0), (0, LANES - N_GROUPS - N_EXPERTS)))
    mix = _outmm_call(m, w_out.astype(BF16))
    x1, tok, logits = _post_mix_call(mix, x, _pad_rows(mod_lat[2:5]),
                                     _pad_rows(jnp.stack([ln1_g, ln1_b])), w_router)

    return _moe(tok, logits, b_router_group, b_router_expert, w_exp_gate, w_exp_up, w_exp_down,
                x1, _pad_rows(mod_lat[5:6]), _pad_rows(jnp.stack([ln2_g, ln2_b])))


def kernel(x, c, ctx, c_ctx, w_ada, b_ada, w_in, conv_qkv, a_log, dt_bias, gdn_norm_w, conv_b,
           w_branch_a, w_branch_b, w_out, ln1_g, ln1_b, w_router_group, b_router_group,
           w_router_expert, b_router_expert, w_exp_gate, w_exp_up, w_exp_down, ln2_g, ln2_b):
    assert x.shape[0] == 1 and w_ada.shape[0] == 1, "single batch element, single layer"
    out = _layer(x[0], ctx[0], c[0], c_ctx, w_ada[0], b_ada[0], w_in[0], conv_qkv[0],
                 a_log[0].reshape(-1), dt_bias[0].reshape(-1), gdn_norm_w[0], conv_b[0],
                 w_branch_a[0], w_branch_b[0], w_out[0], ln1_g[0], ln1_b[0],
                 w_router_group[0], b_router_group[0], w_router_expert[0], b_router_expert[0],
                 w_exp_gate[0], w_exp_up[0], w_exp_down[0], ln2_g[0], ln2_b[0])
    return out[None]
```

```python
import functools

import numpy as np
import jax
import jax.numpy as jnp
from jax import lax
from jax.experimental import pallas as pl
from jax.experimental.pallas import tpu as pltpu

F32 = jnp.float32
BF16 = jnp.bfloat16
HIGHEST = lax.Precision.HIGHEST

D_MODEL = 4096
GRID_W = 64
CHUNK = 64
HEADS = 16
HEAD_DIM = 128
GDN_WIDTH = HEADS * HEAD_DIM
CONV_WIDTH = D_MODEL // 2
N_GROUPS = 8
EXPERTS_PER_GROUP = 8
N_EXPERTS = 64
EXPERT_HIDDEN = 768
N_MOD = 6
DEEPNORM_ALPHA = 2.0 ** 0.25
LN_EPS = 1e-6
RMS_EPS = 1e-6

LANES = 128
SUBLANES = 8
UNITS = 128
VMEM_LIMIT = 52 * 1024 * 1024


def _cparams(sem, vmem=VMEM_LIMIT):
    return pltpu.CompilerParams(dimension_semantics=sem, vmem_limit_bytes=vmem)


def _silu(x):
    return x * jax.nn.sigmoid(x)


def _layer_norm(x):
    mu = jnp.mean(x, axis=-1, keepdims=True)
    xc = x - mu
    var = jnp.mean(xc * xc, axis=-1, keepdims=True)
    return xc * lax.rsqrt(var + LN_EPS)


def _mod_kernel(c_ref, w_ref, b_ref, o_ref):
    s = _silu(c_ref[...])
    o_ref[...] = jnp.dot(s, w_ref[...], preferred_element_type=F32, precision=HIGHEST) + b_ref[...]


def _mod_call(cs, w_ada, b_ada):
    n = w_ada.shape[1]
    tn = 512
    return pl.pallas_call(
        _mod_kernel,
        out_shape=jax.ShapeDtypeStruct((SUBLANES, n), F32),
        grid=(n // tn,),
        in_specs=[pl.BlockSpec((SUBLANES, D_MODEL), lambda j: (0, 0)),
                  pl.BlockSpec((D_MODEL, tn), lambda j: (0, j)),
                  pl.BlockSpec((1, tn), lambda j: (0, j))],
        out_specs=pl.BlockSpec((SUBLANES, tn), lambda j: (0, j)),
        compiler_params=_cparams(("arbitrary",)),
        name="mod",
    )(cs, w_ada, b_ada)


def _ln_mod_kernel(x_ref, shift_ref, scale_ref, o_ref):
    y = _layer_norm(x_ref[...])
    o_ref[...] = (y * (1.0 + scale_ref[...]) + shift_ref[...]).astype(o_ref.dtype)


def _ln_mod_call(x, shift, scale):
    t = x.shape[0]
    tr = min(256, t)
    return pl.pallas_call(
        _ln_mod_kernel,
        out_shape=jax.ShapeDtypeStruct((t, D_MODEL), BF16),
        grid=(t // tr,),
        in_specs=[pl.BlockSpec((tr, D_MODEL), lambda i: (i, 0)),
                  pl.BlockSpec((1, D_MODEL), lambda i: (0, 0)),
                  pl.BlockSpec((1, D_MODEL), lambda i: (0, 0))],
        out_specs=pl.BlockSpec((tr, D_MODEL), lambda i: (i, 0)),
        compiler_params=_cparams(("arbitrary",)),
        name="ln_mod",
    )(x, shift, scale)


COL_QKV = 0
COL_Z = 3 * GDN_WIDTH
COL_XB = 0
COL_BG = CONV_WIDTH
COL_CG = 2 * CONV_WIDTH
COL_GATES = 3 * CONV_WIDTH
COL_BA = 0


def _conv3_rows(x, taps_ref, group):
    rows = x.shape[0]
    pos = lax.broadcasted_iota(jnp.int32, x.shape, 0) % group
    prev = jnp.where(pos == 0, 0.0, pltpu.roll(x, 1, 0))
    nxt = jnp.where(pos == group - 1, 0.0, pltpu.roll(x, rows - 1, 0))
    return taps_ref[0:1, :] * prev + taps_ref[1:2, :] * x + taps_ref[2:3, :] * nxt


def _ip_qkv_kernel(group, x_ref, w_ref, taps_ref, o_ref):
    acc = jnp.dot(x_ref[...], w_ref[...], preferred_element_type=F32)
    y = _silu(_conv3_rows(acc, taps_ref, group))
    j = pl.program_id(0)
    tiles_per_part = GDN_WIDTH // acc.shape[1]
    for hh in range(acc.shape[1] // HEAD_DIM):
        ys = y[:, hh * HEAD_DIM:(hh + 1) * HEAD_DIM]
        r = lax.rsqrt(jnp.sum(ys * ys, axis=-1, keepdims=True) + 1e-6)
        scale = jnp.where(j < tiles_per_part, r * HEAD_DIM ** -0.5,
                          jnp.where(j < 2 * tiles_per_part, r, 1.0))
        o_ref[hh] = (ys * scale).astype(o_ref.dtype)


def _ip_z_kernel(x_ref, w_ref, nw_ref, o_ref):
    acc = jnp.dot(x_ref[...], w_ref[...], preferred_element_type=F32)
    for hh in range(acc.shape[1] // HEAD_DIM):
        zs = acc[:, hh * HEAD_DIM:(hh + 1) * HEAD_DIM]
        o_ref[hh] = _silu(zs) * nw_ref[...]


def _ip_xbc_kernel(group, x_ref, wxb_ref, wbg_ref, wcg_ref, taps_ref, o_ref):
    x = x_ref[...]
    xb = jnp.dot(x, wxb_ref[...], preferred_element_type=F32)
    bg = jnp.dot(x, wbg_ref[...], preferred_element_type=F32)
    cg = jnp.dot(x, wcg_ref[...], preferred_element_type=F32)
    o_ref[...] = (bg * _conv3_rows(cg * xb, taps_ref, group)).astype(o_ref.dtype)


def _ip_gates_kernel(x_ref, w_ref, o_ref):
    acc = jnp.dot(x_ref[...], w_ref[...], preferred_element_type=F32)
    o_ref[...] = jax.nn.sigmoid(acc).astype(o_ref.dtype)


def _ip_ba_kernel(x_ref, w_ref, alog_ref, dtb_ref, o_ref):
    acc = jnp.dot(x_ref[...], w_ref[...], preferred_element_type=F32)
    lane = lax.broadcasted_iota(jnp.int32, acc.shape, 1)
    a = acc + dtb_ref[...]
    softplus = jnp.maximum(a, 0.0) + jnp.log(1.0 + jnp.exp(-jnp.abs(a)))
    o_ref[...] = jnp.where(lane < 2 * HEADS, jax.nn.sigmoid(acc), -jnp.exp(alog_ref[...]) * softplus)


def _inproj(kernel, h, w_cat, col_starts, tn, n_tiles, extra, extra_specs, out_shape, out_spec, tm):
    t = h.shape[0]
    w_specs = [pl.BlockSpec((D_MODEL, tn), functools.partial(lambda j, i, o: (0, o + j), o=c // tn))
               for c in col_starts]
    return pl.pallas_call(
        kernel,
        out_shape=out_shape,
        grid=(n_tiles, t // tm),
        in_specs=[pl.BlockSpec((tm, D_MODEL), lambda j, i: (i, 0))] + w_specs + extra_specs,
        out_specs=out_spec,
        compiler_params=_cparams(("arbitrary", "arbitrary")),
        name="inproj",
    )(h, *([w_cat] * len(col_starts)), *extra)


def _inproj_qkv(h, w_cat, taps, group, tm):
    t = h.shape[0]
    tn = 1024
    hpt = tn // HEAD_DIM
    return _inproj(
        functools.partial(_ip_qkv_kernel, group), h, w_cat, [COL_QKV], tn, 3 * GDN_WIDTH // tn,
        [taps], [pl.BlockSpec((3, tn), lambda j, i: (0, j))],
        jax.ShapeDtypeStruct((3 * HEADS, t, HEAD_DIM), BF16),
        pl.BlockSpec((hpt, tm, HEAD_DIM), lambda j, i: (j, i, 0)), tm)


def _inproj_z(h, w_cat, norm_w, tm):
    t = h.shape[0]
    tn = 1024
    hpt = tn // HEAD_DIM
    return _inproj(
        _ip_z_kernel, h, w_cat, [COL_Z], tn, GDN_WIDTH // tn,
        [norm_w], [pl.BlockSpec((1, HEAD_DIM), lambda j, i: (0, 0))],
        jax.ShapeDtypeStruct((HEADS, t, HEAD_DIM), F32),
        pl.BlockSpec((hpt, tm, HEAD_DIM), lambda j, i: (j, i, 0)), tm)


def _inproj_xbc(h, w_cat, taps, group, tm):
    t = h.shape[0]
    tn = 512
    return _inproj(
        functools.partial(_ip_xbc_kernel, group), h, w_cat, [COL_XB, COL_BG, COL_CG], tn,
        CONV_WIDTH // tn, [taps], [pl.BlockSpec((3, tn), lambda j, i: (0, j))],
        jax.ShapeDtypeStruct((t, CONV_WIDTH), BF16),
        pl.BlockSpec((tm, tn), lambda j, i: (i, j)), tm)


def _inproj_gates(h, w_cat, tm):
    t = h.shape[0]
    tn = 1024
    return _inproj(
        _ip_gates_kernel, h, w_cat, [COL_GATES], tn, 2 * D_MODEL // tn, [], [],
        jax.ShapeDtypeStruct((t, 2 * D_MODEL), BF16),
        pl.BlockSpec((tm, tn), lambda j, i: (i, j)), tm)


def _inproj_ba(h, w_cat, alog_row, dtb_row, tm):
    t = h.shape[0]
    return _inproj(
        _ip_ba_kernel, h, w_cat, [COL_BA], LANES, 1, [alog_row, dtb_row],
        [pl.BlockSpec((1, LANES), lambda j, i: (0, 0))] * 2,
        jax.ShapeDtypeStruct((t, LANES), F32),
        pl.BlockSpec((tm, LANES), lambda j, i: (i, 0)), tm)


def _gdn_scal_kernel(g_ref, gc_ref, ep_ref, egl_ref):
    d = pl.program_id(0)
    r = lax.broadcasted_iota(jnp.int32, (CHUNK, CHUNK), 0)
    c = lax.broadcasted_iota(jnp.int32, (CHUNK, CHUNK), 1)
    tri = jnp.where(d == 0, (c <= r).astype(F32), (c >= r).astype(F32))
    gc = jnp.dot(tri, g_ref[0], preferred_element_type=F32, precision=HIGHEST)
    gl = jnp.where(d == 0, gc[CHUNK - 1:CHUNK, :], gc[0:1, :])
    gc_ref[0] = gc
    ep_ref[0] = jnp.exp(gl - gc)
    egl_ref[0] = jnp.exp(gl)


def _gdn_scal_call(g_s):
    u = g_s.shape[2]
    tu = 512 if u % 512 == 0 else UNITS
    spec = pl.BlockSpec((1, CHUNK, tu), lambda d, i: (d, 0, i))
    return pl.pallas_call(
        _gdn_scal_kernel,
        out_shape=(jax.ShapeDtypeStruct(g_s.shape, F32), jax.ShapeDtypeStruct(g_s.shape, F32),
                   jax.ShapeDtypeStruct((2, 1, u), F32)),
        grid=(2, u // tu),
        in_specs=[spec],
        out_specs=(spec, spec, pl.BlockSpec((1, 1, tu), lambda d, i: (d, 0, i))),
        compiler_params=_cparams(("arbitrary", "arbitrary")),
        name="gdn_scal",
    )(g_s)


def _nt_dot(a, b):
    return lax.dot_general(a, b, (((1,), (1,)), ((), ())), preferred_element_type=F32)


def _gdn_gram_kernel(cb, q_ref, k_ref, g_ref, qk_ref, kt_ref):
    r = lax.broadcasted_iota(jnp.int32, (HEAD_DIM, HEAD_DIM), 0)
    c = lax.broadcasted_iota(jnp.int32, (HEAD_DIM, HEAD_DIM), 1)
    eye = (r == c).astype(BF16)
    for ci in range(cb):
        k = k_ref[0, ci * CHUNK:(ci + 1) * CHUNK, :]
        q = q_ref[0, ci * CHUNK:(ci + 1) * CHUNK, :]
        g_ref[ci] = _nt_dot(k, k)
        qk_ref[ci] = _nt_dot(q, k)
        kt_ref[ci] = _nt_dot(eye, k).astype(BF16)


def _gdn_gram_call(qkv, nc):
    cb = min(32, nc)
    ncb = nc // cb
    u = HEADS * nc
    return pl.pallas_call(
        functools.partial(_gdn_gram_kernel, cb),
        out_shape=(jax.ShapeDtypeStruct((u, CHUNK, CHUNK), F32),
                   jax.ShapeDtypeStruct((u, CHUNK, CHUNK), F32),
                   jax.ShapeDtypeStruct((u, HEAD_DIM, CHUNK), BF16)),
        grid=(HEADS, ncb),
        in_specs=[pl.BlockSpec((1, cb * CHUNK, HEAD_DIM), lambda h, c: (h, c, 0)),
                  pl.BlockSpec((1, cb * CHUNK, HEAD_DIM), lambda h, c: (HEADS + h, c, 0))],
        out_specs=(pl.BlockSpec((cb, CHUNK, CHUNK), lambda h, c: (h * ncb + c, 0, 0)),
                   pl.BlockSpec((cb, CHUNK, CHUNK), lambda h, c: (h * ncb + c, 0, 0)),
                   pl.BlockSpec((cb, HEAD_DIM, CHUNK), lambda h, c: (h * ncb + c, 0, 0))),
        compiler_params=_cparams(("arbitrary", "arbitrary")),
        name="gdn_gram",
    )(qkv, qkv)


NB = CHUNK // SUBLANES


def _row_bcast(ref, row):
    return jnp.broadcast_to(ref[pl.ds(row, 1), :], (SUBLANES, UNITS))


def _gdn_inv_kernel(g_ref, qk_ref, beta_ref, gc_ref, twtu_ref, atde_ref, *scratch):
    for bwd in (False, True):
        @pl.when(pl.program_id(0) == int(bwd))
        def _(bwd=bwd):
            _gdn_inv_body(bwd, g_ref, qk_ref, beta_ref, gc_ref, twtu_ref.at[0], atde_ref.at[0], *scratch)


def _gdn_inv_body(bwd, g_ref, qk_ref, beta_ref, gc_ref, twtu_ref, atde_ref,
                  g_s, qk_s, l_s, t_s, tw_s, tu_s, at_s, de_s, e_s):
    pos = (lambda a: CHUNK - 1 - a) if bwd else (lambda a: a)
    blk = (lambda b: NB - 1 - b) if bwd else (lambda b: b)

    for pp in range(CHUNK // 2):
        sl = slice(pp * LANES, (pp + 1) * LANES)
        g_s[sl, :] = g_ref[:, sl].T
        qk_s[sl, :] = qk_ref[:, sl].T
    e_s[...] = jnp.exp(gc_ref[0])

    @pl.when(pl.program_id(1) == 0)
    def _():
        for ref in (tw_s, tu_s, at_s, de_s):
            ref[...] = jnp.zeros(ref.shape, F32)

    sub = lax.broadcasted_iota(jnp.int32, (SUBLANES, UNITS), 0)
    zero = jnp.zeros((SUBLANES, UNITS), F32)
    rs = range(SUBLANES)
    own = [(SUBLANES - 1 - r) if bwd else r for r in rs]
    earlier = [(sub > own[r]) if bwd else (sub < own[r]) for r in rs]

    def tile_ds(p, b):
        return pl.ds(pl.multiple_of(p * CHUNK + b * SUBLANES, SUBLANES), SUBLANES)

    def cols_ds(b):
        return pl.ds(pl.multiple_of(b * SUBLANES, SUBLANES), SUBLANES)

    def row_block(ib, carry):
        b_own = blk(ib)
        ps = [pos(ib * SUBLANES + r) for r in rs]
        gc_p = [_row_bcast(gc_ref.at[0], p) for p in ps]
        beta_p = [_row_bcast(beta_ref.at[0], p) for p in ps]

        def weights_offdiag(bc, c):
            b = blk(bc)
            gc_c = gc_ref[0, cols_ds(b), :]
            for r in rs:
                dec = jnp.exp(gc_p[r] - gc_c)
                l_s[tile_ds(ps[r], b), :] = beta_p[r] * g_s[tile_ds(ps[r], b), :] * dec
                at_s[tile_ds(ps[r], b), :] = qk_s[tile_ds(ps[r], b), :] * dec
            return c

        lax.fori_loop(0, ib, weights_offdiag, 0)
        gc_c = gc_ref[0, cols_ds(b_own), :]
        for r in rs:
            t = tile_ds(ps[r], b_own)
            dec = jnp.exp(gc_p[r] - gc_c)
            l_s[t, :] = jnp.where(earlier[r], beta_p[r] * g_s[t, :] * dec, 0.0)
            at_s[t, :] = jnp.where(sub == own[r], qk_s[t, :], jnp.where(earlier[r], qk_s[t, :] * dec, 0.0))
            de_s[t, :] = jnp.where(sub == own[r], _row_bcast(e_s, ps[r]), 0.0)

        def finish(b, acc):
            done = []
            for r in rs:
                a_r = acc[r]
                for kk in range(r):
                    a_r = a_r - _row_bcast(l_s, ps[r] * CHUNK + ps[kk]) * done[kk]
                done.append(a_r)
            beta_c = beta_ref[0, cols_ds(b), :]
            be_c = beta_c * e_s[cols_ds(b), :]
            for r in rs:
                t_s[tile_ds(ps[r], b), :] = done[r]
                tu_s[tile_ds(ps[r], b), :] = done[r] * beta_c
                tw_s[tile_ds(ps[r], b), :] = done[r] * be_c

        def subst_offdiag(bc, c):
            b = blk(bc)

            def k_block(kb, acc):
                acc = list(acc)
                for kk in rs:
                    pk = pos(kb * SUBLANES + kk)
                    t_k = t_s[tile_ds(pk, b), :]
                    for r in rs:
                        acc[r] = acc[r] - _row_bcast(l_s, ps[r] * CHUNK + pk) * t_k
                return tuple(acc)

            finish(b, lax.fori_loop(bc, ib, k_block, (zero,) * SUBLANES))
            return c

        lax.fori_loop(0, ib, subst_offdiag, 0)
        finish(b_own, [jnp.where(sub == own[r], 1.0, 0.0) for r in rs])
        return carry

    lax.fori_loop(0, NB, row_block, 0)

    for p in range(CHUNK):
        rows = slice(p * CHUNK, (p + 1) * CHUNK)
        sl = slice(p * LANES, (p + 1) * LANES)
        twtu_ref[:, sl] = jnp.concatenate([tw_s[rows, :], tu_s[rows, :]], axis=0).T.astype(BF16)
        atde_ref[:, sl] = jnp.concatenate([at_s[rows, :], de_s[rows, :]], axis=0).T.astype(BF16)


def _gdn_inv_call(g_flat, qk_flat, beta_s, gc_s):
    u = g_flat.shape[0]
    mat = pl.BlockSpec((UNITS, CHUNK * CHUNK), lambda d, i: (i, 0))
    sc = pl.BlockSpec((1, CHUNK, UNITS), lambda d, i: (d, 0, i))
    out = pl.BlockSpec((1, UNITS, CHUNK * LANES), lambda d, i: (d, i, 0))
    soa = pltpu.VMEM((CHUNK * CHUNK, UNITS), F32)
    return pl.pallas_call(
        _gdn_inv_kernel,
        out_shape=(jax.ShapeDtypeStruct((2, u, CHUNK * LANES), BF16),
                   jax.ShapeDtypeStruct((2, u, CHUNK * LANES), BF16)),
        grid=(2, u // UNITS),
        in_specs=[mat, mat, sc, sc],
        out_specs=(out, out),
        scratch_shapes=[soa] * 8 + [pltpu.VMEM((CHUNK, UNITS), F32)],
        compiler_params=_cparams(("arbitrary", "arbitrary")),
        name="gdn_inv",
    )(g_flat, qk_flat, beta_s, gc_s)


def _gdn_scan_kernel(hb, cb, egl_ref, q_ref, k_ref, v_ref, kt_ref, twtu_ref, atde_ref, ep_ref, s0_ref,
                     o_ref, sfin_ref, s_scr, w_scr, u_scr):
    d = pl.program_id(0)
    hg = pl.program_id(1)
    ci = pl.program_id(2)
    ncb = pl.num_programs(2)

    @pl.when(ci == 0)
    def _():
        s_scr[...] = s0_ref[0]

    cblk = ci + d * (ncb - 1 - 2 * ci)
    zeros = jnp.zeros((CHUNK, HEAD_DIM), BF16)

    def prepare(c, carry):
        r0 = pl.multiple_of(c * CHUNK, CHUNK)
        for hh in range(hb):
            k = k_ref[hh, pl.ds(r0, CHUNK), :]
            v = v_ref[hh, pl.ds(r0, CHUNK), :]
            rhs = jnp.concatenate([jnp.concatenate([k, zeros], axis=1),
                                   jnp.concatenate([zeros, v], axis=1)], axis=0)
            wu = jnp.dot(twtu_ref[0, hh, c], rhs, preferred_element_type=F32)
            w_scr[c, hh] = wu[:, :HEAD_DIM].astype(BF16)
            u_scr[c, hh] = wu[:, HEAD_DIM:]
        return carry

    lax.fori_loop(0, cb, prepare, 0)

    def chunk_step(cc, carry):
        c = cc + d * (cb - 1 - 2 * cc)
        r0 = pl.multiple_of(c * CHUNK, CHUNK)
        heads = range(hb)
        s = [s_scr[hh] for hh in heads]
        x = [jnp.dot(jnp.concatenate([w_scr[c, hh], q_ref[hh, pl.ds(r0, CHUNK), :]], axis=0),
                     s[hh].astype(BF16), preferred_element_type=F32) for hh in heads]
        v_new = [u_scr[c, hh] - x[hh][:CHUNK] for hh in heads]
        for hh in heads:
            kdt = (kt_ref[hh, c].astype(F32) * ep_ref[0, hh, pl.ds(c, 1), :]).astype(BF16)
            egl = egl_ref[d, hg * hb + hh, cblk * cb + c]
            s_scr[hh] = egl * s[hh] + jnp.dot(kdt, v_new[hh].astype(BF16), preferred_element_type=F32)
        for hh in heads:
            rhs = jnp.concatenate([v_new[hh], x[hh][CHUNK:]], axis=0).astype(BF16)
            o_ref[0, hh, pl.ds(r0, CHUNK), :] = jnp.dot(atde_ref[0, hh, c], rhs,
                                                        preferred_element_type=F32).astype(o_ref.dtype)
        return carry

    lax.fori_loop(0, cb, chunk_step, 0)

    @pl.when(ci == ncb - 1)
    def _():
        sfin_ref[0] = s_scr[...]


def _gdn_scan_call(qkv, kt, twtu, atde, ep, egl, s0, nc):
    t = qkv.shape[1]
    hb = HEADS
    cb = min(8, nc)
    ncb = nc // cb
    hgs = HEADS // hb
    nat = lambda d, c: c + d * (ncb - 1 - 2 * c)
    tok = lambda part: pl.BlockSpec((hb, cb * CHUNK, HEAD_DIM),
                                    lambda d, hg, c, egl, part=part: (part * hgs + hg, nat(d, c), 0))
    per_dir = lambda last: pl.BlockSpec((1, hb, cb, CHUNK, last),
                                        lambda d, hg, c, egl: (d, hg, nat(d, c), 0, 0))
    grid_spec = pltpu.PrefetchScalarGridSpec(
        num_scalar_prefetch=1,
        grid=(2, hgs, ncb),
        in_specs=[tok(0), tok(1), tok(2),
                  pl.BlockSpec((hb, cb, HEAD_DIM, CHUNK), lambda d, hg, c, egl: (hg, nat(d, c), 0, 0)),
                  per_dir(LANES), per_dir(LANES),
                  pl.BlockSpec((1, hb, cb, CHUNK), lambda d, hg, c, egl: (d, hg, nat(d, c), 0)),
                  pl.BlockSpec((1, hb, HEAD_DIM, HEAD_DIM), lambda d, hg, c, egl: (d, hg, 0, 0))],
        out_specs=(pl.BlockSpec((1, hb, cb * CHUNK, HEAD_DIM), lambda d, hg, c, egl: (d, hg, nat(d, c), 0)),
                   pl.BlockSpec((1, hb, HEAD_DIM, HEAD_DIM), lambda d, hg, c, egl: (d, hg, 0, 0))),
        scratch_shapes=[pltpu.VMEM((hb, HEAD_DIM, HEAD_DIM), F32),
                        pltpu.VMEM((cb, hb, CHUNK, HEAD_DIM), BF16),
                        pltpu.VMEM((cb, hb, CHUNK, HEAD_DIM), F32)],
    )
    return pl.pallas_call(
        functools.partial(_gdn_scan_kernel, hb, cb),
        out_shape=(jax.ShapeDtypeStruct((2, HEADS, t, HEAD_DIM), BF16),
                   jax.ShapeDtypeStruct((2, HEADS, HEAD_DIM, HEAD_DIM), F32)),
        grid_spec=grid_spec,
        compiler_params=_cparams(("arbitrary", "arbitrary", "arbitrary")),
        name="gdn_scan",
    )(egl, qkv, qkv, qkv, kt.reshape(HEADS, nc, HEAD_DIM, CHUNK),
      twtu.reshape(2, HEADS, nc, CHUNK, LANES), atde.reshape(2, HEADS, nc, CHUNK, LANES), ep, s0)


def _gdn(qkv, bg, s0):
    t = qkv.shape[1]
    nc = t // CHUNK
    u = HEADS * nc
    up = -(-u // UNITS) * UNITS
    to_soa = lambda a: jnp.pad(jnp.transpose(a.reshape(nc, CHUNK, 2, HEADS), (2, 1, 3, 0)).reshape(2, CHUNK, u),
                               ((0, 0), (0, 0), (0, up - u)))
    beta_s = to_soa(bg[:, :2 * HEADS])
    g_s = to_soa(bg[:, 2 * HEADS:4 * HEADS])
    gc_s, ep_s, egl_s = _gdn_scal_call(g_s)
    gram, qk, kt = _gdn_gram_call(qkv, nc)
    pad_u = lambda a: jnp.pad(a.reshape(u, -1), ((0, up - u), (0, 0)))
    g_flat, qk_flat = pad_u(gram), pad_u(qk)
    twtu, atde = (a[:, :u] for a in _gdn_inv_call(g_flat, qk_flat, beta_s, gc_s))
    ep = jnp.transpose(ep_s[:, :, :u].reshape(2, CHUNK, HEADS, nc), (0, 2, 3, 1))
    egl = egl_s[:, 0, :u].reshape(2, HEADS, nc)
    return _gdn_scan_call(qkv, kt, twtu, atde, ep, egl, s0, nc)


def _merge_kernel(o_ref, zs_ref, yc_ref, ga_ref, gb_ref, wa_ref, wb_ref, m_ref, yg_s):
    @pl.when(pl.program_id(1) == 0)
    def _():
        for hh in range(HEADS):
            o = o_ref[0, hh].astype(F32) + o_ref[1, hh].astype(F32)
            y = o * lax.rsqrt(jnp.mean(o * o, axis=-1, keepdims=True) + RMS_EPS) * zs_ref[hh]
            yg_s[:, hh * HEAD_DIM:(hh + 1) * HEAD_DIM] = y.astype(BF16)

    pa = jnp.dot(yg_s[...], wa_ref[...], preferred_element_type=F32)
    pb = jnp.dot(yc_ref[...], wb_ref[...], preferred_element_type=F32)
    m_ref[...] = (ga_ref[...].astype(F32) * pa + gb_ref[...].astype(F32) * pb).astype(m_ref.dtype)


def _merge_call(o, zs, yconv, gates, wa, wb):
    t = yconv.shape[0]
    tm, tn = 512, 1024
    nj = D_MODEL // tn
    return pl.pallas_call(
        _merge_kernel,
        out_shape=jax.ShapeDtypeStruct((t, D_MODEL), BF16),
        grid=(t // tm, nj),
        in_specs=[pl.BlockSpec((2, HEADS, tm, HEAD_DIM), lambda i, j: (0, 0, i, 0)),
                  pl.BlockSpec((HEADS, tm, HEAD_DIM), lambda i, j: (0, i, 0)),
                  pl.BlockSpec((tm, CONV_WIDTH), lambda i, j: (i, 0)),
                  pl.BlockSpec((tm, tn), lambda i, j: (i, j)),
                  pl.BlockSpec((tm, tn), lambda i, j: (i, nj + j)),
                  pl.BlockSpec((GDN_WIDTH, tn), lambda i, j: (0, j)),
                  pl.BlockSpec((CONV_WIDTH, tn), lambda i, j: (0, j))],
        out_specs=pl.BlockSpec((tm, tn), lambda i, j: (i, j)),
        scratch_shapes=[pltpu.VMEM((tm, GDN_WIDTH), BF16)],
        compiler_params=_cparams(("arbitrary", "arbitrary")),
        name="merge",
    )(o, zs, yconv, gates, gates, wa, wb)


LN_ROWS = 64


def _outmm_kernel(m_ref, w_ref, o_ref):
    o_ref[...] = jnp.dot(m_ref[...], w_ref[...], preferred_element_type=F32)


def _outmm_call(m, w_out):
    t = m.shape[0]
    tm, tn = 512, 1024
    return pl.pallas_call(
        _outmm_kernel,
        out_shape=jax.ShapeDtypeStruct((t, D_MODEL), F32),
        grid=(D_MODEL // tn, t // tm),
        in_specs=[pl.BlockSpec((tm, D_MODEL), lambda j, i: (i, 0)),
                  pl.BlockSpec((D_MODEL, tn), lambda j, i: (0, j))],
        out_specs=pl.BlockSpec((tm, tn), lambda j, i: (i, j)),
        compiler_params=_cparams(("arbitrary", "arbitrary")),
        name="out_proj",
    )(m, w_out)


HALF = D_MODEL // 2


def _bf16_bits(v):
    return pltpu.bitcast(v.astype(BF16).astype(F32), jnp.uint32)


def _post_mix_kernel(mix_ref, x_ref, mod_ref, ln_ref, wrh_ref, wrl_ref, x1_ref, tokp_ref, logit_ref, tok_s):
    def rows_step(ci, c):
        sl = pl.ds(pl.multiple_of(ci * LN_ROWS, LN_ROWS), LN_ROWS)
        x1 = _layer_norm(DEEPNORM_ALPHA * x_ref[sl, :] + mod_ref[0:1, :] * mix_ref[sl, :])
        x1 = x1 * ln_ref[0:1, :] + ln_ref[1:2, :]
        x1_ref[sl, :] = x1
        tok = _layer_norm(x1) * (1.0 + mod_ref[2:3, :]) + mod_ref[1:2, :]
        tok_s[sl, :] = tok
        tokp_ref[sl, :] = _bf16_bits(tok[:, :HALF]) | (_bf16_bits(tok[:, HALF:]) >> 16)
        return c

    lax.fori_loop(0, x_ref.shape[0] // LN_ROWS, rows_step, 0)
    tok = tok_s[...]
    t_hi = tok.astype(BF16)
    t_lo = (tok - t_hi.astype(F32)).astype(BF16)
    logit_ref[...] = (jnp.dot(t_hi, wrh_ref[...], preferred_element_type=F32)
                      + jnp.dot(t_lo, wrh_ref[...], preferred_element_type=F32)
                      + jnp.dot(t_hi, wrl_ref[...], preferred_element_type=F32))


def _post_mix_call(mix, x, mod3, ln1, w_router):
    wr_hi = w_router.astype(BF16)
    wr_lo = (w_router - wr_hi.astype(F32)).astype(BF16)
    t = x.shape[0]
    tm = 256
    row = pl.BlockSpec((tm, D_MODEL), lambda i: (i, 0))
    vec = pl.BlockSpec((SUBLANES, D_MODEL), lambda i: (0, 0))
    wr = pl.BlockSpec((D_MODEL, LANES), lambda i: (0, 0))
    return pl.pallas_call(
        _post_mix_kernel,
        out_shape=(jax.ShapeDtypeStruct((t, D_MODEL), F32),
                   jax.ShapeDtypeStruct((t, HALF), jnp.uint32),
                   jax.ShapeDtypeStruct((t, LANES), F32)),
        grid=(t // tm,),
        in_specs=[row, row, vec, vec, wr, wr],
        out_specs=(row, pl.BlockSpec((tm, HALF), lambda i: (i, 0)), pl.BlockSpec((tm, LANES), lambda i: (i, 0))),
        scratch_shapes=[pltpu.VMEM((tm, D_MODEL), F32)],
        compiler_params=_cparams(("arbitrary",)),
        name="post_mix",
    )(mix, x, mod3, ln1, wr_hi, wr_lo)


MOE_BM = 512
MOE_HC = 256


def _gather_kernel(rows, nu_ref, idx_ref, nxt_ref, src_ref, dst_ref, buf, sem):
    i = pl.program_id(0)
    nu = nu_ref[0]
    slot = i % 2

    def copy(iref, s, r):
        return pltpu.make_async_copy(src_ref.at[pl.ds(iref[0, 0, r], 1), :],
                                     buf.at[s, pl.ds(r, 1), :], sem.at[s])

    def issue(iref, s):
        def body(r, c):
            copy(iref, s, r).start()
            return c
        lax.fori_loop(0, rows, body, 0, unroll=8)

    @pl.when(i == 0)
    def _():
        issue(idx_ref, 0)

    @pl.when(i + 1 < nu)
    def _():
        issue(nxt_ref, 1 - slot)

    @pl.when(i < nu)
    def _():
        def body(r, c):
            copy(idx_ref, slot, r).wait()
            return c
        lax.fori_loop(0, rows, body, 0, unroll=8)
        packed = buf[slot]
        dst_ref[:, :HALF] = pltpu.bitcast(packed & jnp.uint32(0xFFFF0000), F32).astype(BF16)
        dst_ref[:, HALF:] = pltpu.bitcast(packed << 16, F32).astype(BF16)

    @pl.when(i >= nu)
    def _():
        dst_ref[...] = jnp.zeros(dst_ref.shape, dst_ref.dtype)


def _gather_call(src, idx, n_used):
    n = idx.shape[0]
    rows = MOE_BM
    nblk = n // rows
    idx3 = idx.reshape(nblk, 1, rows)
    grid_spec = pltpu.PrefetchScalarGridSpec(
        num_scalar_prefetch=1,
        grid=(nblk,),
        in_specs=[pl.BlockSpec((1, 1, rows), lambda i, nu: (i, 0, 0), memory_space=pltpu.SMEM),
                  pl.BlockSpec((1, 1, rows), lambda i, nu: (jnp.minimum(i + 1, nblk - 1), 0, 0),
                               memory_space=pltpu.SMEM),
                  pl.BlockSpec(memory_space=pl.ANY)],
        out_specs=pl.BlockSpec((rows, D_MODEL), lambda i, nu: (i, 0)),
        scratch_shapes=[pltpu.VMEM((2, rows, src.shape[1]), src.dtype), pltpu.SemaphoreType.DMA((2,))],
    )
    return pl.pallas_call(
        functools.partial(_gather_kernel, rows),
        out_shape=jax.ShapeDtypeStruct((n, D_MODEL), BF16),
        grid_spec=grid_spec,
        compiler_params=_cparams(("arbitrary",)),
        name="moe_gather",
    )(n_used, idx3, idx3, src)


def _expert_changed(be_ref, nu_ref, b):
    bb = jnp.minimum(b, nu_ref[0] - 1)
    return (b == 0) | (be_ref[bb] != be_ref[jnp.maximum(bb - 1, 0)])


def _moe_hidden_kernel(be_ref, nu_ref, x_ref, wg_ref, wu_ref, h_ref, w_s):
    b = pl.program_id(0)
    hc = pl.program_id(1)
    live = b < nu_ref[0]

    @pl.when(live & _expert_changed(be_ref, nu_ref, b))
    def _():
        w_s[hc, :, :MOE_HC] = wg_ref[0].astype(BF16)
        w_s[hc, :, MOE_HC:] = wu_ref[0].astype(BF16)

    @pl.when(live)
    def _():
        gu = jnp.dot(x_ref[...], w_s[hc], preferred_element_type=F32)
        h_ref[...] = (_silu(gu[:, :MOE_HC]) * gu[:, MOE_HC:]).astype(h_ref.dtype)

    @pl.when(jnp.logical_not(live))
    def _():
        h_ref[...] = jnp.zeros(h_ref.shape, h_ref.dtype)


def _moe_down_kernel(be_ref, nu_ref, h_ref, wd_ref, y_ref, wd_s):
    b = pl.program_id(0)
    live = b < nu_ref[0]

    @pl.when(live & _expert_changed(be_ref, nu_ref, b))
    def _():
        wd_s[...] = wd_ref[0].astype(BF16)

    @pl.when(live)
    def _():
        y_ref[...] = jnp.dot(h_ref[...], wd_s[...], preferred_element_type=F32)

    @pl.when(jnp.logical_not(live))
    def _():
        y_ref[...] = jnp.zeros(y_ref.shape, F32)


def _expert_call(xs, wg, wu, wd, block_e, n_used):
    n = xs.shape[0]
    nb = n // MOE_BM
    nh = EXPERT_HIDDEN // MOE_HC
    blk = lambda b, nu: jnp.minimum(b, nu[0] - 1)

    def w_chunk(b, h, be, nu):
        bb = blk(b, nu)
        first = (b < nu[0]) & ((bb == 0) | (be[bb] != be[jnp.maximum(bb - 1, 0)]))
        return be[bb], 0, jnp.where(first, h, nh - 1)

    hid = pl.pallas_call(
        _moe_hidden_kernel,
        out_shape=jax.ShapeDtypeStruct((n, EXPERT_HIDDEN), BF16),
        grid_spec=pltpu.PrefetchScalarGridSpec(
            num_scalar_prefetch=2,
            grid=(nb, nh),
            in_specs=[pl.BlockSpec((MOE_BM, D_MODEL), lambda b, h, be, nu: (blk(b, nu), 0)),
                      pl.BlockSpec((1, D_MODEL, MOE_HC), w_chunk),
                      pl.BlockSpec((1, D_MODEL, MOE_HC), w_chunk)],
            out_specs=pl.BlockSpec((MOE_BM, MOE_HC), lambda b, h, be, nu: (b, h)),
            scratch_shapes=[pltpu.VMEM((nh, D_MODEL, 2 * MOE_HC), BF16)]),
        compiler_params=_cparams(("arbitrary", "arbitrary")),
        name="moe_hidden",
    )(block_e, n_used, xs, wg, wu)
    return pl.pallas_call(
        _moe_down_kernel,
        out_shape=jax.ShapeDtypeStruct((n, D_MODEL), F32),
        grid_spec=pltpu.PrefetchScalarGridSpec(
            num_scalar_prefetch=2,
            grid=(nb,),
            in_specs=[pl.BlockSpec((MOE_BM, EXPERT_HIDDEN), lambda b, be, nu: (blk(b, nu), 0)),
                      pl.BlockSpec((1, EXPERT_HIDDEN, D_MODEL), lambda b, be, nu: (be[blk(b, nu)], 0, 0))],
            out_specs=pl.BlockSpec((MOE_BM, D_MODEL), lambda b, be, nu: (b, 0)),
            scratch_shapes=[pltpu.VMEM((EXPERT_HIDDEN, D_MODEL), BF16)]),
        compiler_params=_cparams(("arbitrary",)),
        name="moe_down",
    )(block_e, n_used, hid, wd)


def _combine_kernel(rows, d_ref, nxt_ref, y_ref, x1_ref, w_ref, mod_ref, ln_ref, o_ref, buf, sem):
    i = pl.program_id(0)
    slot = i % 2

    def copy(iref, s, k, r):
        return pltpu.make_async_copy(y_ref.at[pl.ds(iref[0, k, r], 1), :],
                                     buf.at[s, k, pl.ds(r, 1), :], sem.at[s])

    def issue(iref, s):
        def body(r, c):
            copy(iref, s, 0, r).start()
            copy(iref, s, 1, r).start()
            return c
        lax.fori_loop(0, rows, body, 0, unroll=8)

    @pl.when(i == 0)
    def _():
        issue(d_ref, 0)

    @pl.when(i + 1 < pl.num_programs(0))
    def _():
        issue(nxt_ref, 1 - slot)

    def wait_body(r, c):
        copy(d_ref, slot, 0, r).wait()
        copy(d_ref, slot, 1, r).wait()
        return c

    lax.fori_loop(0, rows, wait_body, 0, unroll=8)

    def rows_step(ci, c):
        sl = pl.ds(pl.multiple_of(ci * LN_ROWS, LN_ROWS), LN_ROWS)
        f = w_ref[sl, 0:1] * buf[slot, 0, sl, :] + w_ref[sl, 1:2] * buf[slot, 1, sl, :]
        y = _layer_norm(DEEPNORM_ALPHA * x1_ref[sl, :] + mod_ref[0:1, :] * f)
        o_ref[sl, :] = y * ln_ref[0:1, :] + ln_ref[1:2, :]
        return c

    lax.fori_loop(0, rows // LN_ROWS, rows_step, 0)


def _combine_call(ys, dest, weights, x1, mod_row, ln2):
    t = x1.shape[0]
    rows = 128
    nblk = t // rows
    dest3 = jnp.transpose(dest.reshape(nblk, rows, 2), (0, 2, 1))
    return pl.pallas_call(
        functools.partial(_combine_kernel, rows),
        out_shape=jax.ShapeDtypeStruct((t, D_MODEL), F32),
        grid=(nblk,),
        in_specs=[pl.BlockSpec((1, 2, rows), lambda i: (i, 0, 0), memory_space=pltpu.SMEM),
                  pl.BlockSpec((1, 2, rows), lambda i: (jnp.minimum(i + 1, nblk - 1), 0, 0),
                               memory_space=pltpu.SMEM),
                  pl.BlockSpec(memory_space=pl.ANY),
                  pl.BlockSpec((rows, D_MODEL), lambda i: (i, 0)),
                  pl.BlockSpec((rows, 2), lambda i: (i, 0)),
                  pl.BlockSpec((SUBLANES, D_MODEL), lambda i: (0, 0)),
                  pl.BlockSpec((SUBLANES, D_MODEL), lambda i: (0, 0))],
        out_specs=pl.BlockSpec((rows, D_MODEL), lambda i: (i, 0)),
        scratch_shapes=[pltpu.VMEM((2, 2, rows, D_MODEL), F32), pltpu.SemaphoreType.DMA((2,))],
        compiler_params=_cparams(("arbitrary",)),
        name="moe_combine",
    )(dest3, dest3, ys, x1, weights, mod_row, ln2)


def _route(logits, b_group, b_expert):
    t = logits.shape[0]
    p_group = jax.nn.softmax(logits[:, :N_GROUPS] + b_group, axis=-1)
    group = jnp.argmax(p_group, axis=-1)
    gate_group = jnp.take_along_axis(p_group, group[:, None], axis=-1)
    le = (logits[:, N_GROUPS:N_GROUPS + N_EXPERTS] + b_expert).reshape(t, N_GROUPS, EXPERTS_PER_GROUP)
    le = jnp.take_along_axis(le, group[:, None, None], axis=1)[:, 0]
    top_p, top_i = lax.top_k(jax.nn.softmax(le, axis=-1), 2)
    weights = gate_group * top_p / jnp.sum(top_p, axis=-1, keepdims=True)
    expert_id = group[:, None] * EXPERTS_PER_GROUP + top_i
    return expert_id.astype(jnp.int32), weights


def _moe(tok, logits, b_group, b_expert, wg, wu, wd, x1, mod_row, ln2):
    t = tok.shape[0]
    expert_id, weights = _route(logits, b_group, b_expert)
    e_flat = expert_id.reshape(-1)
    n_assign = e_flat.shape[0]
    onehot = (e_flat[:, None] == jnp.arange(N_EXPERTS)[None, :]).astype(jnp.int32)
    rank = jnp.take_along_axis(jnp.cumsum(onehot, axis=0), e_flat[:, None], axis=1)[:, 0] - 1
    counts = jnp.sum(onehot, axis=0)
    padded = (counts + MOE_BM - 1) // MOE_BM * MOE_BM
    pad_end = jnp.cumsum(padded)
    dest = (pad_end - padded)[e_flat] + rank
    n_blocks = -(-n_assign // MOE_BM) + N_EXPERTS
    n_rows = n_blocks * MOE_BM
    src_tok = jnp.zeros((n_rows,), jnp.int32).at[dest].set(jnp.arange(n_assign, dtype=jnp.int32) // 2)
    block_e = jnp.minimum(jnp.searchsorted(pad_end, jnp.arange(n_blocks) * MOE_BM, side="right"),
                          N_EXPERTS - 1).astype(jnp.int32)
    n_used = (pad_end[-1] // MOE_BM).astype(jnp.int32).reshape(1)
    xs = _gather_call(tok, src_tok, n_used)
    ys = _expert_call(xs, wg, wu, wd, block_e, n_used)
    return _combine_call(ys, dest.reshape(t, 2).astype(jnp.int32), weights, x1, mod_row, ln2)


def _pad_rows(v, rows=SUBLANES):
    return jnp.pad(v, ((0, rows - v.shape[0]), (0, 0)))


def _layer(x, ctx, c, c_ctx, w_ada, b_ada, w_in, conv_qkv, a_log, dt_bias, gdn_norm_w, conv_b,
           w_branch_a, w_branch_b, w_out, ln1_g, ln1_b, w_router_group, b_router_group,
           w_router_expert, b_router_expert, w_exp_gate, w_exp_up, w_exp_down, ln2_g, ln2_b):
    t = x.shape[0]
    mod = _mod_call(_pad_rows(jnp.stack([c, c_ctx])), w_ada, b_ada.reshape(1, -1))
    mod_lat = mod[0].reshape(N_MOD, D_MODEL)
    mod_ctx = mod[1].reshape(N_MOD, D_MODEL)

    n_a = 4 * GDN_WIDTH
    n_ba = 4 * HEADS
    w_pa = w_in[:, :n_a].astype(BF16)
    w_pb = w_in[:, n_a + n_ba:].astype(BF16)
    w_pc = jnp.pad(w_in[:, n_a:n_a + n_ba], ((0, 0), (0, LANES - n_ba))).astype(BF16)
    lane_row = lambda v: jnp.pad(v.reshape(1, -1), ((0, 0), (2 * HEADS, LANES - 4 * HEADS)))
    alog_row, dtb_row = lane_row(a_log), lane_row(dt_bias)

    h_ctx = _ln_mod_call(ctx, mod_ctx[0:1], mod_ctx[1:2])
    tc = ctx.shape[0]
    qkv_c = _inproj_qkv(h_ctx, w_pa, conv_qkv, tc, tc)
    bg_c = _inproj_ba(h_ctx, w_pc, alog_row, dtb_row, tc)
    s0 = jnp.zeros((2, HEADS, HEAD_DIM, HEAD_DIM), F32)
    _, s_ctx = _gdn(qkv_c, bg_c, s0)

    tm = 512
    h = _ln_mod_call(x, mod_lat[0:1], mod_lat[1:2])
    qkv = _inproj_qkv(h, w_pa, conv_qkv, GRID_W, tm)
    zs = _inproj_z(h, w_pa, gdn_norm_w.reshape(1, HEAD_DIM), tm)
    yconv = _inproj_xbc(h, w_pb, conv_b, GRID_W, tm)
    gates = _inproj_gates(h, w_pb, tm)
    bg = _inproj_ba(h, w_pc, alog_row, dtb_row, tm)
    o, _ = _gdn(qkv, bg, s_ctx)
    m = _merge_call(o, zs, yconv, gates, w_branch_a.astype(BF16), w_branch_b.astype(BF16))

    w_router = jnp.pad(jnp.concatenate([w_router_group, w_router_expert], axis=1),
                       ((0, 0), (0, LANES - N_GROUPS - N_EXPERTS)))
    mix = _outmm_call(m, w_out.astype(BF16))
    x1, tok, logits = _post_mix_call(mix, x, _pad_rows(mod_lat[2:5]),
                                     _pad_rows(jnp.stack([ln1_g, ln1_b])), w_router)

    return _moe(tok, logits, b_router_group, b_router_expert, w_exp_gate, w_exp_up, w_exp_down,
                x1, _pad_rows(mod_lat[5:6]), _pad_rows(jnp.stack([ln2_g, ln2_b])))


def kernel(x, c, ctx, c_ctx, w_ada, b_ada, w_in, conv_qkv, a_log, dt_bias, gdn_norm_w, conv_b,
           w_branch_a, w_branch_b, w_out, ln1_g, ln1_b, w_router_group, b_router_group,
           w_router_expert, b_router_expert, w_exp_gate, w_exp_up, w_exp_down, ln2_g, ln2_b):
    assert x.shape[0] == 1 and w_ada.shape[0] == 1, "single batch element, single layer"
    out = _layer(x[0], ctx[0], c[0], c_ctx, w_ada[0], b_ada[0], w_in[0], conv_qkv[0],
                 a_log[0].reshape(-1), dt_bias[0].reshape(-1), gdn_norm_w[0], conv_b[0],
                 w_branch_a[0], w_branch_b[0], w_out[0], ln1_g[0], ln1_b[0],
                 w_router_group[0], b_router_group[0], w_router_expert[0], b_router_expert[0],
                 w_exp_gate[0], w_exp_up[0], w_exp_down[0], ln2_g[0], ln2_b[0])
    return out[None]
```

```python
import functools

import numpy as np
import jax
import jax.numpy as jnp
from jax import lax
from jax.experimental import pallas as pl
from jax.experimental.pallas import tpu as pltpu

F32 = jnp.float32
BF16 = jnp.bfloat16
HIGHEST = lax.Precision.HIGHEST

D_MODEL = 4096
GRID_W = 64
CHUNK = 64
HEADS = 16
HEAD_DIM = 128
GDN_WIDTH = HEADS * HEAD_DIM
CONV_WIDTH = D_MODEL // 2
N_GROUPS = 8
EXPERTS_PER_GROUP = 8
N_EXPERTS = 64
EXPERT_HIDDEN = 768
N_MOD = 6
DEEPNORM_ALPHA = 2.0 ** 0.25
LN_EPS = 1e-6
RMS_EPS = 1e-6

LANES = 128
SUBLANES = 8
UNITS = 128
VMEM_LIMIT = 52 * 1024 * 1024


def _cparams(sem, vmem=VMEM_LIMIT):
    return pltpu.CompilerParams(dimension_semantics=sem, vmem_limit_bytes=vmem)


def _silu(x):
    return x * jax.nn.sigmoid(x)


def _layer_norm(x):
    mu = jnp.mean(x, axis=-1, keepdims=True)
    xc = x - mu
    var = jnp.mean(xc * xc, axis=-1, keepdims=True)
    return xc * lax.rsqrt(var + LN_EPS)


def _mod_kernel(c_ref, w_ref, b_ref, o_ref):
    s = _silu(c_ref[...])
    o_ref[...] = jnp.dot(s, w_ref[...], preferred_element_type=F32, precision=HIGHEST) + b_ref[...]


def _mod_call(cs, w_ada, b_ada):
    n = w_ada.shape[1]
    tn = 512
    return pl.pallas_call(
        _mod_kernel,
        out_shape=jax.ShapeDtypeStruct((SUBLANES, n), F32),
        grid=(n // tn,),
        in_specs=[pl.BlockSpec((SUBLANES, D_MODEL), lambda j: (0, 0)),
                  pl.BlockSpec((D_MODEL, tn), lambda j: (0, j)),
                  pl.BlockSpec((1, tn), lambda j: (0, j))],
        out_specs=pl.BlockSpec((SUBLANES, tn), lambda j: (0, j)),
        compiler_params=_cparams(("arbitrary",)),
        name="mod",
    )(cs, w_ada, b_ada)


def _ln_mod_kernel(x_ref, shift_ref, scale_ref, o_ref):
    y = _layer_norm(x_ref[...])
    o_ref[...] = (y * (1.0 + scale_ref[...]) + shift_ref[...]).astype(o_ref.dtype)


def _ln_mod_call(x, shift, scale):
    t = x.shape[0]
    tr = min(256, t)
    return pl.pallas_call(
        _ln_mod_kernel,
        out_shape=jax.ShapeDtypeStruct((t, D_MODEL), BF16),
        grid=(t // tr,),
        in_specs=[pl.BlockSpec((tr, D_MODEL), lambda i: (i, 0)),
                  pl.BlockSpec((1, D_MODEL), lambda i: (0, 0)),
                  pl.BlockSpec((1, D_MODEL), lambda i: (0, 0))],
        out_specs=pl.BlockSpec((tr, D_MODEL), lambda i: (i, 0)),
        compiler_params=_cparams(("arbitrary",)),
        name="ln_mod",
    )(x, shift, scale)


COL_QKV = 0
COL_Z = 3 * GDN_WIDTH
COL_XB = 0
COL_BG = CONV_WIDTH
COL_CG = 2 * CONV_WIDTH
COL_GATES = 3 * CONV_WIDTH
COL_BA = 0


def _conv3_rows(x, taps_ref, group):
    rows = x.shape[0]
    pos = lax.broadcasted_iota(jnp.int32, x.shape, 0) % group
    prev = jnp.where(pos == 0, 0.0, pltpu.roll(x, 1, 0))
    nxt = jnp.where(pos == group - 1, 0.0, pltpu.roll(x, rows - 1, 0))
    return taps_ref[0:1, :] * prev + taps_ref[1:2, :] * x + taps_ref[2:3, :] * nxt


def _ip_qkv_kernel(group, x_ref, w_ref, taps_ref, o_ref):
    acc = jnp.dot(x_ref[...], w_ref[...], preferred_element_type=F32)
    y = _silu(_conv3_rows(acc, taps_ref, group))
    j = pl.program_id(0)
    tiles_per_part = GDN_WIDTH // acc.shape[1]
    for hh in range(acc.shape[1] // HEAD_DIM):
        ys = y[:, hh * HEAD_DIM:(hh + 1) * HEAD_DIM]
        r = lax.rsqrt(jnp.sum(ys * ys, axis=-1, keepdims=True) + 1e-6)
        scale = jnp.where(j < tiles_per_part, r * HEAD_DIM ** -0.5,
                          jnp.where(j < 2 * tiles_per_part, r, 1.0))
        o_ref[hh] = (ys * scale).astype(o_ref.dtype)


def _ip_z_kernel(x_ref, w_ref, nw_ref, o_ref):
    acc = jnp.dot(x_ref[...], w_ref[...], preferred_element_type=F32)
    for hh in range(acc.shape[1] // HEAD_DIM):
        zs = acc[:, hh * HEAD_DIM:(hh + 1) * HEAD_DIM]
        o_ref[hh] = _silu(zs) * nw_ref[...]


def _ip_xbc_kernel(group, x_ref, wxb_ref, wbg_ref, wcg_ref, taps_ref, o_ref):
    x = x_ref[...]
    xb = jnp.dot(x, wxb_ref[...], preferred_element_type=F32)
    bg = jnp.dot(x, wbg_ref[...], preferred_element_type=F32)
    cg = jnp.dot(x, wcg_ref[...], preferred_element_type=F32)
    o_ref[...] = (bg * _conv3_rows(cg * xb, taps_ref, group)).astype(o_ref.dtype)


def _ip_gates_kernel(x_ref, w_ref, o_ref):
    acc = jnp.dot(x_ref[...], w_ref[...], preferred_element_type=F32)
    o_ref[...] = jax.nn.sigmoid(acc).astype(o_ref.dtype)


def _ip_ba_kernel(x_ref, w_ref, alog_ref, dtb_ref, o_ref):
    acc = jnp.dot(x_ref[...], w_ref[...], preferred_element_type=F32)
    lane = lax.broadcasted_iota(jnp.int32, acc.shape, 1)
    a = acc + dtb_ref[...]
    softplus = jnp.maximum(a, 0.0) + jnp.log(1.0 + jnp.exp(-jnp.abs(a)))
    o_ref[...] = jnp.where(lane < 2 * HEADS, jax.nn.sigmoid(acc), -jnp.exp(alog_ref[...]) * softplus)


def _inproj(kernel, h, w_cat, col_starts, tn, n_tiles, extra, extra_specs, out_shape, out_spec, tm):
    t = h.shape[0]
    w_specs = [pl.BlockSpec((D_MODEL, tn), functools.partial(lambda j, i, o: (0, o + j), o=c // tn))
               for c in col_starts]
    return pl.pallas_call(
        kernel,
        out_shape=out_shape,
        grid=(n_tiles, t // tm),
        in_specs=[pl.BlockSpec((tm, D_MODEL), lambda j, i: (i, 0))] + w_specs + extra_specs,
        out_specs=out_spec,
        compiler_params=_cparams(("arbitrary", "arbitrary")),
        name="inproj",
    )(h, *([w_cat] * len(col_starts)), *extra)


def _inproj_qkv(h, w_cat, taps, group, tm):
    t = h.shape[0]
    tn = 1024
    hpt = tn // HEAD_DIM
    return _inproj(
        functools.partial(_ip_qkv_kernel, group), h, w_cat, [COL_QKV], tn, 3 * GDN_WIDTH // tn,
        [taps], [pl.BlockSpec((3, tn), lambda j, i: (0, j))],
        jax.ShapeDtypeStruct((3 * HEADS, t, HEAD_DIM), BF16),
        pl.BlockSpec((hpt, tm, HEAD_DIM), lambda j, i: (j, i, 0)), tm)


def _inproj_z(h, w_cat, norm_w, tm):
    t = h.shape[0]
    tn = 1024
    hpt = tn // HEAD_DIM
    return _inproj(
        _ip_z_kernel, h, w_cat, [COL_Z], tn, GDN_WIDTH // tn,
        [norm_w], [pl.BlockSpec((1, HEAD_DIM), lambda j, i: (0, 0))],
        jax.ShapeDtypeStruct((HEADS, t, HEAD_DIM), F32),
        pl.BlockSpec((hpt, tm, HEAD_DIM), lambda j, i: (j, i, 0)), tm)


def _inproj_xbc(h, w_cat, taps, group, tm):
    t = h.shape[0]
    tn = 512
    return _inproj(
        functools.partial(_ip_xbc_kernel, group), h, w_cat, [COL_XB, COL_BG, COL_CG], tn,
        CONV_WIDTH // tn, [taps], [pl.BlockSpec((3, tn), lambda j, i: (0, j))],
        jax.ShapeDtypeStruct((t, CONV_WIDTH), BF16),
        pl.BlockSpec((tm, tn), lambda j, i: (i, j)), tm)


def _inproj_gates(h, w_cat, tm):
    t = h.shape[0]
    tn = 1024
    return _inproj(
        _ip_gates_kernel, h, w_cat, [COL_GATES], tn, 2 * D_MODEL // tn, [], [],
        jax.ShapeDtypeStruct((t, 2 * D_MODEL), BF16),
        pl.BlockSpec((tm, tn), lambda j, i: (i, j)), tm)


def _inproj_ba(h, w_cat, alog_row, dtb_row, tm):
    t = h.shape[0]
    return _inproj(
        _ip_ba_kernel, h, w_cat, [COL_BA], LANES, 1, [alog_row, dtb_row],
        [pl.BlockSpec((1, LANES), lambda j, i: (0, 0))] * 2,
        jax.ShapeDtypeStruct((t, LANES), F32),
        pl.BlockSpec((tm, LANES), lambda j, i: (i, 0)), tm)


def _gdn_scal_kernel(g_ref, gc_ref, ep_ref, egl_ref):
    d = pl.program_id(0)
    r = lax.broadcasted_iota(jnp.int32, (CHUNK, CHUNK), 0)
    c = lax.broadcasted_iota(jnp.int32, (CHUNK, CHUNK), 1)
    tri = jnp.where(d == 0, (c <= r).astype(F32), (c >= r).astype(F32))
    gc = jnp.dot(tri, g_ref[0], preferred_element_type=F32, precision=HIGHEST)
    gl = jnp.where(d == 0, gc[CHUNK - 1:CHUNK, :], gc[0:1, :])
    gc_ref[0] = gc
    ep_ref[0] = jnp.exp(gl - gc)
    egl_ref[0] = jnp.exp(gl)


def _gdn_scal_call(g_s):
    u = g_s.shape[2]
    tu = 512 if u % 512 == 0 else UNITS
    spec = pl.BlockSpec((1, CHUNK, tu), lambda d, i: (d, 0, i))
    return pl.pallas_call(
        _gdn_scal_kernel,
        out_shape=(jax.ShapeDtypeStruct(g_s.shape, F32), jax.ShapeDtypeStruct(g_s.shape, F32),
                   jax.ShapeDtypeStruct((2, 1, u), F32)),
        grid=(2, u // tu),
        in_specs=[spec],
        out_specs=(spec, spec, pl.BlockSpec((1, 1, tu), lambda d, i: (d, 0, i))),
        compiler_params=_cparams(("arbitrary", "arbitrary")),
        name="gdn_scal",
    )(g_s)


def _nt_dot(a, b):
    return lax.dot_general(a, b, (((1,), (1,)), ((), ())), preferred_element_type=F32)


def _gdn_gram_kernel(cb, q_ref, k_ref, g_ref, qk_ref, kt_ref):
    r = lax.broadcasted_iota(jnp.int32, (HEAD_DIM, HEAD_DIM), 0)
    c = lax.broadcasted_iota(jnp.int32, (HEAD_DIM, HEAD_DIM), 1)
    eye = (r == c).astype(BF16)
    for ci in range(cb):
        k = k_ref[0, ci * CHUNK:(ci + 1) * CHUNK, :]
        q = q_ref[0, ci * CHUNK:(ci + 1) * CHUNK, :]
        g_ref[ci] = _nt_dot(k, k)
        qk_ref[ci] = _nt_dot(q, k)
        kt_ref[ci] = _nt_dot(eye, k).astype(BF16)


def _gdn_gram_call(qkv, nc):
    cb = min(32, nc)
    ncb = nc // cb
    u = HEADS * nc
    return pl.pallas_call(
        functools.partial(_gdn_gram_kernel, cb),
        out_shape=(jax.ShapeDtypeStruct((u, CHUNK, CHUNK), F32),
                   jax.ShapeDtypeStruct((u, CHUNK, CHUNK), F32),
                   jax.ShapeDtypeStruct((u, HEAD_DIM, CHUNK), BF16)),
        grid=(HEADS, ncb),
        in_specs=[pl.BlockSpec((1, cb * CHUNK, HEAD_DIM), lambda h, c: (h, c, 0)),
                  pl.BlockSpec((1, cb * CHUNK, HEAD_DIM), lambda h, c: (HEADS + h, c, 0))],
        out_specs=(pl.BlockSpec((cb, CHUNK, CHUNK), lambda h, c: (h * ncb + c, 0, 0)),
                   pl.BlockSpec((cb, CHUNK, CHUNK), lambda h, c: (h * ncb + c, 0, 0)),
                   pl.BlockSpec((cb, HEAD_DIM, CHUNK), lambda h, c: (h * ncb + c, 0, 0))),
        compiler_params=_cparams(("arbitrary", "arbitrary")),
        name="gdn_gram",
    )(qkv, qkv)


NB = CHUNK // SUBLANES


def _row_bcast(ref, row):
    return jnp.broadcast_to(ref[pl.ds(row, 1), :], (SUBLANES, UNITS))


def _gdn_inv_kernel(g_ref, qk_ref, beta_ref, gc_ref, twtu_ref, atde_ref, *scratch):
    for bwd in (False, True):
        @pl.when(pl.program_id(0) == int(bwd))
        def _(bwd=bwd):
            _gdn_inv_body(bwd, g_ref, qk_ref, beta_ref, gc_ref, twtu_ref.at[0], atde_ref.at[0], *scratch)


def _gdn_inv_body(bwd, g_ref, qk_ref, beta_ref, gc_ref, twtu_ref, atde_ref,
                  g_s, qk_s, l_s, t_s, tw_s, tu_s, at_s, de_s, e_s):
    pos = (lambda a: CHUNK - 1 - a) if bwd else (lambda a: a)
    blk = (lambda b: NB - 1 - b) if bwd else (lambda b: b)

    for pp in range(CHUNK // 2):
        sl = slice(pp * LANES, (pp + 1) * LANES)
        g_s[sl, :] = g_ref[:, sl].T
        qk_s[sl, :] = qk_ref[:, sl].T
    e_s[...] = jnp.exp(gc_ref[0])

    @pl.when(pl.program_id(1) == 0)
    def _():
        for ref in (tw_s, tu_s, at_s, de_s):
            ref[...] = jnp.zeros(ref.shape, F32)

    sub = lax.broadcasted_iota(jnp.int32, (SUBLANES, UNITS), 0)
    zero = jnp.zeros((SUBLANES, UNITS), F32)
    rs = range(SUBLANES)
    own = [(SUBLANES - 1 - r) if bwd else r for r in rs]
    earlier = [(sub > own[r]) if bwd else (sub < own[r]) for r in rs]

    def tile_ds(p, b):
        return pl.ds(pl.multiple_of(p * CHUNK + b * SUBLANES, SUBLANES), SUBLANES)

    def cols_ds(b):
        return pl.ds(pl.multiple_of(b * SUBLANES, SUBLANES), SUBLANES)

    def row_block(ib, carry):
        b_own = blk(ib)
        ps = [pos(ib * SUBLANES + r) for r in rs]
        gc_p = [_row_bcast(gc_ref.at[0], p) for p in ps]
        beta_p = [_row_bcast(beta_ref.at[0], p) for p in ps]

        def weights_offdiag(bc, c):
            b = blk(bc)
            gc_c = gc_ref[0, cols_ds(b), :]
            for r in rs:
                dec = jnp.exp(gc_p[r] - gc_c)
                l_s[tile_ds(ps[r], b), :] = beta_p[r] * g_s[tile_ds(ps[r], b), :] * dec
                at_s[tile_ds(ps[r], b), :] = qk_s[tile_ds(ps[r], b), :] * dec
            return c

        lax.fori_loop(0, ib, weights_offdiag, 0)
        gc_c = gc_ref[0, cols_ds(b_own), :]
        for r in rs:
            t = tile_ds(ps[r], b_own)
            dec = jnp.exp(gc_p[r] - gc_c)
            l_s[t, :] = jnp.where(earlier[r], beta_p[r] * g_s[t, :] * dec, 0.0)
            at_s[t, :] = jnp.where(sub == own[r], qk_s[t, :], jnp.where(earlier[r], qk_s[t, :] * dec, 0.0))
            de_s[t, :] = jnp.where(sub == own[r], _row_bcast(e_s, ps[r]), 0.0)

        def finish(b, acc):
            done = []
            for r in rs:
                a_r = acc[r]
                for kk in range(r):
                    a_r = a_r - _row_bcast(l_s, ps[r] * CHUNK + ps[kk]) * done[kk]
                done.append(a_r)
            beta_c = beta_ref[0, cols_ds(b), :]
            be_c = beta_c * e_s[cols_ds(b), :]
            for r in rs:
                t_s[tile_ds(ps[r], b), :] = done[r]
                tu_s[tile_ds(ps[r], b), :] = done[r] * beta_c
                tw_s[tile_ds(ps[r], b), :] = done[r] * be_c

        def subst_offdiag(bc, c):
            b = blk(bc)

            def k_block(kb, acc):
                acc = list(acc)
                for kk in rs:
                    pk = pos(kb * SUBLANES + kk)
                    t_k = t_s[tile_ds(pk, b), :]
                    for r in rs:
                        acc[r] = acc[r] - _row_bcast(l_s, ps[r] * CHUNK + pk) * t_k
                return tuple(acc)

            finish(b, lax.fori_loop(bc, ib, k_block, (zero,) * SUBLANES))
            return c

        lax.fori_loop(0, ib, subst_offdiag, 0)
        finish(b_own, [jnp.where(sub == own[r], 1.0, 0.0) for r in rs])
        return carry

    lax.fori_loop(0, NB, row_block, 0)

    for p in range(CHUNK):
        rows = slice(p * CHUNK, (p + 1) * CHUNK)
        sl = slice(p * LANES, (p + 1) * LANES)
        twtu_ref[:, sl] = jnp.concatenate([tw_s[rows, :], tu_s[rows, :]], axis=0).T.astype(BF16)
        atde_ref[:, sl] = jnp.concatenate([at_s[rows, :], de_s[rows, :]], axis=0).T.astype(BF16)


def _gdn_inv_call(g_flat, qk_flat, beta_s, gc_s):
    u = g_flat.shape[0]
    mat = pl.BlockSpec((UNITS, CHUNK * CHUNK), lambda d, i: (i, 0))
    sc = pl.BlockSpec((1, CHUNK, UNITS), lambda d, i: (d, 0, i))
    out = pl.BlockSpec((1, UNITS, CHUNK * LANES), lambda d, i: (d, i, 0))
    soa = pltpu.VMEM((CHUNK * CHUNK, UNITS), F32)
    return pl.pallas_call(
        _gdn_inv_kernel,
        out_shape=(jax.ShapeDtypeStruct((2, u, CHUNK * LANES), BF16),
                   jax.ShapeDtypeStruct((2, u, CHUNK * LANES), BF16)),
        grid=(2, u // UNITS),
        in_specs=[mat, mat, sc, sc],
        out_specs=(out, out),
        scratch_shapes=[soa] * 8 + [pltpu.VMEM((CHUNK, UNITS), F32)],
        compiler_params=_cparams(("arbitrary", "arbitrary")),
        name="gdn_inv",
    )(g_flat, qk_flat, beta_s, gc_s)


def _gdn_scan_kernel(hb, cb, egl_ref, q_ref, k_ref, v_ref, kt_ref, twtu_ref, atde_ref, ep_ref, s0_ref,
                     o_ref, sfin_ref, s_scr, w_scr, u_scr):
    d = pl.program_id(0)
    hg = pl.program_id(1)
    ci = pl.program_id(2)
    ncb = pl.num_programs(2)

    @pl.when(ci == 0)
    def _():
        s_scr[...] = s0_ref[0]

    cblk = ci + d * (ncb - 1 - 2 * ci)
    zeros = jnp.zeros((CHUNK, HEAD_DIM), BF16)

    def prepare(c, carry):
        r0 = pl.multiple_of(c * CHUNK, CHUNK)
        for hh in range(hb):
            k = k_ref[hh, pl.ds(r0, CHUNK), :]
            v = v_ref[hh, pl.ds(r0, CHUNK), :]
            rhs = jnp.concatenate([jnp.concatenate([k, zeros], axis=1),
                                   jnp.concatenate([zeros, v], axis=1)], axis=0)
            wu = jnp.dot(twtu_ref[0, hh, c], rhs, preferred_element_type=F32)
            w_scr[c, hh] = wu[:, :HEAD_DIM].astype(BF16)
            u_scr[c, hh] = wu[:, HEAD_DIM:]
        return carry

    lax.fori_loop(0, cb, prepare, 0)

    def chunk_step(cc, carry):
        c = cc + d * (cb - 1 - 2 * cc)
        r0 = pl.multiple_of(c * CHUNK, CHUNK)
        heads = range(hb)
        s = [s_scr[hh] for hh in heads]
        x = [jnp.dot(jnp.concatenate([w_scr[c, hh], q_ref[hh, pl.ds(r0, CHUNK), :]], axis=0),
                     s[hh].astype(BF16), preferred_element_type=F32) for hh in heads]
        v_new = [u_scr[c, hh] - x[hh][:CHUNK] for hh in heads]
        for hh in heads:
            kdt = (kt_ref[hh, c].astype(F32) * ep_ref[0, hh, pl.ds(c, 1), :]).astype(BF16)
            egl = egl_ref[d, hg * hb + hh, cblk * cb + c]
            s_scr[hh] = egl * s[hh] + jnp.dot(kdt, v_new[hh].astype(BF16), preferred_element_type=F32)
        for hh in heads:
            rhs = jnp.concatenate([v_new[hh], x[hh][CHUNK:]], axis=0).astype(BF16)
            o_ref[0, hh, pl.ds(r0, CHUNK), :] = jnp.dot(atde_ref[0, hh, c], rhs,
                                                        preferred_element_type=F32).astype(o_ref.dtype)
        return carry

    lax.fori_loop(0, cb, chunk_step, 0)

    @pl.when(ci == ncb - 1)
    def _():
        sfin_ref[0] = s_scr[...]


def _gdn_scan_call(qkv, kt, twtu, atde, ep, egl, s0, nc):
    t = qkv.shape[1]
    hb = HEADS
    cb = min(8, nc)
    ncb = nc // cb
    hgs = HEADS // hb
    nat = lambda d, c: c + d * (ncb - 1 - 2 * c)
    tok = lambda part: pl.BlockSpec((hb, cb * CHUNK, HEAD_DIM),
                                    lambda d, hg, c, egl, part=part: (part * hgs + hg, nat(d, c), 0))
    per_dir = lambda last: pl.BlockSpec((1, hb, cb, CHUNK, last),
                                        lambda d, hg, c, egl: (d, hg, nat(d, c), 0, 0))
    grid_spec = pltpu.PrefetchScalarGridSpec(
        num_scalar_prefetch=1,
        grid=(2, hgs, ncb),
        in_specs=[tok(0), tok(1), tok(2),
                  pl.BlockSpec((hb, cb, HEAD_DIM, CHUNK), lambda d, hg, c, egl: (hg, nat(d, c), 0, 0)),
                  per_dir(LANES), per_dir(LANES),
                  pl.BlockSpec((1, hb, cb, CHUNK), lambda d, hg, c, egl: (d, hg, nat(d, c), 0)),
                  pl.BlockSpec((1, hb, HEAD_DIM, HEAD_DIM), lambda d, hg, c, egl: (d, hg, 0, 0))],
        out_specs=(pl.BlockSpec((1, hb, cb * CHUNK, HEAD_DIM), lambda d, hg, c, egl: (d, hg, nat(d, c), 0)),
                   pl.BlockSpec((1, hb, HEAD_DIM, HEAD_DIM), lambda d, hg, c, egl: (d, hg, 0, 0))),
        scratch_shapes=[pltpu.VMEM((hb, HEAD_DIM, HEAD_DIM), F32),
                        pltpu.VMEM((cb, hb, CHUNK, HEAD_DIM), BF16),
                        pltpu.VMEM((cb, hb, CHUNK, HEAD_DIM), F32)],
    )
    return pl.pallas_call(
        functools.partial(_gdn_scan_kernel, hb, cb),
        out_shape=(jax.ShapeDtypeStruct((2, HEADS, t, HEAD_DIM), BF16),
                   jax.ShapeDtypeStruct((2, HEADS, HEAD_DIM, HEAD_DIM), F32)),
        grid_spec=grid_spec,
        compiler_params=_cparams(("arbitrary", "arbitrary", "arbitrary")),
        name="gdn_scan",
    )(egl, qkv, qkv, qkv, kt.reshape(HEADS, nc, HEAD_DIM, CHUNK),
      twtu.reshape(2, HEADS, nc, CHUNK, LANES), atde.reshape(2, HEADS, nc, CHUNK, LANES), ep, s0)


def _gdn(qkv, bg, s0):
    t = qkv.shape[1]
    nc = t // CHUNK
    u = HEADS * nc
    up = -(-u // UNITS) * UNITS
    to_soa = lambda a: jnp.pad(jnp.transpose(a.reshape(nc, CHUNK, 2, HEADS), (2, 1, 3, 0)).reshape(2, CHUNK, u),
                               ((0, 0), (0, 0), (0, up - u)))
    beta_s = to_soa(bg[:, :2 * HEADS])
    g_s = to_soa(bg[:, 2 * HEADS:4 * HEADS])
    gc_s, ep_s, egl_s = _gdn_scal_call(g_s)
    gram, qk, kt = _gdn_gram_call(qkv, nc)
    pad_u = lambda a: jnp.pad(a.reshape(u, -1), ((0, up - u), (0, 0)))
    g_flat, qk_flat = pad_u(gram), pad_u(qk)
    twtu, atde = (a[:, :u] for a in _gdn_inv_call(g_flat, qk_flat, beta_s, gc_s))
    ep = jnp.transpose(ep_s[:, :, :u].reshape(2, CHUNK, HEADS, nc), (0, 2, 3, 1))
    egl = egl_s[:, 0, :u].reshape(2, HEADS, nc)
    return _gdn_scan_call(qkv, kt, twtu, atde, ep, egl, s0, nc)


def _merge_kernel(o_ref, zs_ref, yc_ref, ga_ref, gb_ref, wa_ref, wb_ref, m_ref, yg_s):
    @pl.when(pl.program_id(1) == 0)
    def _():
        for hh in range(HEADS):
            o = o_ref[0, hh].astype(F32) + o_ref[1, hh].astype(F32)
            y = o * lax.rsqrt(jnp.mean(o * o, axis=-1, keepdims=True) + RMS_EPS) * zs_ref[hh]
            yg_s[:, hh * HEAD_DIM:(hh + 1) * HEAD_DIM] = y.astype(BF16)

    pa = jnp.dot(yg_s[...], wa_ref[...], preferred_element_type=F32)
    pb = jnp.dot(yc_ref[...], wb_ref[...], preferred_element_type=F32)
    m_ref[...] = (ga_ref[...].astype(F32) * pa + gb_ref[...].astype(F32) * pb).astype(m_ref.dtype)


def _merge_call(o, zs, yconv, gates, wa, wb):
    t = yconv.shape[0]
    tm, tn = 512, 1024
    nj = D_MODEL // tn
    return pl.pallas_call(
        _merge_kernel,
        out_shape=jax.ShapeDtypeStruct((t, D_MODEL), BF16),
        grid=(t // tm, nj),
        in_specs=[pl.BlockSpec((2, HEADS, tm, HEAD_DIM), lambda i, j: (0, 0, i, 0)),
                  pl.BlockSpec((HEADS, tm, HEAD_DIM), lambda i, j: (0, i, 0)),
                  pl.BlockSpec((tm, CONV_WIDTH), lambda i, j: (i, 0)),
                  pl.BlockSpec((tm, tn), lambda i, j: (i, j)),
                  pl.BlockSpec((tm, tn), lambda i, j: (i, nj + j)),
                  pl.BlockSpec((GDN_WIDTH, tn), lambda i, j: (0, j)),
                  pl.BlockSpec((CONV_WIDTH, tn), lambda i, j: (0, j))],
        out_specs=pl.BlockSpec((tm, tn), lambda i, j: (i, j)),
        scratch_shapes=[pltpu.VMEM((tm, GDN_WIDTH), BF16)],
        compiler_params=_cparams(("arbitrary", "arbitrary")),
        name="merge",
    )(o, zs, yconv, gates, gates, wa, wb)


LN_ROWS = 64


def _outmm_kernel(m_ref, w_ref, o_ref):
    o_ref[...] = jnp.dot(m_ref[...], w_ref[...], preferred_element_type=F32)


def _outmm_call(m, w_out):
    t = m.shape[0]
    tm, tn = 512, 1024
    return pl.pallas_call(
        _outmm_kernel,
        out_shape=jax.ShapeDtypeStruct((t, D_MODEL), F32),
        grid=(D_MODEL // tn, t // tm),
        in_specs=[pl.BlockSpec((tm, D_MODEL), lambda j, i: (i, 0)),
                  pl.BlockSpec((D_MODEL, tn), lambda j, i: (0, j))],
        out_specs=pl.BlockSpec((tm, tn), lambda j, i: (i, j)),
        compiler_params=_cparams(("arbitrary", "arbitrary")),
        name="out_proj",
    )(m, w_out)


HALF = D_MODEL // 2
TOK_TILES = HALF // LANES


def _bf16_bits(v):
    return pltpu.bitcast(v.astype(BF16).astype(F32), jnp.uint32)


def _post_mix_kernel(mix_ref, x_ref, mod_ref, ln_ref, wrh_ref, wrl_ref, x1_ref, tokp_ref, logit_ref, tok_s):
    def rows_step(ci, c):
        sl = pl.ds(pl.multiple_of(ci * LN_ROWS, LN_ROWS), LN_ROWS)
        x1 = _layer_norm(DEEPNORM_ALPHA * x_ref[sl, :] + mod_ref[0:1, :] * mix_ref[sl, :])
        x1 = x1 * ln_ref[0:1, :] + ln_ref[1:2, :]
        x1_ref[sl, :] = x1
        tok = _layer_norm(x1) * (1.0 + mod_ref[2:3, :]) + mod_ref[1:2, :]
        tok_s[sl, :] = tok
        packed = _bf16_bits(tok[:, :HALF]) | (_bf16_bits(tok[:, HALF:]) >> 16)
        for k in range(TOK_TILES):
            tokp_ref[pl.ds(ci * (LN_ROWS * TOK_TILES) + k, LN_ROWS, stride=TOK_TILES), :] = (
                packed[:, k * LANES:(k + 1) * LANES])
        return c

    lax.fori_loop(0, x_ref.shape[0] // LN_ROWS, rows_step, 0)
    tok = tok_s[...]
    t_hi = tok.astype(BF16)
    t_lo = (tok - t_hi.astype(F32)).astype(BF16)
    logit_ref[...] = (jnp.dot(t_hi, wrh_ref[...], preferred_element_type=F32)
                      + jnp.dot(t_lo, wrh_ref[...], preferred_element_type=F32)
                      + jnp.dot(t_hi, wrl_ref[...], preferred_element_type=F32))


def _post_mix_call(mix, x, mod3, ln1, w_router):
    wr_hi = w_router.astype(BF16)
    wr_lo = (w_router - wr_hi.astype(F32)).astype(BF16)
    t = x.shape[0]
    tm = 256
    row = pl.BlockSpec((tm, D_MODEL), lambda i: (i, 0))
    vec = pl.BlockSpec((SUBLANES, D_MODEL), lambda i: (0, 0))
    wr = pl.BlockSpec((D_MODEL, LANES), lambda i: (0, 0))
    return pl.pallas_call(
        _post_mix_kernel,
        out_shape=(jax.ShapeDtypeStruct((t, D_MODEL), F32),
                   jax.ShapeDtypeStruct((t * TOK_TILES, LANES), jnp.uint32),
                   jax.ShapeDtypeStruct((t, LANES), F32)),
        grid=(t // tm,),
        in_specs=[row, row, vec, vec, wr, wr],
        out_specs=(row, pl.BlockSpec((tm * TOK_TILES, LANES), lambda i: (i, 0)),
                   pl.BlockSpec((tm, LANES), lambda i: (i, 0))),
        scratch_shapes=[pltpu.VMEM((tm, D_MODEL), F32)],
        compiler_params=_cparams(("arbitrary",)),
        name="post_mix",
    )(mix, x, mod3, ln1, wr_hi, wr_lo)


MOE_BM = 256
MOE_HC = 256


def _gather_kernel(rows, nu_ref, idx_ref, nxt_ref, src_ref, dst_ref, buf, sem):
    i = pl.program_id(0)
    nu = nu_ref[0]
    slot = i % 2

    def copy(iref, s, r):
        return pltpu.make_async_copy(src_ref.at[iref[0, 0, r]],
                                     buf.at[s, pl.ds(pl.multiple_of(r * TOK_TILES, TOK_TILES), TOK_TILES), :],
                                     sem.at[s])

    def issue(iref, s):
        def body(r, c):
            copy(iref, s, r).start()
            return c
        lax.fori_loop(0, rows, body, 0, unroll=8)

    @pl.when(i == 0)
    def _():
        issue(idx_ref, 0)

    @pl.when(i + 1 < nu)
    def _():
        issue(nxt_ref, 1 - slot)

    @pl.when(i < nu)
    def _():
        def body(r, c):
            copy(idx_ref, slot, r).wait()
            return c
        lax.fori_loop(0, rows, body, 0, unroll=8)
        for k in range(TOK_TILES):
            packed = buf[slot, pl.ds(k, rows, stride=TOK_TILES), :]
            lo, hi = k * LANES, (k + 1) * LANES
            dst_ref[:, lo:hi] = pltpu.bitcast(packed & jnp.uint32(0xFFFF0000), F32).astype(BF16)
            dst_ref[:, HALF + lo:HALF + hi] = pltpu.bitcast(packed << 16, F32).astype(BF16)

    @pl.when(i >= nu)
    def _():
        dst_ref[...] = jnp.zeros(dst_ref.shape, dst_ref.dtype)


def _gather_call(src, idx, n_used):
    n = idx.shape[0]
    rows = MOE_BM
    nblk = n // rows
    idx3 = idx.reshape(nblk, 1, rows)
    grid_spec = pltpu.PrefetchScalarGridSpec(
        num_scalar_prefetch=1,
        grid=(nblk,),
        in_specs=[pl.BlockSpec((1, 1, rows), lambda i, nu: (i, 0, 0), memory_space=pltpu.SMEM),
                  pl.BlockSpec((1, 1, rows), lambda i, nu: (jnp.minimum(i + 1, nblk - 1), 0, 0),
                               memory_space=pltpu.SMEM),
                  pl.BlockSpec(memory_space=pl.ANY)],
        out_specs=pl.BlockSpec((rows, D_MODEL), lambda i, nu: (i, 0)),
        scratch_shapes=[pltpu.VMEM((2, rows * TOK_TILES, LANES), src.dtype), pltpu.SemaphoreType.DMA((2,))],
    )
    return pl.pallas_call(
        functools.partial(_gather_kernel, rows),
        out_shape=jax.ShapeDtypeStruct((n, D_MODEL), BF16),
        grid_spec=grid_spec,
        compiler_params=_cparams(("arbitrary",)),
        name="moe_gather",
    )(n_used, idx3, idx3, src.reshape(-1, TOK_TILES, LANES))


def _expert_changed(be_ref, nu_ref, b):
    bb = jnp.minimum(b, nu_ref[0] - 1)
    return (b == 0) | (be_ref[bb] != be_ref[jnp.maximum(bb - 1, 0)])


def _moe_hidden_kernel(be_ref, nu_ref, x_ref, wg_ref, wu_ref, h_ref, w_s):
    b = pl.program_id(0)
    hc = pl.program_id(1)
    live = b < nu_ref[0]

    @pl.when(live & _expert_changed(be_ref, nu_ref, b))
    def _():
        w_s[hc, :, :MOE_HC] = wg_ref[0].astype(BF16)
        w_s[hc, :, MOE_HC:] = wu_ref[0].astype(BF16)

    @pl.when(live)
    def _():
        gu = jnp.dot(x_ref[...], w_s[hc], preferred_element_type=F32)
        h_ref[...] = (_silu(gu[:, :MOE_HC]) * gu[:, MOE_HC:]).astype(h_ref.dtype)

    @pl.when(jnp.logical_not(live))
    def _():
        h_ref[...] = jnp.zeros(h_ref.shape, h_ref.dtype)


def _moe_down_kernel(be_ref, nu_ref, h_ref, wd_ref, y_ref, wd_s):
    b = pl.program_id(0)
    live = b < nu_ref[0]

    @pl.when(live & _expert_changed(be_ref, nu_ref, b))
    def _():
        wd_s[...] = wd_ref[0].astype(BF16)

    @pl.when(live)
    def _():
        y_ref[...] = jnp.dot(h_ref[...], wd_s[...], preferred_element_type=F32)

    @pl.when(jnp.logical_not(live))
    def _():
        y_ref[...] = jnp.zeros(y_ref.shape, F32)


def _expert_call(xs, wg, wu, wd, block_e, n_used):
    n = xs.shape[0]
    nb = n // MOE_BM
    nh = EXPERT_HIDDEN // MOE_HC
    blk = lambda b, nu: jnp.minimum(b, nu[0] - 1)

    def w_chunk(b, h, be, nu):
        bb = blk(b, nu)
        first = (b < nu[0]) & ((bb == 0) | (be[bb] != be[jnp.maximum(bb - 1, 0)]))
        return be[bb], 0, jnp.where(first, h, nh - 1)

    hid = pl.pallas_call(
        _moe_hidden_kernel,
        out_shape=jax.ShapeDtypeStruct((n, EXPERT_HIDDEN), BF16),
        grid_spec=pltpu.PrefetchScalarGridSpec(
            num_scalar_prefetch=2,
            grid=(nb, nh),
            in_specs=[pl.BlockSpec((MOE_BM, D_MODEL), lambda b, h, be, nu: (blk(b, nu), 0)),
                      pl.BlockSpec((1, D_MODEL, MOE_HC), w_chunk),
                      pl.BlockSpec((1, D_MODEL, MOE_HC), w_chunk)],
            out_specs=pl.BlockSpec((MOE_BM, MOE_HC), lambda b, h, be, nu: (b, h)),
            scratch_shapes=[pltpu.VMEM((nh, D_MODEL, 2 * MOE_HC), BF16)]),
        compiler_params=_cparams(("arbitrary", "arbitrary")),
        name="moe_hidden",
    )(block_e, n_used, xs, wg, wu)
    return pl.pallas_call(
        _moe_down_kernel,
        out_shape=jax.ShapeDtypeStruct((n, D_MODEL), F32),
        grid_spec=pltpu.PrefetchScalarGridSpec(
            num_scalar_prefetch=2,
            grid=(nb,),
            in_specs=[pl.BlockSpec((MOE_BM, EXPERT_HIDDEN), lambda b, be, nu: (blk(b, nu), 0)),
                      pl.BlockSpec((1, EXPERT_HIDDEN, D_MODEL), lambda b, be, nu: (be[blk(b, nu)], 0, 0))],
            out_specs=pl.BlockSpec((MOE_BM, D_MODEL), lambda b, be, nu: (b, 0)),
            scratch_shapes=[pltpu.VMEM((EXPERT_HIDDEN, D_MODEL), BF16)]),
        compiler_params=_cparams(("arbitrary",)),
        name="moe_down",
    )(block_e, n_used, hid, wd)


def _combine_kernel(rows, d_ref, nxt_ref, y_ref, x1_ref, w_ref, mod_ref, ln_ref, o_ref, buf, sem):
    i = pl.program_id(0)
    slot = i % 2

    def copy(iref, s, k, r):
        return pltpu.make_async_copy(y_ref.at[pl.ds(iref[0, k, r], 1), :],
                                     buf.at[s, k, pl.ds(r, 1), :], sem.at[s])

    def issue(iref, s):
        def body(r, c):
            copy(iref, s, 0, r).start()
            copy(iref, s, 1, r).start()
            return c
        lax.fori_loop(0, rows, body, 0, unroll=8)

    @pl.when(i == 0)
    def _():
        issue(d_ref, 0)

    @pl.when(i + 1 < pl.num_programs(0))
    def _():
        issue(nxt_ref, 1 - slot)

    def wait_body(r, c):
        copy(d_ref, slot, 0, r).wait()
        copy(d_ref, slot, 1, r).wait()
        return c

    lax.fori_loop(0, rows, wait_body, 0, unroll=8)

    def rows_step(ci, c):
        sl = pl.ds(pl.multiple_of(ci * LN_ROWS, LN_ROWS), LN_ROWS)
        f = w_ref[sl, 0:1] * buf[slot, 0, sl, :] + w_ref[sl, 1:2] * buf[slot, 1, sl, :]
        y = _layer_norm(DEEPNORM_ALPHA * x1_ref[sl, :] + mod_ref[0:1, :] * f)
        o_ref[sl, :] = y * ln_ref[0:1, :] + ln_ref[1:2, :]
        return c

    lax.fori_loop(0, rows // LN_ROWS, rows_step, 0)


def _combine_call(ys, dest, weights, x1, mod_row, ln2):
    t = x1.shape[0]
    rows = 128
    nblk = t // rows
    dest3 = jnp.transpose(dest.reshape(nblk, rows, 2), (0, 2, 1))
    return pl.pallas_call(
        functools.partial(_combine_kernel, rows),
        out_shape=jax.ShapeDtypeStruct((t, D_MODEL), F32),
        grid=(nblk,),
        in_specs=[pl.BlockSpec((1, 2, rows), lambda i: (i, 0, 0), memory_space=pltpu.SMEM),
                  pl.BlockSpec((1, 2, rows), lambda i: (jnp.minimum(i + 1, nblk - 1), 0, 0),
                               memory_space=pltpu.SMEM),
                  pl.BlockSpec(memory_space=pl.ANY),
                  pl.BlockSpec((rows, D_MODEL), lambda i: (i, 0)),
                  pl.BlockSpec((rows, 2), lambda i: (i, 0)),
                  pl.BlockSpec((SUBLANES, D_MODEL), lambda i: (0, 0)),
                  pl.BlockSpec((SUBLANES, D_MODEL), lambda i: (0, 0))],
        out_specs=pl.BlockSpec((rows, D_MODEL), lambda i: (i, 0)),
        scratch_shapes=[pltpu.VMEM((2, 2, rows, D_MODEL), F32), pltpu.SemaphoreType.DMA((2,))],
        compiler_params=_cparams(("arbitrary",)),
        name="moe_combine",
    )(dest3, dest3, ys, x1, weights, mod_row, ln2)


def _route(logits, b_group, b_expert):
    t = logits.shape[0]
    p_group = jax.nn.softmax(logits[:, :N_GROUPS] + b_group, axis=-1)
    group = jnp.argmax(p_group, axis=-1)
    gate_group = jnp.take_along_axis(p_group, group[:, None], axis=-1)
    le = (logits[:, N_GROUPS:N_GROUPS + N_EXPERTS] + b_expert).reshape(t, N_GROUPS, EXPERTS_PER_GROUP)
    le = jnp.take_along_axis(le, group[:, None, None], axis=1)[:, 0]
    top_p, top_i = lax.top_k(jax.nn.softmax(le, axis=-1), 2)
    weights = gate_group * top_p / jnp.sum(top_p, axis=-1, keepdims=True)
    expert_id = group[:, None] * EXPERTS_PER_GROUP + top_i
    return expert_id.astype(jnp.int32), weights


def _moe(tok, logits, b_group, b_expert, wg, wu, wd, x1, mod_row, ln2):
    t = x1.shape[0]
    expert_id, weights = _route(logits, b_group, b_expert)
    e_flat = expert_id.reshape(-1)
    n_assign = e_flat.shape[0]
    onehot = (e_flat[:, None] == jnp.arange(N_EXPERTS)[None, :]).astype(jnp.int32)
    rank = jnp.take_along_axis(jnp.cumsum(onehot, axis=0), e_flat[:, None], axis=1)[:, 0] - 1
    counts = jnp.sum(onehot, axis=0)
    padded = (counts + MOE_BM - 1) // MOE_BM * MOE_BM
    pad_end = jnp.cumsum(padded)
    dest = (pad_end - padded)[e_flat] + rank
    n_blocks = -(-n_assign // MOE_BM) + N_EXPERTS
    n_rows = n_blocks * MOE_BM
    src_tok = jnp.zeros((n_rows,), jnp.int32).at[dest].set(jnp.arange(n_assign, dtype=jnp.int32) // 2)
    block_e = jnp.minimum(jnp.searchsorted(pad_end, jnp.arange(n_blocks) * MOE_BM, side="right"),
                          N_EXPERTS - 1).astype(jnp.int32)
    n_used = (pad_end[-1] // MOE_BM).astype(jnp.int32).reshape(1)
    xs = _gather_call(tok, src_tok, n_used)
    ys = _expert_call(xs, wg, wu, wd, block_e, n_used)
    return _combine_call(ys, dest.reshape(t, 2).astype(jnp.int32), weights, x1, mod_row, ln2)


def _pad_rows(v, rows=SUBLANES):
    return jnp.pad(v, ((0, rows - v.shape[0]), (0, 0)))


def _layer(x, ctx, c, c_ctx, w_ada, b_ada, w_in, conv_qkv, a_log, dt_bias, gdn_norm_w, conv_b,
           w_branch_a, w_branch_b, w_out, ln1_g, ln1_b, w_router_group, b_router_group,
           w_router_expert, b_router_expert, w_exp_gate, w_exp_up, w_exp_down, ln2_g, ln2_b):
    t = x.shape[0]
    mod = _mod_call(_pad_rows(jnp.stack([c, c_ctx])), w_ada, b_ada.reshape(1, -1))
    mod_lat = mod[0].reshape(N_MOD, D_MODEL)
    mod_ctx = mod[1].reshape(N_MOD, D_MODEL)

    n_a = 4 * GDN_WIDTH
    n_ba = 4 * HEADS
    w_pa = w_in[:, :n_a].astype(BF16)
    w_pb = w_in[:, n_a + n_ba:].astype(BF16)
    w_pc = jnp.pad(w_in[:, n_a:n_a + n_ba], ((0, 0), (0, LANES - n_ba))).astype(BF16)
    lane_row = lambda v: jnp.pad(v.reshape(1, -1), ((0, 0), (2 * HEADS, LANES - 4 * HEADS)))
    alog_row, dtb_row = lane_row(a_log), lane_row(dt_bias)

    h_ctx = _ln_mod_call(ctx, mod_ctx[0:1], mod_ctx[1:2])
    tc = ctx.shape[0]
    qkv_c = _inproj_qkv(h_ctx, w_pa, conv_qkv, tc, tc)
    bg_c = _inproj_ba(h_ctx, w_pc, alog_row, dtb_row, tc)
    s0 = jnp.zeros((2, HEADS, HEAD_DIM, HEAD_DIM), F32)
    _, s_ctx = _gdn(qkv_c, bg_c, s0)

    tm = 512
    h = _ln_mod_call(x, mod_lat[0:1], mod_lat[1:2])
    qkv = _inproj_qkv(h, w_pa, conv_qkv, GRID_W, tm)
    zs = _inproj_z(h, w_pa, gdn_norm_w.reshape(1, HEAD_DIM), tm)
    yconv = _inproj_xbc(h, w_pb, conv_b, GRID_W, tm)
    gates = _inproj_gates(h, w_pb, tm)
    bg = _inproj_ba(h, w_pc, alog_row, dtb_row, tm)
    o, _ = _gdn(qkv, bg, s_ctx)
    m = _merge_call(o, zs, yconv, gates, w_branch_a.astype(BF16), w_branch_b.astype(BF16))

    w_router = jnp.pad(jnp.concatenate([w_router_group, w_router_expert], axis=1),
                       ((0, 0), (0, LANES - N_GROUPS - N_EXPERTS)))
    mix = _outmm_call(m, w_out.astype(BF16))
    x1, tok, logits = _post_mix_call(mix, x, _pad_rows(mod_lat[2:5]),
                                     _pad_rows(jnp.stack([ln1_g, ln1_b])), w_router)

    return _moe(tok, logits, b_router_group, b_router_expert, w_exp_gate, w_exp_up, w_exp_down,
                x1, _pad_rows(mod_lat[5:6]), _pad_rows(jnp.stack([ln2_g, ln2_b])))


def kernel(x, c, ctx, c_ctx, w_ada, b_ada, w_in, conv_qkv, a_log, dt_bias, gdn_norm_w, conv_b,
           w_branch_a, w_branch_b, w_out, ln1_g, ln1_b, w_router_group, b_router_group,
           w_router_expert, b_router_expert, w_exp_gate, w_exp_up, w_exp_down, ln2_g, ln2_b):
    assert x.shape[0] == 1 and w_ada.shape[0] == 1, "single batch element, single layer"
    out = _layer(x[0], ctx[0], c[0], c_ctx, w_ada[0], b_ada[0], w_in[0], conv_qkv[0],
                 a_log[0].reshape(-1), dt_bias[0].reshape(-1), gdn_norm_w[0], conv_b[0],
                 w_branch_a[0], w_branch_b[0], w_out[0], ln1_g[0], ln1_b[0],
                 w_router_group[0], b_router_group[0], w_router_expert[0], b_router_expert[0],
                 w_exp_gate[0], w_exp_up[0], w_exp_down[0], ln2_g[0], ln2_b[0])
    return out[None]
```

```python
import functools

import numpy as np
import jax
import jax.numpy as jnp
from jax import lax
from jax.experimental import pallas as pl
from jax.experimental.pallas import tpu as pltpu

F32 = jnp.float32
BF16 = jnp.bfloat16
HIGHEST = lax.Precision.HIGHEST

D_MODEL = 4096
GRID_W = 64
CHUNK = 64
HEADS = 16
HEAD_DIM = 128
GDN_WIDTH = HEADS * HEAD_DIM
CONV_WIDTH = D_MODEL // 2
N_GROUPS = 8
EXPERTS_PER_GROUP = 8
N_EXPERTS = 64
EXPERT_HIDDEN = 768
N_MOD = 6
DEEPNORM_ALPHA = 2.0 ** 0.25
LN_EPS = 1e-6
RMS_EPS = 1e-6

LANES = 128
SUBLANES = 8
UNITS = 128
VMEM_LIMIT = 52 * 1024 * 1024


def _cparams(sem, vmem=VMEM_LIMIT):
    return pltpu.CompilerParams(dimension_semantics=sem, vmem_limit_bytes=vmem)


def _silu(x):
    return x * jax.nn.sigmoid(x)


def _layer_norm(x):
    mu = jnp.mean(x, axis=-1, keepdims=True)
    xc = x - mu
    var = jnp.mean(xc * xc, axis=-1, keepdims=True)
    return xc * lax.rsqrt(var + LN_EPS)


def _mod_kernel(c_ref, w_ref, b_ref, o_ref):
    s = _silu(c_ref[...])
    o_ref[...] = jnp.dot(s, w_ref[...], preferred_element_type=F32, precision=HIGHEST) + b_ref[...]


def _mod_call(cs, w_ada, b_ada):
    n = w_ada.shape[1]
    tn = 512
    return pl.pallas_call(
        _mod_kernel,
        out_shape=jax.ShapeDtypeStruct((SUBLANES, n), F32),
        grid=(n // tn,),
        in_specs=[pl.BlockSpec((SUBLANES, D_MODEL), lambda j: (0, 0)),
                  pl.BlockSpec((D_MODEL, tn), lambda j: (0, j)),
                  pl.BlockSpec((1, tn), lambda j: (0, j))],
        out_specs=pl.BlockSpec((SUBLANES, tn), lambda j: (0, j)),
        compiler_params=_cparams(("arbitrary",)),
        name="mod",
    )(cs, w_ada, b_ada)


def _ln_mod_kernel(x_ref, shift_ref, scale_ref, o_ref):
    y = _layer_norm(x_ref[...])
    o_ref[...] = (y * (1.0 + scale_ref[...]) + shift_ref[...]).astype(o_ref.dtype)


def _ln_mod_call(x, shift, scale):
    t = x.shape[0]
    tr = min(256, t)
    return pl.pallas_call(
        _ln_mod_kernel,
        out_shape=jax.ShapeDtypeStruct((t, D_MODEL), BF16),
        grid=(t // tr,),
        in_specs=[pl.BlockSpec((tr, D_MODEL), lambda i: (i, 0)),
                  pl.BlockSpec((1, D_MODEL), lambda i: (0, 0)),
                  pl.BlockSpec((1, D_MODEL), lambda i: (0, 0))],
        out_specs=pl.BlockSpec((tr, D_MODEL), lambda i: (i, 0)),
        compiler_params=_cparams(("arbitrary",)),
        name="ln_mod",
    )(x, shift, scale)


COL_QKV = 0
COL_Z = 3 * GDN_WIDTH
COL_XB = 0
COL_BG = CONV_WIDTH
COL_CG = 2 * CONV_WIDTH
COL_GATES = 3 * CONV_WIDTH
COL_BA = 0


def _conv3_rows(x, taps_ref, group):
    rows = x.shape[0]
    pos = lax.broadcasted_iota(jnp.int32, x.shape, 0) % group
    prev = jnp.where(pos == 0, 0.0, pltpu.roll(x, 1, 0))
    nxt = jnp.where(pos == group - 1, 0.0, pltpu.roll(x, rows - 1, 0))
    return taps_ref[0:1, :] * prev + taps_ref[1:2, :] * x + taps_ref[2:3, :] * nxt


def _ip_qkv_kernel(group, x_ref, w_ref, taps_ref, o_ref):
    acc = jnp.dot(x_ref[...], w_ref[...], preferred_element_type=F32)
    y = _silu(_conv3_rows(acc, taps_ref, group))
    j = pl.program_id(0)
    tiles_per_part = GDN_WIDTH // acc.shape[1]
    for hh in range(acc.shape[1] // HEAD_DIM):
        ys = y[:, hh * HEAD_DIM:(hh + 1) * HEAD_DIM]
        r = lax.rsqrt(jnp.sum(ys * ys, axis=-1, keepdims=True) + 1e-6)
        scale = jnp.where(j < tiles_per_part, r * HEAD_DIM ** -0.5,
                          jnp.where(j < 2 * tiles_per_part, r, 1.0))
        o_ref[hh] = (ys * scale).astype(o_ref.dtype)


def _ip_z_kernel(x_ref, w_ref, nw_ref, o_ref):
    acc = jnp.dot(x_ref[...], w_ref[...], preferred_element_type=F32)
    for hh in range(acc.shape[1] // HEAD_DIM):
        zs = acc[:, hh * HEAD_DIM:(hh + 1) * HEAD_DIM]
        o_ref[hh] = _silu(zs) * nw_ref[...]


def _ip_xbc_kernel(group, x_ref, wxb_ref, wbg_ref, wcg_ref, taps_ref, o_ref):
    x = x_ref[...]
    xb = jnp.dot(x, wxb_ref[...], preferred_element_type=F32)
    bg = jnp.dot(x, wbg_ref[...], preferred_element_type=F32)
    cg = jnp.dot(x, wcg_ref[...], preferred_element_type=F32)
    o_ref[...] = (bg * _conv3_rows(cg * xb, taps_ref, group)).astype(o_ref.dtype)


def _ip_gates_kernel(x_ref, w_ref, o_ref):
    acc = jnp.dot(x_ref[...], w_ref[...], preferred_element_type=F32)
    o_ref[...] = jax.nn.sigmoid(acc).astype(o_ref.dtype)


def _ip_ba_kernel(x_ref, w_ref, alog_ref, dtb_ref, o_ref):
    acc = jnp.dot(x_ref[...], w_ref[...], preferred_element_type=F32)
    lane = lax.broadcasted_iota(jnp.int32, acc.shape, 1)
    a = acc + dtb_ref[...]
    softplus = jnp.maximum(a, 0.0) + jnp.log(1.0 + jnp.exp(-jnp.abs(a)))
    o_ref[...] = jnp.where(lane < 2 * HEADS, jax.nn.sigmoid(acc), -jnp.exp(alog_ref[...]) * softplus)


def _inproj(kernel, h, w_cat, col_starts, tn, n_tiles, extra, extra_specs, out_shape, out_spec, tm):
    t = h.shape[0]
    w_specs = [pl.BlockSpec((D_MODEL, tn), functools.partial(lambda j, i, o: (0, o + j), o=c // tn))
               for c in col_starts]
    return pl.pallas_call(
        kernel,
        out_shape=out_shape,
        grid=(n_tiles, t // tm),
        in_specs=[pl.BlockSpec((tm, D_MODEL), lambda j, i: (i, 0))] + w_specs + extra_specs,
        out_specs=out_spec,
        compiler_params=_cparams(("arbitrary", "arbitrary")),
        name="inproj",
    )(h, *([w_cat] * len(col_starts)), *extra)


def _inproj_qkv(h, w_cat, taps, group, tm):
    t = h.shape[0]
    tn = 1024
    hpt = tn // HEAD_DIM
    return _inproj(
        functools.partial(_ip_qkv_kernel, group), h, w_cat, [COL_QKV], tn, 3 * GDN_WIDTH // tn,
        [taps], [pl.BlockSpec((3, tn), lambda j, i: (0, j))],
        jax.ShapeDtypeStruct((3 * HEADS, t, HEAD_DIM), BF16),
        pl.BlockSpec((hpt, tm, HEAD_DIM), lambda j, i: (j, i, 0)), tm)


def _inproj_z(h, w_cat, norm_w, tm):
    t = h.shape[0]
    tn = 1024
    hpt = tn // HEAD_DIM
    return _inproj(
        _ip_z_kernel, h, w_cat, [COL_Z], tn, GDN_WIDTH // tn,
        [norm_w], [pl.BlockSpec((1, HEAD_DIM), lambda j, i: (0, 0))],
        jax.ShapeDtypeStruct((HEADS, t, HEAD_DIM), F32),
        pl.BlockSpec((hpt, tm, HEAD_DIM), lambda j, i: (j, i, 0)), tm)


def _inproj_xbc(h, w_cat, taps, group, tm):
    t = h.shape[0]
    tn = 512
    return _inproj(
        functools.partial(_ip_xbc_kernel, group), h, w_cat, [COL_XB, COL_BG, COL_CG], tn,
        CONV_WIDTH // tn, [taps], [pl.BlockSpec((3, tn), lambda j, i: (0, j))],
        jax.ShapeDtypeStruct((t, CONV_WIDTH), BF16),
        pl.BlockSpec((tm, tn), lambda j, i: (i, j)), tm)


def _inproj_gates(h, w_cat, tm):
    t = h.shape[0]
    tn = 1024
    return _inproj(
        _ip_gates_kernel, h, w_cat, [COL_GATES], tn, 2 * D_MODEL // tn, [], [],
        jax.ShapeDtypeStruct((t, 2 * D_MODEL), BF16),
        pl.BlockSpec((tm, tn), lambda j, i: (i, j)), tm)


def _inproj_ba(h, w_cat, alog_row, dtb_row, tm):
    t = h.shape[0]
    return _inproj(
        _ip_ba_kernel, h, w_cat, [COL_BA], LANES, 1, [alog_row, dtb_row],
        [pl.BlockSpec((1, LANES), lambda j, i: (0, 0))] * 2,
        jax.ShapeDtypeStruct((t, LANES), F32),
        pl.BlockSpec((tm, LANES), lambda j, i: (i, 0)), tm)


def _gdn_scal_kernel(g_ref, gc_ref, ep_ref, egl_ref):
    d = pl.program_id(0)
    r = lax.broadcasted_iota(jnp.int32, (CHUNK, CHUNK), 0)
    c = lax.broadcasted_iota(jnp.int32, (CHUNK, CHUNK), 1)
    tri = jnp.where(d == 0, (c <= r).astype(F32), (c >= r).astype(F32))
    gc = jnp.dot(tri, g_ref[0], preferred_element_type=F32, precision=HIGHEST)
    gl = jnp.where(d == 0, gc[CHUNK - 1:CHUNK, :], gc[0:1, :])
    gc_ref[0] = gc
    ep_ref[0] = jnp.exp(gl - gc)
    egl_ref[0] = jnp.exp(gl)


def _gdn_scal_call(g_s):
    u = g_s.shape[2]
    tu = 512 if u % 512 == 0 else UNITS
    spec = pl.BlockSpec((1, CHUNK, tu), lambda d, i: (d, 0, i))
    return pl.pallas_call(
        _gdn_scal_kernel,
        out_shape=(jax.ShapeDtypeStruct(g_s.shape, F32), jax.ShapeDtypeStruct(g_s.shape, F32),
                   jax.ShapeDtypeStruct((2, 1, u), F32)),
        grid=(2, u // tu),
        in_specs=[spec],
        out_specs=(spec, spec, pl.BlockSpec((1, 1, tu), lambda d, i: (d, 0, i))),
        compiler_params=_cparams(("arbitrary", "arbitrary")),
        name="gdn_scal",
    )(g_s)


def _nt_dot(a, b):
    return lax.dot_general(a, b, (((1,), (1,)), ((), ())), preferred_element_type=F32)


def _gdn_gram_kernel(cb, q_ref, k_ref, g_ref, qk_ref, kt_ref):
    r = lax.broadcasted_iota(jnp.int32, (HEAD_DIM, HEAD_DIM), 0)
    c = lax.broadcasted_iota(jnp.int32, (HEAD_DIM, HEAD_DIM), 1)
    eye = (r == c).astype(BF16)
    for ci in range(cb):
        k = k_ref[0, ci * CHUNK:(ci + 1) * CHUNK, :]
        q = q_ref[0, ci * CHUNK:(ci + 1) * CHUNK, :]
        g_ref[ci] = _nt_dot(k, k)
        qk_ref[ci] = _nt_dot(q, k)
        kt_ref[ci] = _nt_dot(eye, k).astype(BF16)


def _gdn_gram_call(qkv, nc):
    cb = min(32, nc)
    ncb = nc // cb
    u = HEADS * nc
    return pl.pallas_call(
        functools.partial(_gdn_gram_kernel, cb),
        out_shape=(jax.ShapeDtypeStruct((u, CHUNK, CHUNK), F32),
                   jax.ShapeDtypeStruct((u, CHUNK, CHUNK), F32),
                   jax.ShapeDtypeStruct((u, HEAD_DIM, CHUNK), BF16)),
        grid=(HEADS, ncb),
        in_specs=[pl.BlockSpec((1, cb * CHUNK, HEAD_DIM), lambda h, c: (h, c, 0)),
                  pl.BlockSpec((1, cb * CHUNK, HEAD_DIM), lambda h, c: (HEADS + h, c, 0))],
        out_specs=(pl.BlockSpec((cb, CHUNK, CHUNK), lambda h, c: (h * ncb + c, 0, 0)),
                   pl.BlockSpec((cb, CHUNK, CHUNK), lambda h, c: (h * ncb + c, 0, 0)),
                   pl.BlockSpec((cb, HEAD_DIM, CHUNK), lambda h, c: (h * ncb + c, 0, 0))),
        compiler_params=_cparams(("arbitrary", "arbitrary")),
        name="gdn_gram",
    )(qkv, qkv)


NB = CHUNK // SUBLANES


def _row_bcast(ref, row):
    return jnp.broadcast_to(ref[pl.ds(row, 1), :], (SUBLANES, UNITS))


def _gdn_inv_kernel(g_ref, qk_ref, beta_ref, gc_ref, twtu_ref, atde_ref, *scratch):
    for bwd in (False, True):
        @pl.when(pl.program_id(0) == int(bwd))
        def _(bwd=bwd):
            _gdn_inv_body(bwd, g_ref, qk_ref, beta_ref, gc_ref, twtu_ref.at[0], atde_ref.at[0], *scratch)


def _gdn_inv_body(bwd, g_ref, qk_ref, beta_ref, gc_ref, twtu_ref, atde_ref,
                  g_s, qk_s, l_s, t_s, tw_s, tu_s, at_s, de_s, e_s):
    pos = (lambda a: CHUNK - 1 - a) if bwd else (lambda a: a)
    blk = (lambda b: NB - 1 - b) if bwd else (lambda b: b)

    for pp in range(CHUNK // 2):
        sl = slice(pp * LANES, (pp + 1) * LANES)
        g_s[sl, :] = g_ref[:, sl].T
        qk_s[sl, :] = qk_ref[:, sl].T
    e_s[...] = jnp.exp(gc_ref[0])

    @pl.when(pl.program_id(1) == 0)
    def _():
        for ref in (tw_s, tu_s, at_s, de_s):
            ref[...] = jnp.zeros(ref.shape, F32)

    sub = lax.broadcasted_iota(jnp.int32, (SUBLANES, UNITS), 0)
    zero = jnp.zeros((SUBLANES, UNITS), F32)
    rs = range(SUBLANES)
    own = [(SUBLANES - 1 - r) if bwd else r for r in rs]
    earlier = [(sub > own[r]) if bwd else (sub < own[r]) for r in rs]

    def tile_ds(p, b):
        return pl.ds(pl.multiple_of(p * CHUNK + b * SUBLANES, SUBLANES), SUBLANES)

    def cols_ds(b):
        return pl.ds(pl.multiple_of(b * SUBLANES, SUBLANES), SUBLANES)

    def row_block(ib, carry):
        b_own = blk(ib)
        ps = [pos(ib * SUBLANES + r) for r in rs]
        gc_p = [_row_bcast(gc_ref.at[0], p) for p in ps]
        beta_p = [_row_bcast(beta_ref.at[0], p) for p in ps]

        def weights_offdiag(bc, c):
            b = blk(bc)
            gc_c = gc_ref[0, cols_ds(b), :]
            for r in rs:
                dec = jnp.exp(gc_p[r] - gc_c)
                l_s[tile_ds(ps[r], b), :] = beta_p[r] * g_s[tile_ds(ps[r], b), :] * dec
                at_s[tile_ds(ps[r], b), :] = qk_s[tile_ds(ps[r], b), :] * dec
            return c

        lax.fori_loop(0, ib, weights_offdiag, 0)
        gc_c = gc_ref[0, cols_ds(b_own), :]
        for r in rs:
            t = tile_ds(ps[r], b_own)
            dec = jnp.exp(gc_p[r] - gc_c)
            l_s[t, :] = jnp.where(earlier[r], beta_p[r] * g_s[t, :] * dec, 0.0)
            at_s[t, :] = jnp.where(sub == own[r], qk_s[t, :], jnp.where(earlier[r], qk_s[t, :] * dec, 0.0))
            de_s[t, :] = jnp.where(sub == own[r], _row_bcast(e_s, ps[r]), 0.0)

        def finish(b, acc):
            done = []
            for r in rs:
                a_r = acc[r]
                for kk in range(r):
                    a_r = a_r - _row_bcast(l_s, ps[r] * CHUNK + ps[kk]) * done[kk]
                done.append(a_r)
            beta_c = beta_ref[0, cols_ds(b), :]
            be_c = beta_c * e_s[cols_ds(b), :]
            for r in rs:
                t_s[tile_ds(ps[r], b), :] = done[r]
                tu_s[tile_ds(ps[r], b), :] = done[r] * beta_c
                tw_s[tile_ds(ps[r], b), :] = done[r] * be_c

        def subst_offdiag(bc, c):
            b = blk(bc)

            def k_block(kb, acc):
                acc = list(acc)
                for kk in rs:
                    pk = pos(kb * SUBLANES + kk)
                    t_k = t_s[tile_ds(pk, b), :]
                    for r in rs:
                        acc[r] = acc[r] - _row_bcast(l_s, ps[r] * CHUNK + pk) * t_k
                return tuple(acc)

            finish(b, lax.fori_loop(bc, ib, k_block, (zero,) * SUBLANES))
            return c

        lax.fori_loop(0, ib, subst_offdiag, 0)
        finish(b_own, [jnp.where(sub == own[r], 1.0, 0.0) for r in rs])
        return carry

    lax.fori_loop(0, NB, row_block, 0)

    for p in range(CHUNK):
        rows = slice(p * CHUNK, (p + 1) * CHUNK)
        sl = slice(p * LANES, (p + 1) * LANES)
        twtu_ref[:, sl] = jnp.concatenate([tw_s[rows, :], tu_s[rows, :]], axis=0).T.astype(BF16)
        atde_ref[:, sl] = jnp.concatenate([at_s[rows, :], de_s[rows, :]], axis=0).T.astype(BF16)


def _gdn_inv_call(g_flat, qk_flat, beta_s, gc_s):
    u = g_flat.shape[0]
    mat = pl.BlockSpec((UNITS, CHUNK * CHUNK), lambda d, i: (i, 0))
    sc = pl.BlockSpec((1, CHUNK, UNITS), lambda d, i: (d, 0, i))
    out = pl.BlockSpec((1, UNITS, CHUNK * LANES), lambda d, i: (d, i, 0))
    soa = pltpu.VMEM((CHUNK * CHUNK, UNITS), F32)
    return pl.pallas_call(
        _gdn_inv_kernel,
        out_shape=(jax.ShapeDtypeStruct((2, u, CHUNK * LANES), BF16),
                   jax.ShapeDtypeStruct((2, u, CHUNK * LANES), BF16)),
        grid=(2, u // UNITS),
        in_specs=[mat, mat, sc, sc],
        out_specs=(out, out),
        scratch_shapes=[soa] * 8 + [pltpu.VMEM((CHUNK, UNITS), F32)],
        compiler_params=_cparams(("arbitrary", "arbitrary")),
        name="gdn_inv",
    )(g_flat, qk_flat, beta_s, gc_s)


def _gdn_scan_kernel(hb, cb, egl_ref, q_ref, k_ref, v_ref, kt_ref, twtu_ref, atde_ref, ep_ref, s0_ref,
                     o_ref, sfin_ref, s_scr, w_scr, u_scr):
    d = pl.program_id(0)
    hg = pl.program_id(1)
    ci = pl.program_id(2)
    ncb = pl.num_programs(2)

    @pl.when(ci == 0)
    def _():
        s_scr[...] = s0_ref[0]

    cblk = ci + d * (ncb - 1 - 2 * ci)
    zeros = jnp.zeros((CHUNK, HEAD_DIM), BF16)

    def prepare(c, carry):
        r0 = pl.multiple_of(c * CHUNK, CHUNK)
        for hh in range(hb):
            k = k_ref[hh, pl.ds(r0, CHUNK), :]
            v = v_ref[hh, pl.ds(r0, CHUNK), :]
            rhs = jnp.concatenate([jnp.concatenate([k, zeros], axis=1),
                                   jnp.concatenate([zeros, v], axis=1)], axis=0)
            wu = jnp.dot(twtu_ref[0, hh, c], rhs, preferred_element_type=F32)
            w_scr[c, hh] = wu[:, :HEAD_DIM].astype(BF16)
            u_scr[c, hh] = wu[:, HEAD_DIM:]
        return carry

    lax.fori_loop(0, cb, prepare, 0)

    def chunk_step(cc, carry):
        c = cc + d * (cb - 1 - 2 * cc)
        r0 = pl.multiple_of(c * CHUNK, CHUNK)
        heads = range(hb)
        s = [s_scr[hh] for hh in heads]
        x = [jnp.dot(jnp.concatenate([w_scr[c, hh], q_ref[hh, pl.ds(r0, CHUNK), :]], axis=0),
                     s[hh].astype(BF16), preferred_element_type=F32) for hh in heads]
        v_new = [u_scr[c, hh] - x[hh][:CHUNK] for hh in heads]
        for hh in heads:
            kdt = (kt_ref[hh, c].astype(F32) * ep_ref[0, hh, pl.ds(c, 1), :]).astype(BF16)
            egl = egl_ref[d, hg * hb + hh, cblk * cb + c]
            s_scr[hh] = egl * s[hh] + jnp.dot(kdt, v_new[hh].astype(BF16), preferred_element_type=F32)
        for hh in heads:
            rhs = jnp.concatenate([v_new[hh], x[hh][CHUNK:]], axis=0).astype(BF16)
            o_ref[0, hh, pl.ds(r0, CHUNK), :] = jnp.dot(atde_ref[0, hh, c], rhs,
                                                        preferred_element_type=F32).astype(o_ref.dtype)
        return carry

    lax.fori_loop(0, cb, chunk_step, 0)

    @pl.when(ci == ncb - 1)
    def _():
        sfin_ref[0] = s_scr[...]


def _gdn_scan_call(qkv, kt, twtu, atde, ep, egl, s0, nc):
    t = qkv.shape[1]
    hb = HEADS
    cb = min(8, nc)
    ncb = nc // cb
    hgs = HEADS // hb
    nat = lambda d, c: c + d * (ncb - 1 - 2 * c)
    tok = lambda part: pl.BlockSpec((hb, cb * CHUNK, HEAD_DIM),
                                    lambda d, hg, c, egl, part=part: (part * hgs + hg, nat(d, c), 0))
    per_dir = lambda last: pl.BlockSpec((1, hb, cb, CHUNK, last),
                                        lambda d, hg, c, egl: (d, hg, nat(d, c), 0, 0))
    grid_spec = pltpu.PrefetchScalarGridSpec(
        num_scalar_prefetch=1,
        grid=(2, hgs, ncb),
        in_specs=[tok(0), tok(1), tok(2),
                  pl.BlockSpec((hb, cb, HEAD_DIM, CHUNK), lambda d, hg, c, egl: (hg, nat(d, c), 0, 0)),
                  per_dir(LANES), per_dir(LANES),
                  pl.BlockSpec((1, hb, cb, CHUNK), lambda d, hg, c, egl: (d, hg, nat(d, c), 0)),
                  pl.BlockSpec((1, hb, HEAD_DIM, HEAD_DIM), lambda d, hg, c, egl: (d, hg, 0, 0))],
        out_specs=(pl.BlockSpec((1, hb, cb * CHUNK, HEAD_DIM), lambda d, hg, c, egl: (d, hg, nat(d, c), 0)),
                   pl.BlockSpec((1, hb, HEAD_DIM, HEAD_DIM), lambda d, hg, c, egl: (d, hg, 0, 0))),
        scratch_shapes=[pltpu.VMEM((hb, HEAD_DIM, HEAD_DIM), F32),
                        pltpu.VMEM((cb, hb, CHUNK, HEAD_DIM), BF16),
                        pltpu.VMEM((cb, hb, CHUNK, HEAD_DIM), F32)],
    )
    return pl.pallas_call(
        functools.partial(_gdn_scan_kernel, hb, cb),
        out_shape=(jax.ShapeDtypeStruct((2, HEADS, t, HEAD_DIM), BF16),
                   jax.ShapeDtypeStruct((2, HEADS, HEAD_DIM, HEAD_DIM), F32)),
        grid_spec=grid_spec,
        compiler_params=_cparams(("arbitrary", "arbitrary", "arbitrary")),
        name="gdn_scan",
    )(egl, qkv, qkv, qkv, kt.reshape(HEADS, nc, HEAD_DIM, CHUNK),
      twtu.reshape(2, HEADS, nc, CHUNK, LANES), atde.reshape(2, HEADS, nc, CHUNK, LANES), ep, s0)


def _gdn(qkv, bg, s0):
    t = qkv.shape[1]
    nc = t // CHUNK
    u = HEADS * nc
    up = -(-u // UNITS) * UNITS
    to_soa = lambda a: jnp.pad(jnp.transpose(a.reshape(nc, CHUNK, 2, HEADS), (2, 1, 3, 0)).reshape(2, CHUNK, u),
                               ((0, 0), (0, 0), (0, up - u)))
    beta_s = to_soa(bg[:, :2 * HEADS])
    g_s = to_soa(bg[:, 2 * HEADS:4 * HEADS])
    gc_s, ep_s, egl_s = _gdn_scal_call(g_s)
    gram, qk, kt = _gdn_gram_call(qkv, nc)
    pad_u = lambda a: jnp.pad(a.reshape(u, -1), ((0, up - u), (0, 0)))
    g_flat, qk_flat = pad_u(gram), pad_u(qk)
    twtu, atde = (a[:, :u] for a in _gdn_inv_call(g_flat, qk_flat, beta_s, gc_s))
    ep = jnp.transpose(ep_s[:, :, :u].reshape(2, CHUNK, HEADS, nc), (0, 2, 3, 1))
    egl = egl_s[:, 0, :u].reshape(2, HEADS, nc)
    return _gdn_scan_call(qkv, kt, twtu, atde, ep, egl, s0, nc)


def _merge_kernel(o_ref, zs_ref, yc_ref, ga_ref, gb_ref, wa_ref, wb_ref, m_ref, yg_s):
    @pl.when(pl.program_id(1) == 0)
    def _():
        for hh in range(HEADS):
            o = o_ref[0, hh].astype(F32) + o_ref[1, hh].astype(F32)
            y = o * lax.rsqrt(jnp.mean(o * o, axis=-1, keepdims=True) + RMS_EPS) * zs_ref[hh]
            yg_s[:, hh * HEAD_DIM:(hh + 1) * HEAD_DIM] = y.astype(BF16)

    pa = jnp.dot(yg_s[...], wa_ref[...], preferred_element_type=F32)
    pb = jnp.dot(yc_ref[...], wb_ref[...], preferred_element_type=F32)
    m_ref[...] = (ga_ref[...].astype(F32) * pa + gb_ref[...].astype(F32) * pb).astype(m_ref.dtype)


def _merge_call(o, zs, yconv, gates, wa, wb):
    t = yconv.shape[0]
    tm, tn = 512, 1024
    nj = D_MODEL // tn
    return pl.pallas_call(
        _merge_kernel,
        out_shape=jax.ShapeDtypeStruct((t, D_MODEL), BF16),
        grid=(t // tm, nj),
        in_specs=[pl.BlockSpec((2, HEADS, tm, HEAD_DIM), lambda i, j: (0, 0, i, 0)),
                  pl.BlockSpec((HEADS, tm, HEAD_DIM), lambda i, j: (0, i, 0)),
                  pl.BlockSpec((tm, CONV_WIDTH), lambda i, j: (i, 0)),
                  pl.BlockSpec((tm, tn), lambda i, j: (i, j)),
                  pl.BlockSpec((tm, tn), lambda i, j: (i, nj + j)),
                  pl.BlockSpec((GDN_WIDTH, tn), lambda i, j: (0, j)),
                  pl.BlockSpec((CONV_WIDTH, tn), lambda i, j: (0, j))],
        out_specs=pl.BlockSpec((tm, tn), lambda i, j: (i, j)),
        scratch_shapes=[pltpu.VMEM((tm, GDN_WIDTH), BF16)],
        compiler_params=_cparams(("arbitrary", "arbitrary")),
        name="merge",
    )(o, zs, yconv, gates, gates, wa, wb)


LN_ROWS = 64


def _outmm_kernel(m_ref, w_ref, o_ref):
    o_ref[...] = jnp.dot(m_ref[...], w_ref[...], preferred_element_type=F32)


def _outmm_call(m, w_out):
    t = m.shape[0]
    tm, tn = 512, 1024
    return pl.pallas_call(
        _outmm_kernel,
        out_shape=jax.ShapeDtypeStruct((t, D_MODEL), F32),
        grid=(D_MODEL // tn, t // tm),
        in_specs=[pl.BlockSpec((tm, D_MODEL), lambda j, i: (i, 0)),
                  pl.BlockSpec((D_MODEL, tn), lambda j, i: (0, j))],
        out_specs=pl.BlockSpec((tm, tn), lambda j, i: (i, j)),
        compiler_params=_cparams(("arbitrary", "arbitrary")),
        name="out_proj",
    )(m, w_out)


HALF = D_MODEL // 2


def _bf16_bits(v):
    return pltpu.bitcast(v.astype(BF16).astype(F32), jnp.uint32)


def _post_mix_kernel(mix_ref, x_ref, mod_ref, ln_ref, wrh_ref, wrl_ref, x1_ref, tokp_ref, logit_ref, tok_s):
    def rows_step(ci, c):
        sl = pl.ds(pl.multiple_of(ci * LN_ROWS, LN_ROWS), LN_ROWS)
        x1 = _layer_norm(DEEPNORM_ALPHA * x_ref[sl, :] + mod_ref[0:1, :] * mix_ref[sl, :])
        x1 = x1 * ln_ref[0:1, :] + ln_ref[1:2, :]
        x1_ref[sl, :] = x1
        tok = _layer_norm(x1) * (1.0 + mod_ref[2:3, :]) + mod_ref[1:2, :]
        tok_s[sl, :] = tok
        tokp_ref[sl, :] = _bf16_bits(tok[:, :HALF]) | (_bf16_bits(tok[:, HALF:]) >> 16)
        return c

    lax.fori_loop(0, x_ref.shape[0] // LN_ROWS, rows_step, 0)
    tok = tok_s[...]
    t_hi = tok.astype(BF16)
    t_lo = (tok - t_hi.astype(F32)).astype(BF16)
    logit_ref[...] = (jnp.dot(t_hi, wrh_ref[...], preferred_element_type=F32)
                      + jnp.dot(t_lo, wrh_ref[...], preferred_element_type=F32)
                      + jnp.dot(t_hi, wrl_ref[...], preferred_element_type=F32))


def _post_mix_call(mix, x, mod3, ln1, w_router):
    wr_hi = w_router.astype(BF16)
    wr_lo = (w_router - wr_hi.astype(F32)).astype(BF16)
    t = x.shape[0]
    tm = 256
    row = pl.BlockSpec((tm, D_MODEL), lambda i: (i, 0))
    vec = pl.BlockSpec((SUBLANES, D_MODEL), lambda i: (0, 0))
    wr = pl.BlockSpec((D_MODEL, LANES), lambda i: (0, 0))
    return pl.pallas_call(
        _post_mix_kernel,
        out_shape=(jax.ShapeDtypeStruct((t, D_MODEL), F32),
                   jax.ShapeDtypeStruct((t, HALF), jnp.uint32),
                   jax.ShapeDtypeStruct((t, LANES), F32)),
        grid=(t // tm,),
        in_specs=[row, row, vec, vec, wr, wr],
        out_specs=(row, pl.BlockSpec((tm, HALF), lambda i: (i, 0)), pl.BlockSpec((tm, LANES), lambda i: (i, 0))),
        scratch_shapes=[pltpu.VMEM((tm, D_MODEL), F32)],
        compiler_params=_cparams(("arbitrary",)),
        name="post_mix",
    )(mix, x, mod3, ln1, wr_hi, wr_lo)


MOE_BM = 256
MOE_HC = 256
DMA_BURST = 8


def _gather_kernel(rows, nu_ref, idx_ref, nxt_ref, src_ref, dst_ref, buf, sem):
    i = pl.program_id(0)
    nu = nu_ref[0]
    slot = i % 2

    def copy(iref, s, r):
        return pltpu.make_async_copy(src_ref.at[pl.ds(iref[0, 0, r], 1), :],
                                     buf.at[s, pl.ds(r, 1), :], sem.at[s])

    def issue(iref, s):
        def body(r8, c):
            for u in range(DMA_BURST):
                copy(iref, s, r8 * DMA_BURST + u).start(priority=u % 2)
            return c
        lax.fori_loop(0, rows // DMA_BURST, body, 0)

    @pl.when(i == 0)
    def _():
        issue(idx_ref, 0)

    @pl.when(i + 1 < nu)
    def _():
        issue(nxt_ref, 1 - slot)

    @pl.when(i < nu)
    def _():
        def body(r, c):
            copy(idx_ref, slot, r).wait()
            return c
        lax.fori_loop(0, rows, body, 0, unroll=8)
        packed = buf[slot]
        dst_ref[:, :HALF] = pltpu.bitcast(packed & jnp.uint32(0xFFFF0000), F32).astype(BF16)
        dst_ref[:, HALF:] = pltpu.bitcast(packed << 16, F32).astype(BF16)

    @pl.when(i >= nu)
    def _():
        dst_ref[...] = jnp.zeros(dst_ref.shape, dst_ref.dtype)


def _gather_call(src, idx, n_used):
    n = idx.shape[0]
    rows = MOE_BM
    nblk = n // rows
    idx3 = idx.reshape(nblk, 1, rows)
    grid_spec = pltpu.PrefetchScalarGridSpec(
        num_scalar_prefetch=1,
        grid=(nblk,),
        in_specs=[pl.BlockSpec((1, 1, rows), lambda i, nu: (i, 0, 0), memory_space=pltpu.SMEM),
                  pl.BlockSpec((1, 1, rows), lambda i, nu: (jnp.minimum(i + 1, nblk - 1), 0, 0),
                               memory_space=pltpu.SMEM),
                  pl.BlockSpec(memory_space=pl.ANY)],
        out_specs=pl.BlockSpec((rows, D_MODEL), lambda i, nu: (i, 0)),
        scratch_shapes=[pltpu.VMEM((2, rows, src.shape[1]), src.dtype), pltpu.SemaphoreType.DMA((2,))],
    )
    return pl.pallas_call(
        functools.partial(_gather_kernel, rows),
        out_shape=jax.ShapeDtypeStruct((n, D_MODEL), BF16),
        grid_spec=grid_spec,
        compiler_params=_cparams(("arbitrary",)),
        name="moe_gather",
    )(n_used, idx3, idx3, src)


def _expert_changed(be_ref, nu_ref, b):
    bb = jnp.minimum(b, nu_ref[0] - 1)
    return (b == 0) | (be_ref[bb] != be_ref[jnp.maximum(bb - 1, 0)])


def _moe_hidden_kernel(be_ref, nu_ref, x_ref, wg_ref, wu_ref, h_ref, w_s):
    b = pl.program_id(0)
    hc = pl.program_id(1)
    live = b < nu_ref[0]

    @pl.when(live & _expert_changed(be_ref, nu_ref, b))
    def _():
        w_s[hc, :, :MOE_HC] = wg_ref[0].astype(BF16)
        w_s[hc, :, MOE_HC:] = wu_ref[0].astype(BF16)

    @pl.when(live)
    def _():
        gu = jnp.dot(x_ref[...], w_s[hc], preferred_element_type=F32)
        h_ref[...] = (_silu(gu[:, :MOE_HC]) * gu[:, MOE_HC:]).astype(h_ref.dtype)

    @pl.when(jnp.logical_not(live))
    def _():
        h_ref[...] = jnp.zeros(h_ref.shape, h_ref.dtype)


def _moe_down_kernel(be_ref, nu_ref, h_ref, wd_ref, y_ref, wd_s):
    b = pl.program_id(0)
    live = b < nu_ref[0]

    @pl.when(live & _expert_changed(be_ref, nu_ref, b))
    def _():
        wd_s[...] = wd_ref[0].astype(BF16)

    @pl.when(live)
    def _():
        y_ref[...] = jnp.dot(h_ref[...], wd_s[...], preferred_element_type=F32)

    @pl.when(jnp.logical_not(live))
    def _():
        y_ref[...] = jnp.zeros(y_ref.shape, F32)


def _expert_call(xs, wg, wu, wd, block_e, n_used):
    n = xs.shape[0]
    nb = n // MOE_BM
    nh = EXPERT_HIDDEN // MOE_HC
    blk = lambda b, nu: jnp.minimum(b, nu[0] - 1)

    def w_chunk(b, h, be, nu):
        bb = blk(b, nu)
        first = (b < nu[0]) & ((bb == 0) | (be[bb] != be[jnp.maximum(bb - 1, 0)]))
        return be[bb], 0, jnp.where(first, h, nh - 1)

    hid = pl.pallas_call(
        _moe_hidden_kernel,
        out_shape=jax.ShapeDtypeStruct((n, EXPERT_HIDDEN), BF16),
        grid_spec=pltpu.PrefetchScalarGridSpec(
            num_scalar_prefetch=2,
            grid=(nb, nh),
            in_specs=[pl.BlockSpec((MOE_BM, D_MODEL), lambda b, h, be, nu: (blk(b, nu), 0)),
                      pl.BlockSpec((1, D_MODEL, MOE_HC), w_chunk),
                      pl.BlockSpec((1, D_MODEL, MOE_HC), w_chunk)],
            out_specs=pl.BlockSpec((MOE_BM, MOE_HC), lambda b, h, be, nu: (b, h)),
            scratch_shapes=[pltpu.VMEM((nh, D_MODEL, 2 * MOE_HC), BF16)]),
        compiler_params=_cparams(("arbitrary", "arbitrary")),
        name="moe_hidden",
    )(block_e, n_used, xs, wg, wu)
    return pl.pallas_call(
        _moe_down_kernel,
        out_shape=jax.ShapeDtypeStruct((n, D_MODEL), F32),
        grid_spec=pltpu.PrefetchScalarGridSpec(
            num_scalar_prefetch=2,
            grid=(nb,),
            in_specs=[pl.BlockSpec((MOE_BM, EXPERT_HIDDEN), lambda b, be, nu: (blk(b, nu), 0)),
                      pl.BlockSpec((1, EXPERT_HIDDEN, D_MODEL), lambda b, be, nu: (be[blk(b, nu)], 0, 0))],
            out_specs=pl.BlockSpec((MOE_BM, D_MODEL), lambda b, be, nu: (b, 0)),
            scratch_shapes=[pltpu.VMEM((EXPERT_HIDDEN, D_MODEL), BF16)]),
        compiler_params=_cparams(("arbitrary",)),
        name="moe_down",
    )(block_e, n_used, hid, wd)


def _combine_kernel(rows, d_ref, nxt_ref, y_ref, x1_ref, w_ref, mod_ref, ln_ref, o_ref, buf, sem):
    i = pl.program_id(0)
    slot = i % 2

    def copy(iref, s, k, r):
        return pltpu.make_async_copy(y_ref.at[pl.ds(iref[0, k, r], 1), :],
                                     buf.at[s, k, pl.ds(r, 1), :], sem.at[s])

    def issue(iref, s):
        def body(r8, c):
            for u in range(DMA_BURST):
                copy(iref, s, 0, r8 * DMA_BURST + u).start(priority=0)
                copy(iref, s, 1, r8 * DMA_BURST + u).start(priority=1)
            return c
        lax.fori_loop(0, rows // DMA_BURST, body, 0)

    @pl.when(i == 0)
    def _():
        issue(d_ref, 0)

    @pl.when(i + 1 < pl.num_programs(0))
    def _():
        issue(nxt_ref, 1 - slot)

    def wait_body(r, c):
        copy(d_ref, slot, 0, r).wait()
        copy(d_ref, slot, 1, r).wait()
        return c

    lax.fori_loop(0, rows, wait_body, 0, unroll=8)

    def rows_step(ci, c):
        sl = pl.ds(pl.multiple_of(ci * LN_ROWS, LN_ROWS), LN_ROWS)
        f = w_ref[sl, 0:1] * buf[slot, 0, sl, :] + w_ref[sl, 1:2] * buf[slot, 1, sl, :]
        y = _layer_norm(DEEPNORM_ALPHA * x1_ref[sl, :] + mod_ref[0:1, :] * f)
        o_ref[sl, :] = y * ln_ref[0:1, :] + ln_ref[1:2, :]
        return c

    lax.fori_loop(0, rows // LN_ROWS, rows_step, 0)


def _combine_call(ys, dest, weights, x1, mod_row, ln2):
    t = x1.shape[0]
    rows = 128
    nblk = t // rows
    dest3 = jnp.transpose(dest.reshape(nblk, rows, 2), (0, 2, 1))
    return pl.pallas_call(
        functools.partial(_combine_kernel, rows),
        out_shape=jax.ShapeDtypeStruct((t, D_MODEL), F32),
        grid=(nblk,),
        in_specs=[pl.BlockSpec((1, 2, rows), lambda i: (i, 0, 0), memory_space=pltpu.SMEM),
                  pl.BlockSpec((1, 2, rows), lambda i: (jnp.minimum(i + 1, nblk - 1), 0, 0),
                               memory_space=pltpu.SMEM),
                  pl.BlockSpec(memory_space=pl.ANY),
                  pl.BlockSpec((rows, D_MODEL), lambda i: (i, 0)),
                  pl.BlockSpec((rows, 2), lambda i: (i, 0)),
                  pl.BlockSpec((SUBLANES, D_MODEL), lambda i: (0, 0)),
                  pl.BlockSpec((SUBLANES, D_MODEL), lambda i: (0, 0))],
        out_specs=pl.BlockSpec((rows, D_MODEL), lambda i: (i, 0)),
        scratch_shapes=[pltpu.VMEM((2, 2, rows, D_MODEL), F32), pltpu.SemaphoreType.DMA((2,))],
        compiler_params=_cparams(("arbitrary",)),
        name="moe_combine",
    )(dest3, dest3, ys, x1, weights, mod_row, ln2)


def _route(logits, b_group, b_expert):
    t = logits.shape[0]
    p_group = jax.nn.softmax(logits[:, :N_GROUPS] + b_group, axis=-1)
    group = jnp.argmax(p_group, axis=-1)
    gate_group = jnp.take_along_axis(p_group, group[:, None], axis=-1)
    le = (logits[:, N_GROUPS:N_GROUPS + N_EXPERTS] + b_expert).reshape(t, N_GROUPS, EXPERTS_PER_GROUP)
    le = jnp.take_along_axis(le, group[:, None, None], axis=1)[:, 0]
    top_p, top_i = lax.top_k(jax.nn.softmax(le, axis=-1), 2)
    weights = gate_group * top_p / jnp.sum(top_p, axis=-1, keepdims=True)
    expert_id = group[:, None] * EXPERTS_PER_GROUP + top_i
    return expert_id.astype(jnp.int32), weights


def _moe(tok, logits, b_group, b_expert, wg, wu, wd, x1, mod_row, ln2):
    t = tok.shape[0]
    expert_id, weights = _route(logits, b_group, b_expert)
    e_flat = expert_id.reshape(-1)
    n_assign = e_flat.shape[0]
    onehot = (e_flat[:, None] == jnp.arange(N_EXPERTS)[None, :]).astype(jnp.int32)
    rank = jnp.take_along_axis(jnp.cumsum(onehot, axis=0), e_flat[:, None], axis=1)[:, 0] - 1
    counts = jnp.sum(onehot, axis=0)
    padded = (counts + MOE_BM - 1) // MOE_BM * MOE_BM
    pad_end = jnp.cumsum(padded)
    dest = (pad_end - padded)[e_flat] + rank
    n_blocks = -(-n_assign // MOE_BM) + N_EXPERTS
    n_rows = n_blocks * MOE_BM
    src_tok = jnp.zeros((n_rows,), jnp.int32).at[dest].set(jnp.arange(n_assign, dtype=jnp.int32) // 2)
    block_e = jnp.minimum(jnp.searchsorted(pad_end, jnp.arange(n_blocks) * MOE_BM, side="right"),
                          N_EXPERTS - 1).astype(jnp.int32)
    n_used = (pad_end[-1] // MOE_BM).astype(jnp.int32).reshape(1)
    xs = _gather_call(tok, src_tok, n_used)
    ys = _expert_call(xs, wg, wu, wd, block_e, n_used)
    return _combine_call(ys, dest.reshape(t, 2).astype(jnp.int32), weights, x1, mod_row, ln2)


def _pad_rows(v, rows=SUBLANES):
    return jnp.pad(v, ((0, rows - v.shape[0]), (0, 0)))


def _layer(x, ctx, c, c_ctx, w_ada, b_ada, w_in, conv_qkv, a_log, dt_bias, gdn_norm_w, conv_b,
           w_branch_a, w_branch_b, w_out, ln1_g, ln1_b, w_router_group, b_router_group,
           w_router_expert, b_router_expert, w_exp_gate, w_exp_up, w_exp_down, ln2_g, ln2_b):
    t = x.shape[0]
    mod = _mod_call(_pad_rows(jnp.stack([c, c_ctx])), w_ada, b_ada.reshape(1, -1))
    mod_lat = mod[0].reshape(N_MOD, D_MODEL)
    mod_ctx = mod[1].reshape(N_MOD, D_MODEL)

    n_a = 4 * GDN_WIDTH
    n_ba = 4 * HEADS
    w_pa = w_in[:, :n_a].astype(BF16)
    w_pb = w_in[:, n_a + n_ba:].astype(BF16)
    w_pc = jnp.pad(w_in[:, n_a:n_a + n_ba], ((0, 0), (0, LANES - n_ba))).astype(BF16)
    lane_row = lambda v: jnp.pad(v.reshape(1, -1), ((0, 0), (2 * HEADS, LANES - 4 * HEADS)))
    alog_row, dtb_row = lane_row(a_log), lane_row(dt_bias)

    h_ctx = _ln_mod_call(ctx, mod_ctx[0:1], mod_ctx[1:2])
    tc = ctx.shape[0]
    qkv_c = _inproj_qkv(h_ctx, w_pa, conv_qkv, tc, tc)
    bg_c = _inproj_ba(h_ctx, w_pc, alog_row, dtb_row, tc)
    s0 = jnp.zeros((2, HEADS, HEAD_DIM, HEAD_DIM), F32)
    _, s_ctx = _gdn(qkv_c, bg_c, s0)

    tm = 512
    h = _ln_mod_call(x, mod_lat[0:1], mod_lat[1:2])
    qkv = _inproj_qkv(h, w_pa, conv_qkv, GRID_W, tm)
    zs = _inproj_z(h, w_pa, gdn_norm_w.reshape(1, HEAD_DIM), tm)
    yconv = _inproj_xbc(h, w_pb, conv_b, GRID_W, tm)
    gates = _inproj_gates(h, w_pb, tm)
    bg = _inproj_ba(h, w_pc, alog_row, dtb_row, tm)
    o, _ = _gdn(qkv, bg, s_ctx)
    m = _merge_call(o, zs, yconv, gates, w_branch_a.astype(BF16), w_branch_b.astype(BF16))

    w_router = jnp.pad(jnp.concatenate([w_router_group, w_router_expert], axis=1),
                       ((0, 0), (0, LANES - N_GROUPS - N_EXPERTS)))
    mix = _outmm_call(m, w_out.astype(BF16))
    x1, tok, logits = _post_mix_call(mix, x, _pad_rows(mod_lat[2:5]),
                                     _pad_rows(jnp.stack([ln1_g, ln1_b])), w_router)

    return _moe(tok, logits, b_router_group, b_router_expert, w_exp_gate, w_exp_up, w_exp_down,
                x1, _pad_rows(mod_lat[5:6]), _pad_rows(jnp.stack([ln2_g, ln2_b])))


def kernel(x, c, ctx, c_ctx, w_ada, b_ada, w_in, conv_qkv, a_log, dt_bias, gdn_norm_w, conv_b,
           w_branch_a, w_branch_b, w_out, ln1_g, ln1_b, w_router_group, b_router_group,
           w_router_expert, b_router_expert, w_exp_gate, w_exp_up, w_exp_down, ln2_g, ln2_b):
    assert x.shape[0] == 1 and w_ada.shape[0] == 1, "single batch element, single layer"
    out = _layer(x[0], ctx[0], c[0], c_ctx, w_ada[0], b_ada[0], w_in[0], conv_qkv[0],
                 a_log[0].reshape(-1), dt_bias[0].reshape(-1), gdn_norm_w[0], conv_b[0],
                 w_branch_a[0], w_branch_b[0], w_out[0], ln1_g[0], ln1_b[0],
                 w_router_group[0], b_router_group[0], w_router_expert[0], b_router_expert[0],
                 w_exp_gate[0], w_exp_up[0], w_exp_down[0], ln2_g[0], ln2_b[0])
    return out[None]
```

```python
import functools

import numpy as np
import jax
import jax.numpy as jnp
from jax import lax
from jax.experimental import pallas as pl
from jax.experimental.pallas import tpu as pltpu

F32 = jnp.float32
BF16 = jnp.bfloat16
HIGHEST = lax.Precision.HIGHEST

D_MODEL = 4096
GRID_W = 64
CHUNK = 64
HEADS = 16
HEAD_DIM = 128
GDN_WIDTH = HEADS * HEAD_DIM
CONV_WIDTH = D_MODEL // 2
N_GROUPS = 8
EXPERTS_PER_GROUP = 8
N_EXPERTS = 64
EXPERT_HIDDEN = 768
N_MOD = 6
DEEPNORM_ALPHA = 2.0 ** 0.25
LN_EPS = 1e-6
RMS_EPS = 1e-6

LANES = 128
SUBLANES = 8
UNITS = 128
VMEM_LIMIT = 52 * 1024 * 1024


def _cparams(sem, vmem=VMEM_LIMIT):
    return pltpu.CompilerParams(dimension_semantics=sem, vmem_limit_bytes=vmem)


def _silu(x):
    return x * jax.nn.sigmoid(x)


def _layer_norm(x):
    mu = jnp.mean(x, axis=-1, keepdims=True)
    xc = x - mu
    var = jnp.mean(xc * xc, axis=-1, keepdims=True)
    return xc * lax.rsqrt(var + LN_EPS)


def _mod_kernel(c_ref, w_ref, b_ref, o_ref):
    s = _silu(c_ref[...])
    o_ref[...] = jnp.dot(s, w_ref[...], preferred_element_type=F32, precision=HIGHEST) + b_ref[...]


def _mod_call(cs, w_ada, b_ada):
    n = w_ada.shape[1]
    tn = 512
    return pl.pallas_call(
        _mod_kernel,
        out_shape=jax.ShapeDtypeStruct((SUBLANES, n), F32),
        grid=(n // tn,),
        in_specs=[pl.BlockSpec((SUBLANES, D_MODEL), lambda j: (0, 0)),
                  pl.BlockSpec((D_MODEL, tn), lambda j: (0, j)),
                  pl.BlockSpec((1, tn), lambda j: (0, j))],
        out_specs=pl.BlockSpec((SUBLANES, tn), lambda j: (0, j)),
        compiler_params=_cparams(("arbitrary",)),
        name="mod",
    )(cs, w_ada, b_ada)


def _ln_mod_kernel(x_ref, shift_ref, scale_ref, o_ref):
    y = _layer_norm(x_ref[...])
    o_ref[...] = (y * (1.0 + scale_ref[...]) + shift_ref[...]).astype(o_ref.dtype)


def _ln_mod_call(x, shift, scale):
    t = x.shape[0]
    tr = min(256, t)
    return pl.pallas_call(
        _ln_mod_kernel,
        out_shape=jax.ShapeDtypeStruct((t, D_MODEL), BF16),
        grid=(t // tr,),
        in_specs=[pl.BlockSpec((tr, D_MODEL), lambda i: (i, 0)),
                  pl.BlockSpec((1, D_MODEL), lambda i: (0, 0)),
                  pl.BlockSpec((1, D_MODEL), lambda i: (0, 0))],
        out_specs=pl.BlockSpec((tr, D_MODEL), lambda i: (i, 0)),
        compiler_params=_cparams(("arbitrary",)),
        name="ln_mod",
    )(x, shift, scale)


COL_QKV = 0
COL_Z = 3 * GDN_WIDTH
COL_XB = 0
COL_BG = CONV_WIDTH
COL_CG = 2 * CONV_WIDTH
COL_GATES = 3 * CONV_WIDTH
COL_BA = 0


def _conv3_rows(x, taps_ref, group):
    rows = x.shape[0]
    pos = lax.broadcasted_iota(jnp.int32, x.shape, 0) % group
    prev = jnp.where(pos == 0, 0.0, pltpu.roll(x, 1, 0))
    nxt = jnp.where(pos == group - 1, 0.0, pltpu.roll(x, rows - 1, 0))
    return taps_ref[0:1, :] * prev + taps_ref[1:2, :] * x + taps_ref[2:3, :] * nxt


def _ip_qkv_kernel(group, x_ref, w_ref, taps_ref, o_ref):
    acc = jnp.dot(x_ref[...], w_ref[...], preferred_element_type=F32)
    y = _silu(_conv3_rows(acc, taps_ref, group))
    j = pl.program_id(0)
    tiles_per_part = GDN_WIDTH // acc.shape[1]
    for hh in range(acc.shape[1] // HEAD_DIM):
        ys = y[:, hh * HEAD_DIM:(hh + 1) * HEAD_DIM]
        r = lax.rsqrt(jnp.sum(ys * ys, axis=-1, keepdims=True) + 1e-6)
        scale = jnp.where(j < tiles_per_part, r * HEAD_DIM ** -0.5,
                          jnp.where(j < 2 * tiles_per_part, r, 1.0))
        o_ref[hh] = (ys * scale).astype(o_ref.dtype)


def _ip_z_kernel(x_ref, w_ref, nw_ref, o_ref):
    acc = jnp.dot(x_ref[...], w_ref[...], preferred_element_type=F32)
    for hh in range(acc.shape[1] // HEAD_DIM):
        zs = acc[:, hh * HEAD_DIM:(hh + 1) * HEAD_DIM]
        o_ref[hh] = _silu(zs) * nw_ref[...]


def _ip_xbc_kernel(group, x_ref, wxb_ref, wbg_ref, wcg_ref, taps_ref, o_ref):
    x = x_ref[...]
    xb = jnp.dot(x, wxb_ref[...], preferred_element_type=F32)
    bg = jnp.dot(x, wbg_ref[...], preferred_element_type=F32)
    cg = jnp.dot(x, wcg_ref[...], preferred_element_type=F32)
    o_ref[...] = (bg * _conv3_rows(cg * xb, taps_ref, group)).astype(o_ref.dtype)


def _ip_gates_kernel(x_ref, w_ref, o_ref):
    acc = jnp.dot(x_ref[...], w_ref[...], preferred_element_type=F32)
    o_ref[...] = jax.nn.sigmoid(acc).astype(o_ref.dtype)


def _ip_ba_kernel(x_ref, w_ref, alog_ref, dtb_ref, o_ref):
    acc = jnp.dot(x_ref[...], w_ref[...], preferred_element_type=F32)
    lane = lax.broadcasted_iota(jnp.int32, acc.shape, 1)
    a = acc + dtb_ref[...]
    softplus = jnp.maximum(a, 0.0) + jnp.log(1.0 + jnp.exp(-jnp.abs(a)))
    o_ref[...] = jnp.where(lane < 2 * HEADS, jax.nn.sigmoid(acc), -jnp.exp(alog_ref[...]) * softplus)


def _inproj(kernel, h, w_cat, col_starts, tn, n_tiles, extra, extra_specs, out_shape, out_spec, tm):
    t = h.shape[0]
    w_specs = [pl.BlockSpec((D_MODEL, tn), functools.partial(lambda j, i, o: (0, o + j), o=c // tn))
               for c in col_starts]
    return pl.pallas_call(
        kernel,
        out_shape=out_shape,
        grid=(n_tiles, t // tm),
        in_specs=[pl.BlockSpec((tm, D_MODEL), lambda j, i: (i, 0))] + w_specs + extra_specs,
        out_specs=out_spec,
        compiler_params=_cparams(("arbitrary", "arbitrary")),
        name="inproj",
    )(h, *([w_cat] * len(col_starts)), *extra)


def _inproj_qkv(h, w_cat, taps, group, tm):
    t = h.shape[0]
    tn = 1024
    hpt = tn // HEAD_DIM
    return _inproj(
        functools.partial(_ip_qkv_kernel, group), h, w_cat, [COL_QKV], tn, 3 * GDN_WIDTH // tn,
        [taps], [pl.BlockSpec((3, tn), lambda j, i: (0, j))],
        jax.ShapeDtypeStruct((3 * HEADS, t, HEAD_DIM), BF16),
        pl.BlockSpec((hpt, tm, HEAD_DIM), lambda j, i: (j, i, 0)), tm)


def _inproj_z(h, w_cat, norm_w, tm):
    t = h.shape[0]
    tn = 1024
    hpt = tn // HEAD_DIM
    return _inproj(
        _ip_z_kernel, h, w_cat, [COL_Z], tn, GDN_WIDTH // tn,
        [norm_w], [pl.BlockSpec((1, HEAD_DIM), lambda j, i: (0, 0))],
        jax.ShapeDtypeStruct((HEADS, t, HEAD_DIM), F32),
        pl.BlockSpec((hpt, tm, HEAD_DIM), lambda j, i: (j, i, 0)), tm)


def _inproj_xbc(h, w_cat, taps, group, tm):
    t = h.shape[0]
    tn = 512
    return _inproj(
        functools.partial(_ip_xbc_kernel, group), h, w_cat, [COL_XB, COL_BG, COL_CG], tn,
        CONV_WIDTH // tn, [taps], [pl.BlockSpec((3, tn), lambda j, i: (0, j))],
        jax.ShapeDtypeStruct((t, CONV_WIDTH), BF16),
        pl.BlockSpec((tm, tn), lambda j, i: (i, j)), tm)


def _inproj_gates(h, w_cat, tm):
    t = h.shape[0]
    tn = 1024
    return _inproj(
        _ip_gates_kernel, h, w_cat, [COL_GATES], tn, 2 * D_MODEL // tn, [], [],
        jax.ShapeDtypeStruct((t, 2 * D_MODEL), BF16),
        pl.BlockSpec((tm, tn), lambda j, i: (i, j)), tm)


def _inproj_ba(h, w_cat, alog_row, dtb_row, tm):
    t = h.shape[0]
    return _inproj(
        _ip_ba_kernel, h, w_cat, [COL_BA], LANES, 1, [alog_row, dtb_row],
        [pl.BlockSpec((1, LANES), lambda j, i: (0, 0))] * 2,
        jax.ShapeDtypeStruct((t, LANES), F32),
        pl.BlockSpec((tm, LANES), lambda j, i: (i, 0)), tm)


def _gdn_scal_kernel(g_ref, gc_ref, ep_ref, egl_ref):
    d = pl.program_id(0)
    r = lax.broadcasted_iota(jnp.int32, (CHUNK, CHUNK), 0)
    c = lax.broadcasted_iota(jnp.int32, (CHUNK, CHUNK), 1)
    tri = jnp.where(d == 0, (c <= r).astype(F32), (c >= r).astype(F32))
    gc = jnp.dot(tri, g_ref[0], preferred_element_type=F32, precision=HIGHEST)
    gl = jnp.where(d == 0, gc[CHUNK - 1:CHUNK, :], gc[0:1, :])
    gc_ref[0] = gc
    ep_ref[0] = jnp.exp(gl - gc)
    egl_ref[0] = jnp.exp(gl)


def _gdn_scal_call(g_s):
    u = g_s.shape[2]
    tu = 512 if u % 512 == 0 else UNITS
    spec = pl.BlockSpec((1, CHUNK, tu), lambda d, i: (d, 0, i))
    return pl.pallas_call(
        _gdn_scal_kernel,
        out_shape=(jax.ShapeDtypeStruct(g_s.shape, F32), jax.ShapeDtypeStruct(g_s.shape, F32),
                   jax.ShapeDtypeStruct((2, 1, u), F32)),
        grid=(2, u // tu),
        in_specs=[spec],
        out_specs=(spec, spec, pl.BlockSpec((1, 1, tu), lambda d, i: (d, 0, i))),
        compiler_params=_cparams(("arbitrary", "arbitrary")),
        name="gdn_scal",
    )(g_s)


def _nt_dot(a, b):
    return lax.dot_general(a, b, (((1,), (1,)), ((), ())), preferred_element_type=F32)


def _gdn_gram_kernel(cb, q_ref, k_ref, g_ref, qk_ref, kt_ref):
    r = lax.broadcasted_iota(jnp.int32, (HEAD_DIM, HEAD_DIM), 0)
    c = lax.broadcasted_iota(jnp.int32, (HEAD_DIM, HEAD_DIM), 1)
    eye = (r == c).astype(BF16)
    for ci in range(cb):
        k = k_ref[0, ci * CHUNK:(ci + 1) * CHUNK, :]
        q = q_ref[0, ci * CHUNK:(ci + 1) * CHUNK, :]
        g_ref[ci] = _nt_dot(k, k)
        qk_ref[ci] = _nt_dot(q, k)
        kt_ref[ci] = _nt_dot(eye, k).astype(BF16)


def _gdn_gram_call(qkv, nc):
    cb = min(32, nc)
    ncb = nc // cb
    u = HEADS * nc
    return pl.pallas_call(
        functools.partial(_gdn_gram_kernel, cb),
        out_shape=(jax.ShapeDtypeStruct((u, CHUNK, CHUNK), F32),
                   jax.ShapeDtypeStruct((u, CHUNK, CHUNK), F32),
                   jax.ShapeDtypeStruct((u, HEAD_DIM, CHUNK), BF16)),
        grid=(HEADS, ncb),
        in_specs=[pl.BlockSpec((1, cb * CHUNK, HEAD_DIM), lambda h, c: (h, c, 0)),
                  pl.BlockSpec((1, cb * CHUNK, HEAD_DIM), lambda h, c: (HEADS + h, c, 0))],
        out_specs=(pl.BlockSpec((cb, CHUNK, CHUNK), lambda h, c: (h * ncb + c, 0, 0)),
                   pl.BlockSpec((cb, CHUNK, CHUNK), lambda h, c: (h * ncb + c, 0, 0)),
                   pl.BlockSpec((cb, HEAD_DIM, CHUNK), lambda h, c: (h * ncb + c, 0, 0))),
        compiler_params=_cparams(("arbitrary", "arbitrary")),
        name="gdn_gram",
    )(qkv, qkv)


NB = CHUNK // SUBLANES


def _row_bcast(ref, row):
    return jnp.broadcast_to(ref[pl.ds(row, 1), :], (SUBLANES, UNITS))


def _gdn_inv_kernel(g_ref, qk_ref, beta_ref, gc_ref, twtu_ref, atde_ref, *scratch):
    for bwd in (False, True):
        @pl.when(pl.program_id(0) == int(bwd))
        def _(bwd=bwd):
            _gdn_inv_body(bwd, g_ref, qk_ref, beta_ref, gc_ref, twtu_ref.at[0], atde_ref.at[0], *scratch)


def _gdn_inv_body(bwd, g_ref, qk_ref, beta_ref, gc_ref, twtu_ref, atde_ref,
                  g_s, qk_s, l_s, t_s, tw_s, tu_s, at_s, de_s, e_s):
    pos = (lambda a: CHUNK - 1 - a) if bwd else (lambda a: a)
    blk = (lambda b: NB - 1 - b) if bwd else (lambda b: b)

    for pp in range(CHUNK // 2):
        sl = slice(pp * LANES, (pp + 1) * LANES)
        g_s[sl, :] = g_ref[:, sl].T
        qk_s[sl, :] = qk_ref[:, sl].T
    e_s[...] = jnp.exp(gc_ref[0])

    @pl.when(pl.program_id(1) == 0)
    def _():
        for ref in (tw_s, tu_s, at_s, de_s):
            ref[...] = jnp.zeros(ref.shape, F32)

    sub = lax.broadcasted_iota(jnp.int32, (SUBLANES, UNITS), 0)
    zero = jnp.zeros((SUBLANES, UNITS), F32)
    rs = range(SUBLANES)
    own = [(SUBLANES - 1 - r) if bwd else r for r in rs]
    earlier = [(sub > own[r]) if bwd else (sub < own[r]) for r in rs]

    def tile_ds(p, b):
        return pl.ds(pl.multiple_of(p * CHUNK + b * SUBLANES, SUBLANES), SUBLANES)

    def cols_ds(b):
        return pl.ds(pl.multiple_of(b * SUBLANES, SUBLANES), SUBLANES)

    def row_block(ib, carry):
        b_own = blk(ib)
        ps = [pos(ib * SUBLANES + r) for r in rs]
        gc_p = [_row_bcast(gc_ref.at[0], p) for p in ps]
        beta_p = [_row_bcast(beta_ref.at[0], p) for p in ps]

        def weights_offdiag(bc, c):
            b = blk(bc)
            gc_c = gc_ref[0, cols_ds(b), :]
            for r in rs:
                dec = jnp.exp(gc_p[r] - gc_c)
                l_s[tile_ds(ps[r], b), :] = beta_p[r] * g_s[tile_ds(ps[r], b), :] * dec
                at_s[tile_ds(ps[r], b), :] = qk_s[tile_ds(ps[r], b), :] * dec
            return c

        lax.fori_loop(0, ib, weights_offdiag, 0)
        gc_c = gc_ref[0, cols_ds(b_own), :]
        for r in rs:
            t = tile_ds(ps[r], b_own)
            dec = jnp.exp(gc_p[r] - gc_c)
            l_s[t, :] = jnp.where(earlier[r], beta_p[r] * g_s[t, :] * dec, 0.0)
            at_s[t, :] = jnp.where(sub == own[r], qk_s[t, :], jnp.where(earlier[r], qk_s[t, :] * dec, 0.0))
            de_s[t, :] = jnp.where(sub == own[r], _row_bcast(e_s, ps[r]), 0.0)

        def finish(b, acc):
            done = []
            for r in rs:
                a_r = acc[r]
                for kk in range(r):
                    a_r = a_r - _row_bcast(l_s, ps[r] * CHUNK + ps[kk]) * done[kk]
                done.append(a_r)
            beta_c = beta_ref[0, cols_ds(b), :]
            be_c = beta_c * e_s[cols_ds(b), :]
            for r in rs:
                t_s[tile_ds(ps[r], b), :] = done[r]
                tu_s[tile_ds(ps[r], b), :] = done[r] * beta_c
                tw_s[tile_ds(ps[r], b), :] = done[r] * be_c

        def subst_offdiag(bc, c):
            b = blk(bc)

            def k_block(kb, acc):
                acc = list(acc)
                for kk in rs:
                    pk = pos(kb * SUBLANES + kk)
                    t_k = t_s[tile_ds(pk, b), :]
                    for r in rs:
                        acc[r] = acc[r] - _row_bcast(l_s, ps[r] * CHUNK + pk) * t_k
                return tuple(acc)

            finish(b, lax.fori_loop(bc, ib, k_block, (zero,) * SUBLANES))
            return c

        lax.fori_loop(0, ib, subst_offdiag, 0)
        finish(b_own, [jnp.where(sub == own[r], 1.0, 0.0) for r in rs])
        return carry

    lax.fori_loop(0, NB, row_block, 0)

    for p in range(CHUNK):
        rows = slice(p * CHUNK, (p + 1) * CHUNK)
        sl = slice(p * LANES, (p + 1) * LANES)
        twtu_ref[:, sl] = jnp.concatenate([tw_s[rows, :], tu_s[rows, :]], axis=0).T.astype(BF16)
        atde_ref[:, sl] = jnp.concatenate([at_s[rows, :], de_s[rows, :]], axis=0).T.astype(BF16)


def _gdn_inv_call(g_flat, qk_flat, beta_s, gc_s):
    u = g_flat.shape[0]
    mat = pl.BlockSpec((UNITS, CHUNK * CHUNK), lambda d, i: (i, 0))
    sc = pl.BlockSpec((1, CHUNK, UNITS), lambda d, i: (d, 0, i))
    out = pl.BlockSpec((1, UNITS, CHUNK * LANES), lambda d, i: (d, i, 0))
    soa = pltpu.VMEM((CHUNK * CHUNK, UNITS), F32)
    return pl.pallas_call(
        _gdn_inv_kernel,
        out_shape=(jax.ShapeDtypeStruct((2, u, CHUNK * LANES), BF16),
                   jax.ShapeDtypeStruct((2, u, CHUNK * LANES), BF16)),
        grid=(2, u // UNITS),
        in_specs=[mat, mat, sc, sc],
        out_specs=(out, out),
        scratch_shapes=[soa] * 8 + [pltpu.VMEM((CHUNK, UNITS), F32)],
        compiler_params=_cparams(("arbitrary", "arbitrary")),
        name="gdn_inv",
    )(g_flat, qk_flat, beta_s, gc_s)


def _gdn_scan_kernel(hb, cb, egl_ref, q_ref, k_ref, v_ref, kt_ref, twtu_ref, atde_ref, ep_ref, s0_ref,
                     o_ref, sfin_ref, s_scr, w_scr, u_scr):
    d = pl.program_id(0)
    hg = pl.program_id(1)
    ci = pl.program_id(2)
    ncb = pl.num_programs(2)

    @pl.when(ci == 0)
    def _():
        s_scr[...] = s0_ref[0]

    cblk = ci + d * (ncb - 1 - 2 * ci)
    zeros = jnp.zeros((CHUNK, HEAD_DIM), BF16)

    def prepare(c, carry):
        r0 = pl.multiple_of(c * CHUNK, CHUNK)
        for hh in range(hb):
            k = k_ref[hh, pl.ds(r0, CHUNK), :]
            v = v_ref[hh, pl.ds(r0, CHUNK), :]
            rhs = jnp.concatenate([jnp.concatenate([k, zeros], axis=1),
                                   jnp.concatenate([zeros, v], axis=1)], axis=0)
            wu = jnp.dot(twtu_ref[0, hh, c], rhs, preferred_element_type=F32)
            w_scr[c, hh] = wu[:, :HEAD_DIM].astype(BF16)
            u_scr[c, hh] = wu[:, HEAD_DIM:]
        return carry

    lax.fori_loop(0, cb, prepare, 0)

    def chunk_step(cc, carry):
        c = cc + d * (cb - 1 - 2 * cc)
        r0 = pl.multiple_of(c * CHUNK, CHUNK)
        heads = range(hb)
        s = [s_scr[hh] for hh in heads]
        x = [jnp.dot(jnp.concatenate([w_scr[c, hh], q_ref[hh, pl.ds(r0, CHUNK), :]], axis=0),
                     s[hh].astype(BF16), preferred_element_type=F32) for hh in heads]
        v_new = [u_scr[c, hh] - x[hh][:CHUNK] for hh in heads]
        for hh in heads:
            kdt = (kt_ref[hh, c].astype(F32) * ep_ref[0, hh, pl.ds(c, 1), :]).astype(BF16)
            egl = egl_ref[d, hg * hb + hh, cblk * cb + c]
            s_scr[hh] = egl * s[hh] + jnp.dot(kdt, v_new[hh].astype(BF16), preferred_element_type=F32)
        for hh in heads:
            rhs = jnp.concatenate([v_new[hh], x[hh][CHUNK:]], axis=0).astype(BF16)
            o_ref[0, hh, pl.ds(r0, CHUNK), :] = jnp.dot(atde_ref[0, hh, c], rhs,
                                                        preferred_element_type=F32).astype(o_ref.dtype)
        return carry

    lax.fori_loop(0, cb, chunk_step, 0)

    @pl.when(ci == ncb - 1)
    def _():
        sfin_ref[0] = s_scr[...]


def _gdn_scan_call(qkv, kt, twtu, atde, ep, egl, s0, nc):
    t = qkv.shape[1]
    hb = HEADS
    cb = min(8, nc)
    ncb = nc // cb
    hgs = HEADS // hb
    nat = lambda d, c: c + d * (ncb - 1 - 2 * c)
    tok = lambda part: pl.BlockSpec((hb, cb * CHUNK, HEAD_DIM),
                                    lambda d, hg, c, egl, part=part: (part * hgs + hg, nat(d, c), 0))
    per_dir = lambda last: pl.BlockSpec((1, hb, cb, CHUNK, last),
                                        lambda d, hg, c, egl: (d, hg, nat(d, c), 0, 0))
    grid_spec = pltpu.PrefetchScalarGridSpec(
        num_scalar_prefetch=1,
        grid=(2, hgs, ncb),
        in_specs=[tok(0), tok(1), tok(2),
                  pl.BlockSpec((hb, cb, HEAD_DIM, CHUNK), lambda d, hg, c, egl: (hg, nat(d, c), 0, 0)),
                  per_dir(LANES), per_dir(LANES),
                  pl.BlockSpec((1, hb, cb, CHUNK), lambda d, hg, c, egl: (d, hg, nat(d, c), 0)),
                  pl.BlockSpec((1, hb, HEAD_DIM, HEAD_DIM), lambda d, hg, c, egl: (d, hg, 0, 0))],
        out_specs=(pl.BlockSpec((1, hb, cb * CHUNK, HEAD_DIM), lambda d, hg, c, egl: (d, hg, nat(d, c), 0)),
                   pl.BlockSpec((1, hb, HEAD_DIM, HEAD_DIM), lambda d, hg, c, egl: (d, hg, 0, 0))),
        scratch_shapes=[pltpu.VMEM((hb, HEAD_DIM, HEAD_DIM), F32),
                        pltpu.VMEM((cb, hb, CHUNK, HEAD_DIM), BF16),
                        pltpu.VMEM((cb, hb, CHUNK, HEAD_DIM), F32)],
    )
    return pl.pallas_call(
        functools.partial(_gdn_scan_kernel, hb, cb),
        out_shape=(jax.ShapeDtypeStruct((2, HEADS, t, HEAD_DIM), BF16),
                   jax.ShapeDtypeStruct((2, HEADS, HEAD_DIM, HEAD_DIM), F32)),
        grid_spec=grid_spec,
        compiler_params=_cparams(("arbitrary", "arbitrary", "arbitrary")),
        name="gdn_scan",
    )(egl, qkv, qkv, qkv, kt.reshape(HEADS, nc, HEAD_DIM, CHUNK),
      twtu.reshape(2, HEADS, nc, CHUNK, LANES), atde.reshape(2, HEADS, nc, CHUNK, LANES), ep, s0)


def _gdn(qkv, bg, s0):
    t = qkv.shape[1]
    nc = t // CHUNK
    u = HEADS * nc
    up = -(-u // UNITS) * UNITS
    to_soa = lambda a: jnp.pad(jnp.transpose(a.reshape(nc, CHUNK, 2, HEADS), (2, 1, 3, 0)).reshape(2, CHUNK, u),
                               ((0, 0), (0, 0), (0, up - u)))
    beta_s = to_soa(bg[:, :2 * HEADS])
    g_s = to_soa(bg[:, 2 * HEADS:4 * HEADS])
    gc_s, ep_s, egl_s = _gdn_scal_call(g_s)
    gram, qk, kt = _gdn_gram_call(qkv, nc)
    pad_u = lambda a: jnp.pad(a.reshape(u, -1), ((0, up - u), (0, 0)))
    g_flat, qk_flat = pad_u(gram), pad_u(qk)
    twtu, atde = (a[:, :u] for a in _gdn_inv_call(g_flat, qk_flat, beta_s, gc_s))
    ep = jnp.transpose(ep_s[:, :, :u].reshape(2, CHUNK, HEADS, nc), (0, 2, 3, 1))
    egl = egl_s[:, 0, :u].reshape(2, HEADS, nc)
    return _gdn_scan_call(qkv, kt, twtu, atde, ep, egl, s0, nc)


def _merge_kernel(o_ref, zs_ref, yc_ref, ga_ref, gb_ref, wa_ref, wb_ref, m_ref, yg_s):
    @pl.when(pl.program_id(1) == 0)
    def _():
        for hh in range(HEADS):
            o = o_ref[0, hh].astype(F32) + o_ref[1, hh].astype(F32)
            y = o * lax.rsqrt(jnp.mean(o * o, axis=-1, keepdims=True) + RMS_EPS) * zs_ref[hh]
            yg_s[:, hh * HEAD_DIM:(hh + 1) * HEAD_DIM] = y.astype(BF16)

    pa = jnp.dot(yg_s[...], wa_ref[...], preferred_element_type=F32)
    pb = jnp.dot(yc_ref[...], wb_ref[...], preferred_element_type=F32)
    m_ref[...] = (ga_ref[...].astype(F32) * pa + gb_ref[...].astype(F32) * pb).astype(m_ref.dtype)


def _merge_call(o, zs, yconv, gates, wa, wb):
    t = yconv.shape[0]
    tm, tn = 512, 1024
    nj = D_MODEL // tn
    return pl.pallas_call(
        _merge_kernel,
        out_shape=jax.ShapeDtypeStruct((t, D_MODEL), BF16),
        grid=(t // tm, nj),
        in_specs=[pl.BlockSpec((2, HEADS, tm, HEAD_DIM), lambda i, j: (0, 0, i, 0)),
                  pl.BlockSpec((HEADS, tm, HEAD_DIM), lambda i, j: (0, i, 0)),
                  pl.BlockSpec((tm, CONV_WIDTH), lambda i, j: (i, 0)),
                  pl.BlockSpec((tm, tn), lambda i, j: (i, j)),
                  pl.BlockSpec((tm, tn), lambda i, j: (i, nj + j)),
                  pl.BlockSpec((GDN_WIDTH, tn), lambda i, j: (0, j)),
                  pl.BlockSpec((CONV_WIDTH, tn), lambda i, j: (0, j))],
        out_specs=pl.BlockSpec((tm, tn), lambda i, j: (i, j)),
        scratch_shapes=[pltpu.VMEM((tm, GDN_WIDTH), BF16)],
        compiler_params=_cparams(("arbitrary", "arbitrary")),
        name="merge",
    )(o, zs, yconv, gates, gates, wa, wb)


LN_ROWS = 64


def _outmm_kernel(m_ref, w_ref, o_ref):
    o_ref[...] = jnp.dot(m_ref[...], w_ref[...], preferred_element_type=F32)


def _outmm_call(m, w_out):
    t = m.shape[0]
    tm, tn = 512, 1024
    return pl.pallas_call(
        _outmm_kernel,
        out_shape=jax.ShapeDtypeStruct((t, D_MODEL), F32),
        grid=(D_MODEL // tn, t // tm),
        in_specs=[pl.BlockSpec((tm, D_MODEL), lambda j, i: (i, 0)),
                  pl.BlockSpec((D_MODEL, tn), lambda j, i: (0, j))],
        out_specs=pl.BlockSpec((tm, tn), lambda j, i: (i, j)),
        compiler_params=_cparams(("arbitrary", "arbitrary")),
        name="out_proj",
    )(m, w_out)


HALF = D_MODEL // 2


def _bf16_bits(v):
    return pltpu.bitcast(v.astype(BF16).astype(F32), jnp.uint32)


def _post_mix_kernel(mix_ref, x_ref, mod_ref, ln_ref, wrh_ref, wrl_ref, x1_ref, tokp_ref, logit_ref, tok_s):
    def rows_step(ci, c):
        sl = pl.ds(pl.multiple_of(ci * LN_ROWS, LN_ROWS), LN_ROWS)
        x1 = _layer_norm(DEEPNORM_ALPHA * x_ref[sl, :] + mod_ref[0:1, :] * mix_ref[sl, :])
        x1 = x1 * ln_ref[0:1, :] + ln_ref[1:2, :]
        x1_ref[sl, :] = x1
        tok = _layer_norm(x1) * (1.0 + mod_ref[2:3, :]) + mod_ref[1:2, :]
        tok_s[sl, :] = tok
        tokp_ref[sl, :] = _bf16_bits(tok[:, :HALF]) | (_bf16_bits(tok[:, HALF:]) >> 16)
        return c

    lax.fori_loop(0, x_ref.shape[0] // LN_ROWS, rows_step, 0)
    tok = tok_s[...]
    t_hi = tok.astype(BF16)
    t_lo = (tok - t_hi.astype(F32)).astype(BF16)
    logit_ref[...] = (jnp.dot(t_hi, wrh_ref[...], preferred_element_type=F32)
                      + jnp.dot(t_lo, wrh_ref[...], preferred_element_type=F32)
                      + jnp.dot(t_hi, wrl_ref[...], preferred_element_type=F32))


def _post_mix_call(mix, x, mod3, ln1, w_router):
    wr_hi = w_router.astype(BF16)
    wr_lo = (w_router - wr_hi.astype(F32)).astype(BF16)
    t = x.shape[0]
    tm = 256
    row = pl.BlockSpec((tm, D_MODEL), lambda i: (i, 0))
    vec = pl.BlockSpec((SUBLANES, D_MODEL), lambda i: (0, 0))
    wr = pl.BlockSpec((D_MODEL, LANES), lambda i: (0, 0))
    return pl.pallas_call(
        _post_mix_kernel,
        out_shape=(jax.ShapeDtypeStruct((t, D_MODEL), F32),
                   jax.ShapeDtypeStruct((t, HALF), jnp.uint32),
                   jax.ShapeDtypeStruct((t, LANES), F32)),
        grid=(t // tm,),
        in_specs=[row, row, vec, vec, wr, wr],
        out_specs=(row, pl.BlockSpec((tm, HALF), lambda i: (i, 0)), pl.BlockSpec((tm, LANES), lambda i: (i, 0))),
        scratch_shapes=[pltpu.VMEM((tm, D_MODEL), F32)],
        compiler_params=_cparams(("arbitrary",)),
        name="post_mix",
    )(mix, x, mod3, ln1, wr_hi, wr_lo)


MOE_BM = 256
MOE_HC = 256


def _gather_kernel(rows, nu_ref, idx_ref, nxt_ref, src_ref, dst_ref, buf, sem):
    i = pl.program_id(0)
    nu = nu_ref[0]
    slot = i % 2

    def copy(iref, s, r):
        return pltpu.make_async_copy(src_ref.at[pl.ds(iref[0, 0, r], 1), :],
                                     buf.at[s, pl.ds(r, 1), :], sem.at[s])

    def issue(iref, s):
        def body(r, c):
            copy(iref, s, r).start()
            return c
        lax.fori_loop(0, rows, body, 0, unroll=8)

    @pl.when(i == 0)
    def _():
        issue(idx_ref, 0)

    @pl.when(i + 1 < nu)
    def _():
        issue(nxt_ref, 1 - slot)

    @pl.when(i < nu)
    def _():
        def body(r, c):
            copy(idx_ref, slot, r).wait()
            return c
        lax.fori_loop(0, rows, body, 0, unroll=8)
        packed = buf[slot]
        dst_ref[:, :HALF] = pltpu.bitcast(packed & jnp.uint32(0xFFFF0000), F32).astype(BF16)
        dst_ref[:, HALF:] = pltpu.bitcast(packed << 16, F32).astype(BF16)

    @pl.when(i >= nu)
    def _():
        dst_ref[...] = jnp.zeros(dst_ref.shape, dst_ref.dtype)


def _gather_call(src, idx, n_used):
    n = idx.shape[0]
    rows = MOE_BM
    nblk = n // rows
    idx3 = idx.reshape(nblk, 1, rows)
    grid_spec = pltpu.PrefetchScalarGridSpec(
        num_scalar_prefetch=1,
        grid=(nblk,),
        in_specs=[pl.BlockSpec((1, 1, rows), lambda i, nu: (i, 0, 0), memory_space=pltpu.SMEM),
                  pl.BlockSpec((1, 1, rows), lambda i, nu: (jnp.minimum(i + 1, nblk - 1), 0, 0),
                               memory_space=pltpu.SMEM),
                  pl.BlockSpec(memory_space=pl.ANY)],
        out_specs=pl.BlockSpec((rows, D_MODEL), lambda i, nu: (i, 0)),
        scratch_shapes=[pltpu.VMEM((2, rows, src.shape[1]), src.dtype), pltpu.SemaphoreType.DMA((2,))],
    )
    return pl.pallas_call(
        functools.partial(_gather_kernel, rows),
        out_shape=jax.ShapeDtypeStruct((n, D_MODEL), BF16),
        grid_spec=grid_spec,
        compiler_params=_cparams(("arbitrary",)),
        name="moe_gather",
    )(n_used, idx3, idx3, src)


def _expert_changed(be_ref, nu_ref, b):
    bb = jnp.minimum(b, nu_ref[0] - 1)
    return (b == 0) | (be_ref[bb] != be_ref[jnp.maximum(bb - 1, 0)])


def _moe_hidden_kernel(be_ref, nu_ref, x_ref, wg_ref, wu_ref, h_ref, w_s):
    b = pl.program_id(0)
    hc = pl.program_id(1)
    live = b < nu_ref[0]

    @pl.when(live & _expert_changed(be_ref, nu_ref, b))
    def _():
        w_s[hc, :, :MOE_HC] = wg_ref[0].astype(BF16)
        w_s[hc, :, MOE_HC:] = wu_ref[0].astype(BF16)

    @pl.when(live)
    def _():
        gu = jnp.dot(x_ref[...], w_s[hc], preferred_element_type=F32)
        h_ref[...] = (_silu(gu[:, :MOE_HC]) * gu[:, MOE_HC:]).astype(h_ref.dtype)

    @pl.when(jnp.logical_not(live))
    def _():
        h_ref[...] = jnp.zeros(h_ref.shape, h_ref.dtype)


def _moe_down_kernel(be_ref, nu_ref, h_ref, wd_ref, y_ref, wd_s):
    b = pl.program_id(0)
    live = b < nu_ref[0]

    @pl.when(live & _expert_changed(be_ref, nu_ref, b))
    def _():
        wd_s[...] = wd_ref[0].astype(BF16)

    @pl.when(live)
    def _():
        y_ref[...] = jnp.dot(h_ref[...], wd_s[...], preferred_element_type=F32)

    @pl.when(jnp.logical_not(live))
    def _():
        y_ref[...] = jnp.zeros(y_ref.shape, F32)


def _expert_call(xs, wg, wu, wd, block_e, n_used):
    n = xs.shape[0]
    nb = n // MOE_BM
    nh = EXPERT_HIDDEN // MOE_HC
    blk = lambda b, nu: jnp.minimum(b, nu[0] - 1)

    def w_chunk(b, h, be, nu):
        bb = blk(b, nu)
        first = (b < nu[0]) & ((bb == 0) | (be[bb] != be[jnp.maximum(bb - 1, 0)]))
        return be[bb], 0, jnp.where(first, h, nh - 1)

    hid = pl.pallas_call(
        _moe_hidden_kernel,
        out_shape=jax.ShapeDtypeStruct((n, EXPERT_HIDDEN), BF16),
        grid_spec=pltpu.PrefetchScalarGridSpec(
            num_scalar_prefetch=2,
            grid=(nb, nh),
            in_specs=[pl.BlockSpec((MOE_BM, D_MODEL), lambda b, h, be, nu: (blk(b, nu), 0)),
                      pl.BlockSpec((1, D_MODEL, MOE_HC), w_chunk),
                      pl.BlockSpec((1, D_MODEL, MOE_HC), w_chunk)],
            out_specs=pl.BlockSpec((MOE_BM, MOE_HC), lambda b, h, be, nu: (b, h)),
            scratch_shapes=[pltpu.VMEM((nh, D_MODEL, 2 * MOE_HC), BF16)]),
        compiler_params=_cparams(("arbitrary", "arbitrary")),
        name="moe_hidden",
    )(block_e, n_used, xs, wg, wu)
    return pl.pallas_call(
        _moe_down_kernel,
        out_shape=jax.ShapeDtypeStruct((n, D_MODEL), F32),
        grid_spec=pltpu.PrefetchScalarGridSpec(
            num_scalar_prefetch=2,
            grid=(nb,),
            in_specs=[pl.BlockSpec((MOE_BM, EXPERT_HIDDEN), lambda b, be, nu: (blk(b, nu), 0)),
                      pl.BlockSpec((1, EXPERT_HIDDEN, D_MODEL), lambda b, be, nu: (be[blk(b, nu)], 0, 0))],
            out_specs=pl.BlockSpec((MOE_BM, D_MODEL), lambda b, be, nu: (b, 0)),
            scratch_shapes=[pltpu.VMEM((EXPERT_HIDDEN, D_MODEL), BF16)]),
        compiler_params=_cparams(("arbitrary",)),
        name="moe_down",
    )(block_e, n_used, hid, wd)


def _combine_kernel(rows, d_ref, nxt_ref, y_ref, x1_ref, w_ref, mod_ref, ln_ref, o_ref, buf, sem):
    i = pl.program_id(0)
    slot = i % 2

    def copy(iref, s, k, r):
        return pltpu.make_async_copy(y_ref.at[pl.ds(iref[0, k, r], 1), :],
                                     buf.at[s, k, pl.ds(r, 1), :], sem.at[s])

    def issue(iref, s):
        def body(r, c):
            copy(iref, s, 0, r).start()
            copy(iref, s, 1, r).start()
            return c
        lax.fori_loop(0, rows, body, 0, unroll=8)

    @pl.when(i == 0)
    def _():
        issue(d_ref, 0)

    @pl.when(i + 1 < pl.num_programs(0))
    def _():
        issue(nxt_ref, 1 - slot)

    def wait_body(r, c):
        copy(d_ref, slot, 0, r).wait()
        copy(d_ref, slot, 1, r).wait()
        return c

    lax.fori_loop(0, rows, wait_body, 0, unroll=8)

    def rows_step(ci, c):
        sl = pl.ds(pl.multiple_of(ci * LN_ROWS, LN_ROWS), LN_ROWS)
        f = w_ref[sl, 0:1] * buf[slot, 0, sl, :] + w_ref[sl, 1:2] * buf[slot, 1, sl, :]
        y = _layer_norm(DEEPNORM_ALPHA * x1_ref[sl, :] + mod_ref[0:1, :] * f)
        o_ref[sl, :] = y * ln_ref[0:1, :] + ln_ref[1:2, :]
        return c

    lax.fori_loop(0, rows // LN_ROWS, rows_step, 0)


def _combine_call(ys, dest, weights, x1, mod_row, ln2):
    t = x1.shape[0]
    rows = 128
    nblk = t // rows
    dest3 = jnp.transpose(dest.reshape(nblk, rows, 2), (0, 2, 1))
    return pl.pallas_call(
        functools.partial(_combine_kernel, rows),
        out_shape=jax.ShapeDtypeStruct((t, D_MODEL), F32),
        grid=(nblk,),
        in_specs=[pl.BlockSpec((1, 2, rows), lambda i: (i, 0, 0), memory_space=pltpu.SMEM),
                  pl.BlockSpec((1, 2, rows), lambda i: (jnp.minimum(i + 1, nblk - 1), 0, 0),
                               memory_space=pltpu.SMEM),
                  pl.BlockSpec(memory_space=pl.ANY),
                  pl.BlockSpec((rows, D_MODEL), lambda i: (i, 0)),
                  pl.BlockSpec((rows, 2), lambda i: (i, 0)),
                  pl.BlockSpec((SUBLANES, D_MODEL), lambda i: (0, 0)),
                  pl.BlockSpec((SUBLANES, D_MODEL), lambda i: (0, 0))],
        out_specs=pl.BlockSpec((rows, D_MODEL), lambda i: (i, 0)),
        scratch_shapes=[pltpu.VMEM((2, 2, rows, D_MODEL), F32), pltpu.SemaphoreType.DMA((2,))],
        compiler_params=_cparams(("arbitrary",)),
        name="moe_combine",
    )(dest3, dest3, ys, x1, weights, mod_row, ln2)


def _route(logits, b_group, b_expert):
    t = logits.shape[0]
    p_group = jax.nn.softmax(logits[:, :N_GROUPS] + b_group, axis=-1)
    group = jnp.argmax(p_group, axis=-1)
    gate_group = jnp.take_along_axis(p_group, group[:, None], axis=-1)
    le = (logits[:, N_GROUPS:N_GROUPS + N_EXPERTS] + b_expert).reshape(t, N_GROUPS, EXPERTS_PER_GROUP)
    le = jnp.take_along_axis(le, group[:, None, None], axis=1)[:, 0]
    top_p, top_i = lax.top_k(jax.nn.softmax(le, axis=-1), 2)
    weights = gate_group * top_p / jnp.sum(top_p, axis=-1, keepdims=True)
    expert_id = group[:, None] * EXPERTS_PER_GROUP + top_i
    return expert_id.astype(jnp.int32), weights


def _moe(tok, logits, b_group, b_expert, wg, wu, wd, x1, mod_row, ln2):
    t = tok.shape[0]
    expert_id, weights = _route(logits, b_group, b_expert)
    e_flat = expert_id.reshape(-1)
    n_assign = e_flat.shape[0]
    onehot = (e_flat[:, None] == jnp.arange(N_EXPERTS)[None, :]).astype(jnp.int32)
    rank = jnp.take_along_axis(jnp.cumsum(onehot, axis=0), e_flat[:, None], axis=1)[:, 0] - 1
    counts = jnp.sum(onehot, axis=0)
    padded = (counts + MOE_BM - 1) // MOE_BM * MOE_BM
    pad_end = jnp.cumsum(padded)
    dest = (pad_end - padded)[e_flat] + rank
    n_blocks = -(-n_assign // MOE_BM) + N_EXPERTS
    n_rows = n_blocks * MOE_BM
    src_tok = jnp.zeros((n_rows,), jnp.int32).at[dest].set(jnp.arange(n_assign, dtype=jnp.int32) // 2)
    block_e = jnp.minimum(jnp.searchsorted(pad_end, jnp.arange(n_blocks) * MOE_BM, side="right"),
                          N_EXPERTS - 1).astype(jnp.int32)
    n_used = (pad_end[-1] // MOE_BM).astype(jnp.int32).reshape(1)
    xs = _gather_call(tok, src_tok, n_used)
    ys = _expert_call(xs, wg, wu, wd, block_e, n_used)
    return _combine_call(ys, dest.reshape(t, 2).astype(jnp.int32), weights, x1, mod_row, ln2)


def _pad_rows(v, rows=SUBLANES):
    return jnp.pad(v, ((0, rows - v.shape[0]), (0, 0)))


def _layer(x, ctx, c, c_ctx, w_ada, b_ada, w_in, conv_qkv, a_log, dt_bias, gdn_norm_w, conv_b,
           w_branch_a, w_branch_b, w_out, ln1_g, ln1_b, w_router_group, b_router_group,
           w_router_expert, b_router_expert, w_exp_gate, w_exp_up, w_exp_down, ln2_g, ln2_b):
    t = x.shape[0]
    mod = _mod_call(_pad_rows(jnp.stack([c, c_ctx])), w_ada, b_ada.reshape(1, -1))
    mod_lat = mod[0].reshape(N_MOD, D_MODEL)
    mod_ctx = mod[1].reshape(N_MOD, D_MODEL)

    n_a = 4 * GDN_WIDTH
    n_ba = 4 * HEADS
    w_pa = w_in[:, :n_a].astype(BF16)
    w_pb = w_in[:, n_a + n_ba:].astype(BF16)
    w_pc = jnp.pad(w_in[:, n_a:n_a + n_ba], ((0, 0), (0, LANES - n_ba))).astype(BF16)
    lane_row = lambda v: jnp.pad(v.reshape(1, -1), ((0, 0), (2 * HEADS, LANES - 4 * HEADS)))
    alog_row, dtb_row = lane_row(a_log), lane_row(dt_bias)

    h_ctx = _ln_mod_call(ctx, mod_ctx[0:1], mod_ctx[1:2])
    tc = ctx.shape[0]
    qkv_c = _inproj_qkv(h_ctx, w_pa, conv_qkv, tc, tc)
    bg_c = _inproj_ba(h_ctx, w_pc, alog_row, dtb_row, tc)
    s0 = jnp.zeros((2, HEADS, HEAD_DIM, HEAD_DIM), F32)
    _, s_ctx = _gdn(qkv_c, bg_c, s0)

    tm = 1024
    h = _ln_mod_call(x, mod_lat[0:1], mod_lat[1:2])
    qkv = _inproj_qkv(h, w_pa, conv_qkv, GRID_W, tm)
    zs = _inproj_z(h, w_pa, gdn_norm_w.reshape(1, HEAD_DIM), tm)
    yconv = _inproj_xbc(h, w_pb, conv_b, GRID_W, tm // 2)
    gates = _inproj_gates(h, w_pb, tm)
    bg = _inproj_ba(h, w_pc, alog_row, dtb_row, tm)
    o, _ = _gdn(qkv, bg, s_ctx)
    m = _merge_call(o, zs, yconv, gates, w_branch_a.astype(BF16), w_branch_b.astype(BF16))

    w_router = jnp.pad(jnp.concatenate([w_router_group, w_router_expert], axis=1),
                       ((0, 0), (0, LANES - N_GROUPS - N_EXPERTS)))
    mix = _outmm_call(m, w_out.astype(BF16))
    x1, tok, logits = _post_mix_call(mix, x, _pad_rows(mod_lat[2:5]),
                                     _pad_rows(jnp.stack([ln1_g, ln1_b])), w_router)

    return _moe(tok, logits, b_router_group, b_router_expert, w_exp_gate, w_exp_up, w_exp_down,
                x1, _pad_rows(mod_lat[5:6]), _pad_rows(jnp.stack([ln2_g, ln2_b])))


def kernel(x, c, ctx, c_ctx, w_ada, b_ada, w_in, conv_qkv, a_log, dt_bias, gdn_norm_w, conv_b,
           w_branch_a, w_branch_b, w_out, ln1_g, ln1_b, w_router_group, b_router_group,
           w_router_expert, b_router_expert, w_exp_gate, w_exp_up, w_exp_down, ln2_g, ln2_b):
    assert x.shape[0] == 1 and w_ada.shape[0] == 1, "single batch element, single layer"
    out = _layer(x[0], ctx[0], c[0], c_ctx, w_ada[0], b_ada[0], w_in[0], conv_qkv[0],
                 a_log[0].reshape(-1), dt_bias[0].reshape(-1), gdn_norm_w[0], conv_b[0],
                 w_branch_a[0], w_branch_b[0], w_out[0], ln1_g[0], ln1_b[0],
                 w_router_group[0], b_router_group[0], w_router_expert[0], b_router_expert[0],
                 w_exp_gate[0], w_exp_up[0], w_exp_down[0], ln2_g[0], ln2_b[0])
    return out[None]
```
